```python
import math
import jax, jax.numpy as jnp
from jax import lax
import numpy as np

D_MODEL = 1024
BATCH = 4
SEQ = 4096
DEPTH = 1
DEC_BATCH = 32
DEC_SEQ = 1
PAST_LEN = 16384
PAGE_SIZE = 128

MIX_WIDTH = D_MODEL
HG_WIDTH = MIX_WIDTH // 2
HG_EXPAND = 128
HG_HEADS = HG_WIDTH // HG_EXPAND
HG_VDIM = HG_WIDTH // HG_HEADS
HG_CHUNK = 64
DA_WIDTH = MIX_WIDTH - HG_WIDTH
DA_HEAD_DIM = 64
DA_HEADS = DA_WIDTH // (2 * DA_HEAD_DIM)
DA_VDIM = 2 * DA_HEAD_DIM
DA_SCALE = DA_HEAD_DIM ** -0.5
Q_BLOCK = 128
D_FF = 2816
NORM_EPS = 1e-6
NEG_INF = -1e30
IN_SPLITS = (HG_WIDTH, 2 * HG_WIDTH, 3 * HG_WIDTH, 4 * HG_WIDTH,
             4 * HG_WIDTH + DA_WIDTH, 4 * HG_WIDTH + 2 * DA_WIDTH)
IN_COLS = 4 * HG_WIDTH + 3 * DA_WIDTH

kernel_name = 'hymba_hgrn2_diffattn_macaron_step'


def rmsnorm(x, g):
    xf = x.astype(jnp.float32)
    y = xf * lax.rsqrt(jnp.mean(xf * xf, axis=-1, keepdims=True) + NORM_EPS)
    return (y * g.astype(jnp.float32)).astype(x.dtype)


def swiglu(h, w_gate, w_up, w_down):
    return (jax.nn.silu(h @ w_gate) * (h @ w_up)) @ w_down


def alibi_slopes(n_heads):
    return jnp.asarray(np.power(2.0, -8.0 * np.arange(1, n_heads + 1) / n_heads), dtype=jnp.float32)


def hgrn2_chunked(q, k, v, logf, s0):
    B, L, H, DK = q.shape
    C = HG_CHUNK if L % HG_CHUNK == 0 else L
    n = L // C

    def to_chunks(a):
        return a.reshape(B, n, C, H, a.shape[-1]).transpose(1, 0, 3, 2, 4)

    qc, kc, vc, gc = to_chunks(q), to_chunks(k), to_chunks(v), to_chunks(logf)
    causal = jnp.tril(jnp.ones((C, C), dtype=bool))

    def step(S, inp):
        qb, kb, vb, gb = inp
        b = jnp.cumsum(gb, axis=2)
        diff = b[:, :, :, None, :] - b[:, :, None, :, :]
        decay = jnp.where(causal[:, :, None], jnp.exp(jnp.minimum(diff, 0.0)), 0.0)
        A = jnp.einsum('bhtd,bhsd,bhtsd->bhts', qb, kb, decay)
        o = (jnp.einsum('bhts,bhsv->bhtv', A, vb)
             + jnp.einsum('bhtd,bhdv->bhtv', qb * jnp.exp(b), S))
        b_last = b[:, :, -1:, :]
        S_new = (jnp.exp(b_last[:, :, 0, :])[..., None] * S
                 + jnp.einsum('bhsd,bhsv->bhdv', kb * jnp.exp(b_last - b), vb))
        return S_new, o

    S_fin, oc = lax.scan(step, s0, (qc, kc, vc, gc))
    o = oc.transpose(1, 0, 3, 2, 4).reshape(B, L, H, v.shape[-1])
    return o, S_fin


def hgrn2_mixer(xq, xf, xi, xg, lb, s0, g_out):
    B, L, _ = xq.shape

    def heads(a):
        return a.astype(jnp.float32).reshape(B, L, HG_HEADS, -1)

    q = jax.nn.silu(heads(xq)) * (HG_EXPAND ** -0.5)
    f = lb + (1.0 - lb) * jax.nn.sigmoid(heads(xf))
    i = heads(xi)
    o, s_new = hgrn2_chunked(q, 1.0 - f, i, jnp.log(f), s0.astype(jnp.float32))
    o = rmsnorm(o, g_out) * jax.nn.silu(heads(xg))
    return o, s_new


def diff_attention(q, q_pos, segments, lam):
    B, Lq = q.shape[0], q.shape[1]
    slopes = jnp.repeat(alibi_slopes(DA_HEADS), 2)[:, None, None]
    qf = q.astype(jnp.float32)
    scores = []
    lens = []
    for k, _, k_pos in segments:
        s = jnp.einsum('bqnd,bknd->bnqk', qf, k.astype(jnp.float32)) * DA_SCALE
        rel = (q_pos[:, None] - k_pos[None, :]).astype(jnp.float32)
        scores.append(jnp.where(rel >= 0, s - slopes * rel, NEG_INF))
        lens.append(k.shape[1])
    p = jax.nn.softmax(jnp.concatenate(scores, axis=-1), axis=-1)
    p = p.reshape(B, DA_HEADS, 2, Lq, p.shape[-1])
    a = p[:, :, 0] - lam * p[:, :, 1]
    outs = []
    start = 0
    for (_, v, _), n in zip(segments, lens):
        outs.append(jnp.einsum('bhqk,bkhv->bqhv', a[..., start:start + n], v.astype(jnp.float32)))
        start += n
    out = outs[0]
    for extra in outs[1:]:
        out = out + extra
    return out


def prompt_attention(q, k, v, lam):
    B, L = q.shape[0], q.shape[1]
    blk = Q_BLOCK if L % Q_BLOCK == 0 else L
    nb = L // blk
    k_pos = jnp.arange(L, dtype=jnp.int32)
    qb = q.reshape(B, nb, blk, 2 * DA_HEADS, DA_HEAD_DIM).transpose(1, 0, 2, 3, 4)

    def one_block(args):
        q_blk, b_idx = args
        q_pos = b_idx * blk + jnp.arange(blk, dtype=jnp.int32)
        return diff_attention(q_blk, q_pos, ((k, v, k_pos),), lam)

    o = lax.map(one_block, (qb, jnp.arange(nb, dtype=jnp.int32)))
    return o.transpose(1, 0, 2, 3, 4).reshape(B, L, DA_HEADS, DA_VDIM)


def sample_attention(q, k, v, past_k, past_v, lam):
    L = q.shape[1]
    past_len = past_k.shape[1]
    past_pos = jnp.arange(past_len, dtype=jnp.int32)
    new_pos = past_len + jnp.arange(L, dtype=jnp.int32)
    return diff_attention(q, new_pos, ((past_k, past_v, past_pos), (k, v, new_pos)), lam)


def decoder_layer(x, s0, past_k, past_v, lb, lam, lam_init, p):
    B, L, _ = x.shape
    dt = x.dtype
    h = x + 0.5 * swiglu(rmsnorm(x, p['ffn1_norm']), p['ffn1_w_gate'], p['ffn1_w_up'], p['ffn1_w_down'])
    z = rmsnorm(h, p['mix_norm']) @ p['w_in']
    xq, xf, xi, xg, dq, dk, dv = jnp.split(z, IN_SPLITS, axis=-1)
    o_hg, s_new = hgrn2_mixer(xq, xf, xi, xg, lb, s0, p['hg_out_norm'])
    q = rmsnorm(dq.reshape(B, L, 2 * DA_HEADS, DA_HEAD_DIM), p['da_q_norm'])
    k = rmsnorm(dk.reshape(B, L, 2 * DA_HEADS, DA_HEAD_DIM), p['da_k_norm'])
    v = dv.reshape(B, L, DA_HEADS, DA_VDIM)
    if past_k is None:
        o_da = prompt_attention(q, k, v, lam)
    else:
        o_da = sample_attention(q, k, v, past_k, past_v, lam)
    o_da = rmsnorm(o_da, p['da_subln']) * (1.0 - lam_init)
    o = jnp.concatenate([o_hg.reshape(B, L, HG_WIDTH), o_da.reshape(B, L, DA_WIDTH)], axis=-1).astype(dt)
    h = h + o @ p['w_out']
    y = h + 0.5 * swiglu(rmsnorm(h, p['ffn2_norm']), p['ffn2_w_gate'], p['ffn2_w_up'], p['ffn2_w_down'])
    return y, k, v, s_new


def setup_inputs(seed: int = 0) -> dict:
    key = jax.random.key(seed)
    ks = jax.random.split(key, 32)
    f32 = jnp.float32
    n_pages = PAST_LEN // PAGE_SIZE
    n_used = DEC_BATCH * n_pages
    n_pool = n_used + max(n_used // 4, 1)

    def nrm(k, shape, scale):
        return jax.random.normal(k, shape, f32) * scale

    def gain(k, shape):
        return 1.0 + 0.05 * jax.random.normal(k, shape, f32)

    page_table = jax.random.permutation(ks[5], n_pool)[:n_used].reshape(DEC_BATCH, n_pages).astype(jnp.int32)
    return {
        'x_prompt': nrm(ks[0], (BATCH, SEQ, D_MODEL), 1.0),
        'x_sample': nrm(ks[1], (DEC_BATCH, DEC_SEQ, D_MODEL), 1.0),
        'cache_k': nrm(ks[2], (DEPTH, n_pool, PAGE_SIZE, 2 * DA_HEADS, DA_HEAD_DIM), 1.0),
        'cache_v': nrm(ks[3], (DEPTH, n_pool, PAGE_SIZE, DA_HEADS, DA_VDIM), 1.0),
        'state_hgrn': nrm(ks[4], (DEPTH, DEC_BATCH, HG_HEADS, HG_EXPAND, HG_VDIM), 0.5),
        'page_table': page_table,
        'ffn1_norm': gain(ks[6], (DEPTH, D_MODEL)),
        'ffn1_w_gate': nrm(ks[7], (DEPTH, D_MODEL, D_FF), D_MODEL ** -0.5),
        'ffn1_w_up': nrm(ks[8], (DEPTH, D_MODEL, D_FF), D_MODEL ** -0.5),
        'ffn1_w_down': nrm(ks[9], (DEPTH, D_FF, D_MODEL), D_FF ** -0.5),
        'mix_norm': gain(ks[10], (DEPTH, D_MODEL)),
        'w_in': nrm(ks[11], (DEPTH, D_MODEL, IN_COLS), D_MODEL ** -0.5),
        'hg_lb_logits': nrm(ks[12], (DEPTH + 1, HG_WIDTH), 0.5),
        'hg_out_norm': gain(ks[13], (DEPTH, HG_VDIM)),
        'da_q_norm': gain(ks[14], (DEPTH, DA_HEAD_DIM)),
        'da_k_norm': gain(ks[15], (DEPTH, DA_HEAD_DIM)),
        'da_lambda_q1': nrm(ks[16], (DEPTH, DA_HEAD_DIM), 0.1),
        'da_lambda_k1': nrm(ks[17], (DEPTH, DA_HEAD_DIM), 0.1),
        'da_lambda_q2': nrm(ks[18], (DEPTH, DA_HEAD_DIM), 0.1),
        'da_lambda_k2': nrm(ks[19], (DEPTH, DA_HEAD_DIM), 0.1),
        'da_subln': gain(ks[20], (DEPTH, DA_VDIM)),
        'w_out': nrm(ks[21], (DEPTH, MIX_WIDTH, D_MODEL), MIX_WIDTH ** -0.5),
        'ffn2_norm': gain(ks[22], (DEPTH, D_MODEL)),
        'ffn2_w_gate': nrm(ks[23], (DEPTH, D_MODEL, D_FF), D_MODEL ** -0.5),
        'ffn2_w_up': nrm(ks[24], (DEPTH, D_MODEL, D_FF), D_MODEL ** -0.5),
        'ffn2_w_down': nrm(ks[25], (DEPTH, D_FF, D_MODEL), D_FF ** -0.5),
    }


def reference(x_prompt, x_sample, cache_k, cache_v, state_hgrn, page_table,
              ffn1_norm, ffn1_w_gate, ffn1_w_up, ffn1_w_down, mix_norm, w_in,
              hg_lb_logits, hg_out_norm, da_q_norm, da_k_norm,
              da_lambda_q1, da_lambda_k1, da_lambda_q2, da_lambda_k2, da_subln, w_out,
              ffn2_norm, ffn2_w_gate, ffn2_w_up, ffn2_w_down):
    n_batch = x_prompt.shape[0]
    dec_batch = x_sample.shape[0]
    lb_all = jnp.cumsum(jax.nn.softmax(hg_lb_logits.astype(jnp.float32), axis=0), axis=0)
    yp, ys = x_prompt, x_sample
    kp_l, vp_l, sp_l, ks_l, vs_l, ss_l = [], [], [], [], [], []
    for l in range(DEPTH):
        p = {
            'ffn1_norm': ffn1_norm[l], 'ffn1_w_gate': ffn1_w_gate[l], 'ffn1_w_up': ffn1_w_up[l],
            'ffn1_w_down': ffn1_w_down[l], 'mix_norm': mix_norm[l], 'w_in': w_in[l],
            'hg_out_norm': hg_out_norm[l], 'da_q_norm': da_q_norm[l], 'da_k_norm': da_k_norm[l],
            'da_subln': da_subln[l], 'w_out': w_out[l], 'ffn2_norm': ffn2_norm[l],
            'ffn2_w_gate': ffn2_w_gate[l], 'ffn2_w_up': ffn2_w_up[l], 'ffn2_w_down': ffn2_w_down[l],
        }
        lam_init = 0.8 - 0.6 * math.exp(-0.3 * l)
        lam = (jnp.exp(jnp.sum(da_lambda_q1[l].astype(jnp.float32) * da_lambda_k1[l].astype(jnp.float32)))
               - jnp.exp(jnp.sum(da_lambda_q2[l].astype(jnp.float32) * da_lambda_k2[l].astype(jnp.float32)))
               + lam_init)
        lb = lb_all[l].reshape(HG_HEADS, HG_EXPAND)
        s0 = jnp.zeros((n_batch, HG_HEADS, HG_EXPAND, HG_VDIM), jnp.float32)
        yp, kp, vp, sp = decoder_layer(yp, s0, None, None, lb, lam, lam_init, p)
        past_k = cache_k[l][page_table].reshape(dec_batch, -1, 2 * DA_HEADS, DA_HEAD_DIM)
        past_v = cache_v[l][page_table].reshape(dec_batch, -1, DA_HEADS, DA_VDIM)
        ys, ksn, vsn, ssn = decoder_layer(ys, state_hgrn[l], past_k, past_v, lb, lam, lam_init, p)
        kp_l.append(kp); vp_l.append(vp); sp_l.append(sp)
        ks_l.append(ksn); vs_l.append(vsn); ss_l.append(ssn)
    k_prompt = jnp.stack(kp_l, axis=0)
    v_prompt = jnp.stack(vp_l, axis=0)
    state_prompt = jnp.stack(sp_l, axis=0)
    k_sample = jnp.stack(ks_l, axis=0)
    v_sample = jnp.stack(vs_l, axis=0)
    state_sample = jnp.stack(ss_l, axis=0)
    return (yp, ys, k_prompt, v_prompt, state_prompt, k_sample, v_sample, state_sample)
```

```python
import functools
import math

import jax
import jax.numpy as jnp
from jax import lax
from jax.experimental import pallas as pl
from jax.experimental.pallas import tpu as pltpu

F32 = jnp.float32
BF16 = jnp.bfloat16

NORM_EPS = 1e-6
NEG_INF = -1e30
LANES = 128
SUBLANES = 8
HG_HEADS = 4
HG_DK = 128
HG_DV = 128
HG_WIDTH = HG_HEADS * HG_DK
DA_HEADS = 4
DA_DH = 64
DA_DV = 128
DA_WIDTH = 2 * DA_HEADS * DA_DH
DA_SCALE = DA_DH ** -0.5
HG_COLS = 4 * HG_WIDTH
VMEM_LIMIT = 56 * 1024 * 1024

_NT = (((1,), (1,)), ((), ()))
_TN = (((0,), (0,)), ((), ()))


def _rmsnorm(x, g):
    ms = jnp.mean(x * x, axis=-1, keepdims=True)
    return x * lax.rsqrt(ms + NORM_EPS) * g


def _silu(x):
    return x * jax.nn.sigmoid(x)


def _split3(x):
    hi = x.astype(BF16)
    r1 = x - hi.astype(F32)
    mid = r1.astype(BF16)
    lo = (r1 - mid.astype(F32)).astype(BF16)
    return hi, mid, lo


def _dot_exact_rhs(sel, x):
    hi, mid, lo = _split3(x)
    return (jnp.dot(sel, hi, preferred_element_type=F32)
            + jnp.dot(sel, mid, preferred_element_type=F32)
            + jnp.dot(sel, lo, preferred_element_type=F32))


def _dot_exact_lhs(x, sel):
    hi, mid, lo = _split3(x)
    return (jnp.dot(hi, sel, preferred_element_type=F32)
            + jnp.dot(mid, sel, preferred_element_type=F32)
            + jnp.dot(lo, sel, preferred_element_type=F32))


def _col_bcast(row, width=LANES):
    n = row.shape[1]
    r = lax.broadcasted_iota(jnp.int32, (n, n), 0)
    c = lax.broadcasted_iota(jnp.int32, (n, n), 1)
    diag = jnp.where(r == c, jnp.broadcast_to(row, (n, n)), 0.0)
    return _dot_exact_lhs(diag, jnp.ones((n, width), BF16))


def _group64_rmsnorm(x, g):
    m, w = x.shape
    r = lax.broadcasted_iota(jnp.int32, (LANES, LANES), 0) // DA_DH
    c = lax.broadcasted_iota(jnp.int32, (LANES, LANES), 1) // DA_DH
    same = jnp.where(r == c, 1.0, 0.0).astype(BF16)
    x2 = x * x
    parts = [_dot_exact_lhs(x2[:, i * LANES:(i + 1) * LANES], same) for i in range(w // LANES)]
    ss = jnp.concatenate(parts, axis=-1)
    return x * lax.rsqrt(ss * (1.0 / DA_DH) + NORM_EPS) * g


def _swiglu(xn, wg_ref, wu_ref, wd_ref):
    g = jnp.dot(xn, wg_ref[...], preferred_element_type=F32)
    u = jnp.dot(xn, wu_ref[...], preferred_element_type=F32)
    a = (_silu(g) * u).astype(BF16)
    return jnp.dot(a, wd_ref[...], preferred_element_type=F32)


def _lower_bound(logits, layer):
    m = jnp.max(logits, axis=0, keepdims=True)
    e = jnp.exp(logits - m)
    return jnp.sum(e[:layer + 1], axis=0, keepdims=True) / jnp.sum(e, axis=0, keepdims=True)


def _lambda(lq1_ref, lk1_ref, lq2_ref, lk2_ref, lam_init):
    a = jnp.sum(lq1_ref[...] * lk1_ref[...], axis=-1, keepdims=True)
    b = jnp.sum(lq2_ref[...] * lk2_ref[...], axis=-1, keepdims=True)
    return jnp.exp(a) - jnp.exp(b) + lam_init


def _ffn_inproj_kernel(x_ref, g1_ref, wg_ref, wu_ref, wd_ref, gm_ref, win_ref, gq_ref, gk_ref,
                       h_ref, zhg_ref, q_ref, k_ref, kb_ref, v_ref, vb_ref):
    x = x_ref[...]
    h = x + 0.5 * _swiglu(_rmsnorm(x, g1_ref[...]).astype(BF16), wg_ref, wu_ref, wd_ref)
    h_ref[...] = h
    z = jnp.dot(_rmsnorm(h, gm_ref[...]).astype(BF16), win_ref[...], preferred_element_type=F32)
    zhg_ref[...] = z[:, :HG_COLS]
    dq = z[:, HG_COLS:HG_COLS + DA_WIDTH]
    dk = z[:, HG_COLS + DA_WIDTH:HG_COLS + 2 * DA_WIDTH]
    dv = z[:, HG_COLS + 2 * DA_WIDTH:]
    q_ref[...] = (_group64_rmsnorm(dq, gq_ref[...]) * DA_SCALE).astype(BF16)
    k = _group64_rmsnorm(dk, gk_ref[...])
    k_ref[...] = k
    kb_ref[...] = k.astype(BF16)
    v_ref[...] = dv
    vb_ref[...] = dv.astype(BF16)


def _resident(shape):
    return pl.BlockSpec(shape, lambda *_: (0,) * len(shape), pipeline_mode=pl.Buffered(1))


def _ffn_inproj(x, g1, wg, wu, wd, gm, win, gq, gk, tm):
    n, d = x.shape
    dff = wg.shape[1]
    rows = lambda w: pl.BlockSpec((tm, w), lambda i: (i, 0))
    outs = [(d, F32), (HG_COLS, F32), (DA_WIDTH, BF16), (DA_WIDTH, F32), (DA_WIDTH, BF16),
            (DA_WIDTH, F32), (DA_WIDTH, BF16)]
    return pl.pallas_call(
        _ffn_inproj_kernel,
        grid=(n // tm,),
        in_specs=[rows(d), _resident((1, d)), _resident((d, dff)), _resident((d, dff)),
                  _resident((dff, d)), _resident((1, d)), _resident(win.shape),
                  _resident((1, DA_WIDTH)), _resident((1, DA_WIDTH))],
        out_specs=[rows(w) for w, _ in outs],
        out_shape=[jax.ShapeDtypeStruct((n, w), dt) for w, dt in outs],
        compiler_params=pltpu.CompilerParams(dimension_semantics=("arbitrary",),
                                             vmem_limit_bytes=VMEM_LIMIT),
        name="ffn_inproj",
    )(x, g1, wg, wu, wd, gm, win, gq, gk)


def _outproj_ffn_kernel(h_ref, ohg_ref, oda_ref, wo_ref, g2_ref, wg_ref, wu_ref, wd_ref, y_ref):
    h = (h_ref[...]
         + jnp.dot(ohg_ref[...], wo_ref[:HG_WIDTH, :], preferred_element_type=F32)
         + jnp.dot(oda_ref[...], wo_ref[HG_WIDTH:, :], preferred_element_type=F32))
    y_ref[...] = h + 0.5 * _swiglu(_rmsnorm(h, g2_ref[...]).astype(BF16), wg_ref, wu_ref, wd_ref)


def _outproj_ffn(h, ohg, oda, wo, g2, wg, wu, wd, tm):
    n, d = h.shape
    dff = wg.shape[1]
    rows = lambda w: pl.BlockSpec((tm, w), lambda i: (i, 0))
    return pl.pallas_call(
        _outproj_ffn_kernel,
        grid=(n // tm,),
        in_specs=[rows(d), rows(HG_WIDTH), rows(DA_WIDTH), _resident(wo.shape), _resident((1, d)),
                  _resident((d, dff)), _resident((d, dff)), _resident((dff, d))],
        out_specs=rows(d),
        out_shape=jax.ShapeDtypeStruct((n, d), F32),
        compiler_params=pltpu.CompilerParams(dimension_semantics=("arbitrary",),
                                             vmem_limit_bytes=VMEM_LIMIT),
        name="outproj_ffn",
    )(h, ohg, oda, wo, g2, wg, wu, wd)


HG_CHUNK = 128
HG_BLOCK = SUBLANES


def _hgrn_gates(z, lb):
    xq = z[:, :HG_WIDTH]
    xf = z[:, HG_WIDTH:2 * HG_WIDTH]
    xi = z[:, 2 * HG_WIDTH:3 * HG_WIDTH]
    xg = z[:, 3 * HG_WIDTH:]
    q = _silu(xq) * (HG_DK ** -0.5)
    f = lb + (1.0 - lb) * jax.nn.sigmoid(xf)
    return q, 1.0 - f, f, xi, _silu(xg)


def _hgrn_chunk_kernel(z_ref, lbl_ref, gout_ref, o_ref, sfin_ref, st_ref, *, layer):
    c = pl.program_id(1)
    n = z_ref.shape[0]

    @pl.when(c == 0)
    def _():
        st_ref[...] = jnp.zeros_like(st_ref)

    lb = _lower_bound(lbl_ref[...], layer)
    q, k, f, v, gate = _hgrn_gates(z_ref[...], lb)
    row = lax.broadcasted_iota(jnp.int32, (n, n), 0)
    col = lax.broadcasted_iota(jnp.int32, (n, n), 1)
    tri = jnp.where(row >= col, 1.0, 0.0).astype(BF16)
    b = _dot_exact_rhs(tri, jnp.log(f))
    trow = lax.broadcasted_iota(jnp.int32, (n, 1), 0)

    a = [jnp.zeros((n, n), F32) for _ in range(HG_HEADS)]
    w = HG_BLOCK
    while 2 * w <= n:
        span = 2 * w
        ref = jnp.concatenate(
            [jnp.broadcast_to(b[p * span + w - 1:p * span + w, :], (span, HG_WIDTH))
             for p in range(n // span)], axis=0)
        e = jnp.exp(-jnp.abs(b - ref))
        right = (trow % span) >= w
        qw = jnp.where(right, q * e, 0.0).astype(BF16)
        kw = jnp.where(right, 0.0, k * e).astype(BF16)
        lvl = ((row // span) == (col // span)) & ((row % span) >= w) & ((col % span) < w)
        for h in range(HG_HEADS):
            hs = slice(h * HG_DK, (h + 1) * HG_DK)
            p = lax.dot_general(qw[:, hs], kw[:, hs], _NT, preferred_element_type=F32)
            a[h] = jnp.where(lvl, p, a[h])
        w = span
    for d in range(HG_BLOCK):
        kd = pltpu.roll(k, d, axis=0) if d else k
        bd = pltpu.roll(b, d, axis=0) if d else b
        p = q * kd * jnp.exp(jnp.minimum(b - bd, 0.0))
        near = ((row - col) == d) & ((row % HG_BLOCK) >= d)
        for h in range(HG_HEADS):
            hs = slice(h * HG_DK, (h + 1) * HG_DK)
            a[h] = jnp.where(near, jnp.sum(p[:, hs], axis=-1, keepdims=True), a[h])

    b_last = b[n - 1:n, :]
    q_in = (q * jnp.exp(b)).astype(BF16)
    k_out = (k * jnp.exp(b_last - b)).astype(BF16)
    carry = jnp.exp(b_last)
    vb = v.astype(BF16)
    for h in range(HG_HEADS):
        hs = slice(h * HG_DK, (h + 1) * HG_DK)
        st = st_ref[h]
        o = (jnp.dot(a[h].astype(BF16), vb[:, hs], preferred_element_type=F32)
             + lax.dot_general(q_in[:, hs], st.astype(BF16), _NT, preferred_element_type=F32))
        st_new = st * carry[:, hs] + lax.dot_general(vb[:, hs], k_out[:, hs], _TN,
                                                     preferred_element_type=F32)
        st_ref[h] = st_new
        o_ref[:, hs] = (_rmsnorm(o, gout_ref[...]) * gate[:, hs]).astype(BF16)

        @pl.when(c == pl.num_programs(1) - 1)
        def _():
            sfin_ref[0, h] = st_new.T


def _hgrn_prompt(zhg, lb_logits, gout, batch, layer):
    n = zhg.shape[0]
    nc = n // batch // HG_CHUNK
    return pl.pallas_call(
        functools.partial(_hgrn_chunk_kernel, layer=layer),
        grid=(batch, nc),
        in_specs=[pl.BlockSpec((HG_CHUNK, HG_COLS), lambda b, c: (b * nc + c, 0)),
                  pl.BlockSpec(lb_logits.shape, lambda b, c: (0, 0)),
                  pl.BlockSpec((1, HG_DV), lambda b, c: (0, 0))],
        out_specs=[pl.BlockSpec((HG_CHUNK, HG_WIDTH), lambda b, c: (b * nc + c, 0)),
                   pl.BlockSpec((1, HG_HEADS, HG_DK, HG_DV), lambda b, c: (b, 0, 0, 0))],
        out_shape=[jax.ShapeDtypeStruct((n, HG_WIDTH), BF16),
                   jax.ShapeDtypeStruct((batch, HG_HEADS, HG_DK, HG_DV), F32)],
        scratch_shapes=[pltpu.VMEM((HG_HEADS, HG_DV, HG_DK), F32)],
        compiler_params=pltpu.CompilerParams(dimension_semantics=("arbitrary", "arbitrary")),
        name="hgrn_prompt",
    )(zhg, lb_logits, gout)


def _hgrn_step_kernel(z_ref, s_ref, lbl_ref, gout_ref, o_ref, snew_ref, *, layer):
    lb = _lower_bound(lbl_ref[...], layer)
    q, k, f, v, gate = _hgrn_gates(z_ref[0], lb)
    outs = []
    for h in range(HG_HEADS):
        hs = slice(h * HG_DK, (h + 1) * HG_DK)
        s_new = _col_bcast(f[:, hs]) * s_ref[0, h] + _col_bcast(k[:, hs]) * v[:, hs]
        snew_ref[0, h] = s_new
        o = jnp.sum(_col_bcast(q[:, hs]) * s_new, axis=0, keepdims=True)
        outs.append(_rmsnorm(o, gout_ref[...]) * gate[:, hs])
    o_ref[0] = jnp.concatenate(outs, axis=-1).astype(BF16)


def _hgrn_sample(zhg, state, lb_logits, gout, layer):
    nb = zhg.shape[0]
    return pl.pallas_call(
        functools.partial(_hgrn_step_kernel, layer=layer),
        grid=(nb,),
        in_specs=[pl.BlockSpec((1, 1, HG_COLS), lambda b: (b, 0, 0)),
                  pl.BlockSpec((1, HG_HEADS, HG_DK, HG_DV), lambda b: (b, 0, 0, 0)),
                  pl.BlockSpec(lb_logits.shape, lambda b: (0, 0)),
                  pl.BlockSpec((1, HG_DV), lambda b: (0, 0))],
        out_specs=[pl.BlockSpec((1, 1, HG_WIDTH), lambda b: (b, 0, 0)),
                   pl.BlockSpec((1, HG_HEADS, HG_DK, HG_DV), lambda b: (b, 0, 0, 0))],
        out_shape=[jax.ShapeDtypeStruct((nb, 1, HG_WIDTH), BF16),
                   jax.ShapeDtypeStruct(state.shape, F32)],
        compiler_params=pltpu.CompilerParams(dimension_semantics=("arbitrary",)),
        name="hgrn_sample",
    )(zhg.reshape(nb, 1, HG_COLS), state, lb_logits, gout)


ATT_TILE = 256


def _head_slope(h):
    return jnp.exp2(jnp.full((1, 1), -8.0 / DA_HEADS, F32) * (h + 1).astype(F32))


def _diff_combine(acc1, l1, acc2, l2, lam, gsub, lam_init):
    out = acc1 / l1 - lam * (acc2 / l2)
    return _rmsnorm(out, gsub) * (1.0 - lam_init)


def _attn_prompt_kernel(q_ref, k_ref, v_ref, lq1_ref, lk1_ref, lq2_ref, lk2_ref, gsub_ref, o_ref,
                        m_ref, l_ref, acc_ref, *, lam_init):
    h = pl.program_id(1)
    qi = pl.program_id(2)
    t = q_ref.shape[0]
    slope = _head_slope(h)
    q = q_ref[...]
    lane = lax.broadcasted_iota(jnp.int32, (1, 2 * DA_DH), 1)
    zero = jnp.zeros_like(q)
    qs = (jnp.where(lane < DA_DH, q, zero), jnp.where(lane >= DA_DH, q, zero))
    base = (lax.broadcasted_iota(jnp.int32, (t, t), 0)
            - lax.broadcasted_iota(jnp.int32, (t, t), 1)).astype(F32)
    m_ref[...] = jnp.full_like(m_ref, NEG_INF)
    l_ref[...] = jnp.zeros_like(l_ref)
    acc_ref[...] = jnp.zeros_like(acc_ref)

    def tile(ki, masked):
        start = pl.multiple_of(ki * t, t)
        kt = k_ref[pl.ds(start, t), :]
        vt = v_ref[pl.ds(start, t), :]
        rel = base + ((qi - ki) * t).astype(F32)
        bias = -slope * rel
        for i in range(2):
            s = lax.dot_general(qs[i], kt, _NT, preferred_element_type=F32) + bias
            if masked:
                s = jnp.where(base >= 0.0, s, NEG_INF)
            m_old = m_ref[i]
            m_new = jnp.maximum(m_old, jnp.max(s, axis=-1, keepdims=True))
            alpha = jnp.exp(m_old - m_new)
            p = jnp.exp(s - m_new)
            l_ref[i] = alpha * l_ref[i] + jnp.sum(p, axis=-1, keepdims=True)
            acc_ref[i] = alpha * acc_ref[i] + jnp.dot(p.astype(BF16), vt,
                                                      preferred_element_type=F32)
            m_ref[i] = m_new

    def body(ki, carry):
        tile(ki, False)
        return carry

    lax.fori_loop(0, qi, body, 0)
    tile(qi, True)
    lam = _lambda(lq1_ref, lk1_ref, lq2_ref, lk2_ref, lam_init)
    o_ref[...] = _diff_combine(acc_ref[0], l_ref[0], acc_ref[1], l_ref[1], lam, gsub_ref[...],
                               lam_init).astype(BF16)


def _attn_prompt(qb, kb, vb, lams, gsub, batch, lam_init):
    n = qb.shape[0]
    seq = n // batch
    nq = seq // ATT_TILE
    small = lambda a: pl.BlockSpec(a.shape, lambda b, h, i: (0, 0))
    return pl.pallas_call(
        functools.partial(_attn_prompt_kernel, lam_init=lam_init),
        grid=(batch, DA_HEADS, nq),
        in_specs=[pl.BlockSpec((ATT_TILE, 2 * DA_DH), lambda b, h, i: (b * nq + i, h)),
                  pl.BlockSpec((seq, 2 * DA_DH), lambda b, h, i: (b, h)),
                  pl.BlockSpec((seq, DA_DV), lambda b, h, i: (b, h))]
                 + [small(a) for a in lams] + [small(gsub)],
        out_specs=pl.BlockSpec((ATT_TILE, DA_DV), lambda b, h, i: (b * nq + i, h)),
        out_shape=jax.ShapeDtypeStruct((n, DA_HEADS * DA_DV), BF16),
        scratch_shapes=[pltpu.VMEM((2, ATT_TILE, 1), F32), pltpu.VMEM((2, ATT_TILE, 1), F32),
                        pltpu.VMEM((2, ATT_TILE, DA_DV), F32)],
        compiler_params=pltpu.CompilerParams(
            dimension_semantics=("arbitrary", "arbitrary", "arbitrary")),
        name="attn_prompt",
    )(qb, kb, vb, *lams, gsub)


PAGES_PER_STEP = 8


def _attn_sample_kernel(pt_ref, q_ref, kn_ref, vn_ref, lq1_ref, lk1_ref, lq2_ref, lk2_ref,
                        gsub_ref, *refs, lam_init, past_len):
    del pt_ref
    k_refs = refs[:PAGES_PER_STEP]
    v_refs = refs[PAGES_PER_STEP:2 * PAGES_PER_STEP]
    o_ref, qcol_ref, m_ref, l_ref, acc_ref = refs[2 * PAGES_PER_STEP:]
    j = pl.program_id(1)
    nmap = 2 * DA_HEADS
    page = k_refs[0].shape[-1]
    mrow = lax.broadcasted_iota(jnp.int32, (nmap, 1), 0)
    slope = jnp.exp2((-8.0 / DA_HEADS) * ((mrow // 2) + 1).astype(F32))

    @pl.when(j == 0)
    def _():
        q = q_ref[0].astype(F32)
        qcol = _col_bcast(q, page)
        qcol_ref[...] = qcol
        kcol = _col_bcast(kn_ref[0], page)
        s_new = jnp.sum((qcol * kcol).reshape(nmap, DA_DH, page), axis=1)
        m_ref[...] = s_new
        l_ref[...] = jnp.ones_like(l_ref)
        vn = vn_ref[0]
        acc_ref[...] = jnp.concatenate(
            [vn[:, (r // 2) * DA_DV:(r // 2 + 1) * DA_DV] for r in range(nmap)], axis=0)

    qcol = qcol_ref[...].reshape(nmap, DA_DH, page)
    lane = lax.broadcasted_iota(jnp.int32, (1, page), 1)
    scores = []
    for i in range(PAGES_PER_STEP):
        s = jnp.sum(k_refs[i][0, 0] * qcol, axis=1)
        pos = (j * PAGES_PER_STEP + i) * page + lane
        scores.append(s - slope * (past_len - pos).astype(F32))
    m_old = m_ref[...]
    m_new = m_old
    for s in scores:
        m_new = jnp.maximum(m_new, jnp.max(s, axis=-1, keepdims=True))
    alpha = jnp.exp(m_old - m_new)
    l_new = alpha * l_ref[...]
    acc = alpha * acc_ref[...]
    for i, s in enumerate(scores):
        p = jnp.exp(s - m_new)
        l_new = l_new + jnp.sum(p, axis=-1, keepdims=True)
        pb = p.astype(BF16)
        for h in range(DA_HEADS):
            vh = v_refs[i][0, 0, :, h, :].astype(BF16)
            pv = jnp.dot(pb, vh, preferred_element_type=F32)
            acc = acc + jnp.where((mrow // 2) == h, pv, 0.0)
    m_ref[...] = m_new
    l_ref[...] = l_new
    acc_ref[...] = acc

    @pl.when(j == pl.num_programs(1) - 1)
    def _():
        lam = _lambda(lq1_ref, lk1_ref, lq2_ref, lk2_ref, lam_init)
        outs = []
        for h in range(DA_HEADS):
            r1, r2 = 2 * h, 2 * h + 1
            outs.append(_diff_combine(acc[r1:r1 + 1], l_new[r1:r1 + 1], acc[r2:r2 + 1],
                                      l_new[r2:r2 + 1], lam, gsub_ref[...], lam_init))
        o_ref[0] = jnp.concatenate(outs, axis=-1).astype(BF16)


def _attn_sample(qb, k_new, v_new, kt_pages, v_pages, page_table, lams, gsub, lam_init):
    nb, n_pages = page_table.shape
    page = kt_pages.shape[-1]
    steps = n_pages // PAGES_PER_STEP
    nmap = 2 * DA_HEADS
    row = lambda w: pl.BlockSpec((1, 1, w), lambda b, j, pt: (b, 0, 0))
    small = lambda a: pl.BlockSpec(a.shape, lambda b, j, pt: (0, 0))

    def paged(block, i):
        return pl.BlockSpec(block, lambda b, j, pt: (0, pt[b, j * PAGES_PER_STEP + i], 0, 0, 0))

    k_specs = [paged((1, 1, nmap, DA_DH, page), i) for i in range(PAGES_PER_STEP)]
    v_specs = [paged((1, 1, page, DA_HEADS, DA_DV), i) for i in range(PAGES_PER_STEP)]
    grid_spec = pltpu.PrefetchScalarGridSpec(
        num_scalar_prefetch=1,
        grid=(nb, steps),
        in_specs=[row(DA_WIDTH), row(DA_WIDTH), row(DA_HEADS * DA_DV)]
                 + [small(a) for a in lams] + [small(gsub)] + k_specs + v_specs,
        out_specs=row(DA_HEADS * DA_DV),
        scratch_shapes=[pltpu.VMEM((DA_WIDTH, page), F32), pltpu.VMEM((nmap, page), F32),
                        pltpu.VMEM((nmap, page), F32), pltpu.VMEM((nmap, DA_DV), F32)],
    )
    out = pl.pallas_call(
        functools.partial(_attn_sample_kernel, lam_init=lam_init, past_len=n_pages * page),
        grid_spec=grid_spec,
        out_shape=jax.ShapeDtypeStruct((nb, 1, DA_HEADS * DA_DV), BF16),
        compiler_params=pltpu.CompilerParams(dimension_semantics=("arbitrary", "arbitrary")),
        name="attn_sample",
    )(page_table, qb.reshape(nb, 1, DA_WIDTH), k_new.reshape(nb, 1, DA_WIDTH),
      v_new.reshape(nb, 1, DA_HEADS * DA_DV), *lams, gsub,
      *([kt_pages] * PAGES_PER_STEP), *([v_pages] * PAGES_PER_STEP))
    return out.reshape(nb, DA_HEADS * DA_DV)


PROMPT_ROWS = 256


def kernel(x_prompt, x_sample, cache_k, cache_v, state_hgrn, page_table, ffn1_norm, ffn1_w_gate, ffn1_w_up, ffn1_w_down, mix_norm, w_in, hg_lb_logits, hg_out_norm, da_q_norm, da_k_norm, da_lambda_q1, da_lambda_k1, da_lambda_q2, da_lambda_k2, da_subln, w_out, ffn2_norm, ffn2_w_gate, ffn2_w_up, ffn2_w_down):
    batch, seq, d = x_prompt.shape
    nb = x_sample.shape[0]
    depth = ffn1_norm.shape[0]
    nmap = 2 * DA_HEADS
    yp = x_prompt.reshape(batch * seq, d)
    ys = x_sample.reshape(nb, d)
    kt_cache = jnp.transpose(cache_k, (0, 1, 3, 4, 2))
    outs = [[] for _ in range(6)]
    for l in range(depth):
        lam_init = 0.8 - 0.6 * math.exp(-0.3 * l)
        bf = lambda w: w[l].astype(BF16)
        w1 = (bf(ffn1_w_gate), bf(ffn1_w_up), bf(ffn1_w_down))
        w2 = (bf(ffn2_w_gate), bf(ffn2_w_up), bf(ffn2_w_down))
        win, wo = bf(w_in), bf(w_out)
        gq = jnp.tile(da_q_norm[l:l + 1], (1, nmap))
        gk = jnp.tile(da_k_norm[l:l + 1], (1, nmap))
        lams = (da_lambda_q1[l:l + 1], da_lambda_k1[l:l + 1], da_lambda_q2[l:l + 1],
                da_lambda_k2[l:l + 1])
        gout, gsub = hg_out_norm[l:l + 1], da_subln[l:l + 1]

        def inproj(x, tm):
            return _ffn_inproj(x, ffn1_norm[l:l + 1], *w1, mix_norm[l:l + 1], win, gq, gk, tm)

        def outproj(h, ohg, oda, tm):
            return _outproj_ffn(h, ohg, oda, wo, ffn2_norm[l:l + 1], *w2, tm)

        h, zhg, qb, k, kb, v, vb = inproj(yp, PROMPT_ROWS)
        ohg, s_fin = _hgrn_prompt(zhg, hg_lb_logits, gout, batch, l)
        oda = _attn_prompt(qb, kb, vb, lams, gsub, batch, lam_init)
        yp = outproj(h, ohg, oda, PROMPT_ROWS)
        outs[0].append(k.reshape(batch, seq, nmap, DA_DH))
        outs[1].append(v.reshape(batch, seq, DA_HEADS, DA_DV))
        outs[2].append(s_fin)
        h, zhg, qb, k, _, v, _ = inproj(ys, nb)
        ohg, s_new = _hgrn_sample(zhg, state_hgrn[l], hg_lb_logits, gout, l)
        oda = _attn_sample(qb, k, v, kt_cache[l:l + 1], cache_v[l:l + 1], page_table, lams, gsub,
                           lam_init)
        ys = outproj(h, ohg.reshape(nb, HG_WIDTH), oda, nb)
        outs[3].append(k.reshape(nb, 1, nmap, DA_DH))
        outs[4].append(v.reshape(nb, 1, DA_HEADS, DA_DV))
        outs[5].append(s_new)
    stacked = [jnp.stack(o, axis=0) for o in outs]
    return (yp.reshape(batch, seq, d), ys.reshape(nb, 1, d), *stacked)
```

```python
import functools
import math

import jax
import jax.numpy as jnp
from jax import lax
from jax.experimental import pallas as pl
from jax.experimental.pallas import tpu as pltpu

F32 = jnp.float32
BF16 = jnp.bfloat16

NORM_EPS = 1e-6
NEG_INF = -1e30
LANES = 128
SUBLANES = 8
HG_HEADS = 4
HG_DK = 128
HG_DV = 128
HG_WIDTH = HG_HEADS * HG_DK
DA_HEADS = 4
DA_DH = 64
DA_DV = 128
DA_WIDTH = 2 * DA_HEADS * DA_DH
DA_SCALE = DA_DH ** -0.5
HG_COLS = 4 * HG_WIDTH
VMEM_LIMIT = 56 * 1024 * 1024

_NT = (((1,), (1,)), ((), ()))
_TN = (((0,), (0,)), ((), ()))


def _rmsnorm(x, g):
    ms = jnp.mean(x * x, axis=-1, keepdims=True)
    return x * lax.rsqrt(ms + NORM_EPS) * g


def _sigmoid(x):
    return 0.5 * jnp.tanh(0.5 * x) + 0.5


def _silu(x):
    return x * _sigmoid(x)


def _split3(x):
    hi = x.astype(BF16)
    r1 = x - hi.astype(F32)
    mid = r1.astype(BF16)
    lo = (r1 - mid.astype(F32)).astype(BF16)
    return hi, mid, lo


def _dot_exact_rhs(sel, x):
    hi, mid, lo = _split3(x)
    return (jnp.dot(sel, hi, preferred_element_type=F32)
            + jnp.dot(sel, mid, preferred_element_type=F32)
            + jnp.dot(sel, lo, preferred_element_type=F32))


def _dot_exact_lhs(x, sel):
    hi, mid, lo = _split3(x)
    return (jnp.dot(hi, sel, preferred_element_type=F32)
            + jnp.dot(mid, sel, preferred_element_type=F32)
            + jnp.dot(lo, sel, preferred_element_type=F32))


def _col_bcast(row, width=LANES):
    n = row.shape[1]
    r = lax.broadcasted_iota(jnp.int32, (n, n), 0)
    c = lax.broadcasted_iota(jnp.int32, (n, n), 1)
    diag = jnp.where(r == c, jnp.broadcast_to(row, (n, n)), 0.0)
    return _dot_exact_lhs(diag, jnp.ones((n, width), BF16))


def _group64_rmsnorm(x, g):
    m, w = x.shape
    r = lax.broadcasted_iota(jnp.int32, (LANES, LANES), 0) // DA_DH
    c = lax.broadcasted_iota(jnp.int32, (LANES, LANES), 1) // DA_DH
    same = jnp.where(r == c, 1.0, 0.0).astype(BF16)
    x2 = x * x
    parts = [_dot_exact_lhs(x2[:, i * LANES:(i + 1) * LANES], same) for i in range(w // LANES)]
    ss = jnp.concatenate(parts, axis=-1)
    return x * lax.rsqrt(ss * (1.0 / DA_DH) + NORM_EPS) * g


def _swiglu(xn, wg_ref, wu_ref, wd_ref):
    g = jnp.dot(xn, wg_ref[...], preferred_element_type=F32)
    u = jnp.dot(xn, wu_ref[...], preferred_element_type=F32)
    a = (_silu(g) * u).astype(BF16)
    return jnp.dot(a, wd_ref[...], preferred_element_type=F32)


def _lower_bound(logits, layer):
    m = jnp.max(logits, axis=0, keepdims=True)
    e = jnp.exp(logits - m)
    return jnp.sum(e[:layer + 1], axis=0, keepdims=True) / jnp.sum(e, axis=0, keepdims=True)


def _lambda(lq1_ref, lk1_ref, lq2_ref, lk2_ref, lam_init):
    a = jnp.sum(lq1_ref[...] * lk1_ref[...], axis=-1, keepdims=True)
    b = jnp.sum(lq2_ref[...] * lk2_ref[...], axis=-1, keepdims=True)
    return jnp.exp(a) - jnp.exp(b) + lam_init


def _ffn_inproj_kernel(x_ref, g1_ref, wg_ref, wu_ref, wd_ref, gm_ref, win_ref, gq_ref, gk_ref,
                       h_ref, zhg_ref, q_ref, k_ref, kb_ref, v_ref):
    x = x_ref[...]
    h = x + 0.5 * _swiglu(_rmsnorm(x, g1_ref[...]).astype(BF16), wg_ref, wu_ref, wd_ref)
    h_ref[...] = h
    z = jnp.dot(_rmsnorm(h, gm_ref[...]).astype(BF16), win_ref[...], preferred_element_type=F32)
    zhg_ref[...] = z[:, :HG_COLS]
    dq = z[:, HG_COLS:HG_COLS + DA_WIDTH]
    dk = z[:, HG_COLS + DA_WIDTH:HG_COLS + 2 * DA_WIDTH]
    dv = z[:, HG_COLS + 2 * DA_WIDTH:]
    q_ref[...] = (_group64_rmsnorm(dq, gq_ref[...]) * DA_SCALE).astype(BF16)
    k = _group64_rmsnorm(dk, gk_ref[...])
    k_ref[...] = k
    kb_ref[...] = k.astype(BF16)
    v_ref[...] = dv


def _resident(shape):
    return pl.BlockSpec(shape, lambda *_: (0,) * len(shape), pipeline_mode=pl.Buffered(1))


def _ffn_inproj(x, g1, wg, wu, wd, gm, win, gq, gk, tm):
    n, d = x.shape
    dff = wg.shape[1]
    rows = lambda w: pl.BlockSpec((tm, w), lambda i: (i, 0))
    outs = [(d, F32), (HG_COLS, F32), (DA_WIDTH, BF16), (DA_WIDTH, F32), (DA_WIDTH, BF16),
            (DA_WIDTH, F32)]
    return pl.pallas_call(
        _ffn_inproj_kernel,
        grid=(n // tm,),
        in_specs=[rows(d), _resident((1, d)), _resident((d, dff)), _resident((d, dff)),
                  _resident((dff, d)), _resident((1, d)), _resident(win.shape),
                  _resident((1, DA_WIDTH)), _resident((1, DA_WIDTH))],
        out_specs=[rows(w) for w, _ in outs],
        out_shape=[jax.ShapeDtypeStruct((n, w), dt) for w, dt in outs],
        compiler_params=pltpu.CompilerParams(dimension_semantics=("arbitrary",),
                                             vmem_limit_bytes=VMEM_LIMIT),
        name="ffn_inproj",
    )(x, g1, wg, wu, wd, gm, win, gq, gk)


def _outproj_ffn_kernel(h_ref, ohg_ref, oda_ref, wo_ref, g2_ref, wg_ref, wu_ref, wd_ref, y_ref):
    h = (h_ref[...]
         + jnp.dot(ohg_ref[...], wo_ref[:HG_WIDTH, :], preferred_element_type=F32)
         + jnp.dot(oda_ref[...], wo_ref[HG_WIDTH:, :], preferred_element_type=F32))
    y_ref[...] = h + 0.5 * _swiglu(_rmsnorm(h, g2_ref[...]).astype(BF16), wg_ref, wu_ref, wd_ref)


def _outproj_ffn(h, ohg, oda, wo, g2, wg, wu, wd, tm):
    n, d = h.shape
    dff = wg.shape[1]
    rows = lambda w: pl.BlockSpec((tm, w), lambda i: (i, 0))
    return pl.pallas_call(
        _outproj_ffn_kernel,
        grid=(n // tm,),
        in_specs=[rows(d), rows(HG_WIDTH), rows(DA_WIDTH), _resident(wo.shape), _resident((1, d)),
                  _resident((d, dff)), _resident((d, dff)), _resident((dff, d))],
        out_specs=rows(d),
        out_shape=jax.ShapeDtypeStruct((n, d), F32),
        compiler_params=pltpu.CompilerParams(dimension_semantics=("arbitrary",),
                                             vmem_limit_bytes=VMEM_LIMIT),
        name="outproj_ffn",
    )(h, ohg, oda, wo, g2, wg, wu, wd)


HG_CHUNK = 128
HG_BLOCK = 4
HG_GROUP = 4


def _hgrn_gates(z, lb):
    xq = z[:, :HG_WIDTH]
    xf = z[:, HG_WIDTH:2 * HG_WIDTH]
    xi = z[:, 2 * HG_WIDTH:3 * HG_WIDTH]
    xg = z[:, 3 * HG_WIDTH:]
    q = _silu(xq) * (HG_DK ** -0.5)
    f = lb + (1.0 - lb) * _sigmoid(xf)
    return q, 1.0 - f, f, xi, _silu(xg)


def _hgrn_chunk_kernel(z_ref, lbl_ref, gout_ref, o_ref, sfin_ref, st_ref, *, layer):
    c = pl.program_id(1)
    group, n = z_ref.shape[0], z_ref.shape[1]

    @pl.when(c == 0)
    def _():
        st_ref[...] = jnp.zeros_like(st_ref)

    lb = _lower_bound(lbl_ref[...], layer)
    row = lax.broadcasted_iota(jnp.int32, (n, n), 0)
    col = lax.broadcasted_iota(jnp.int32, (n, n), 1)
    tri = jnp.where(row >= col, 1.0, 0.0).astype(BF16)
    trow = lax.broadcasted_iota(jnp.int32, (n, 1), 0)
    spans = []
    w = HG_BLOCK
    while 2 * w <= n:
        span = 2 * w
        spans.append((w, span, (trow % span) >= w,
                      ((row // span) == (col // span)) & ((row % span) >= w) & ((col % span) < w)))
        w = span
    nears = [((row - col) == d) & ((row % HG_BLOCK) >= d) for d in range(HG_BLOCK)]

    for r in range(group):
        q, k, f, v, gate = _hgrn_gates(z_ref[r], lb)
        b = _dot_exact_rhs(tri, jnp.log2(f))
        a = [jnp.zeros((n, n), F32) for _ in range(HG_HEADS)]
        for w, span, right, lvl in spans:
            ref = jnp.concatenate(
                [jnp.broadcast_to(b[p * span + w - 1:p * span + w, :], (span, HG_WIDTH))
                 for p in range(n // span)], axis=0)
            e = jnp.exp2(-jnp.abs(b - ref))
            qw = jnp.where(right, q * e, 0.0).astype(BF16)
            kw = jnp.where(right, 0.0, k * e).astype(BF16)
            for h in range(HG_HEADS):
                hs = slice(h * HG_DK, (h + 1) * HG_DK)
                p = lax.dot_general(qw[:, hs], kw[:, hs], _NT, preferred_element_type=F32)
                a[h] = jnp.where(lvl, p, a[h])
        for d in range(HG_BLOCK):
            kd = pltpu.roll(k, d, axis=0) if d else k
            bd = pltpu.roll(b, d, axis=0) if d else b
            p = q * kd * jnp.exp2(jnp.minimum(b - bd, 0.0))
            for h in range(HG_HEADS):
                hs = slice(h * HG_DK, (h + 1) * HG_DK)
                a[h] = jnp.where(nears[d], jnp.sum(p[:, hs], axis=-1, keepdims=True), a[h])

        b_last = b[n - 1:n, :]
        q_in = (q * jnp.exp2(b)).astype(BF16)
        k_out = (k * jnp.exp2(b_last - b)).astype(BF16)
        carry = jnp.exp2(b_last)
        vb = v.astype(BF16)
        for h in range(HG_HEADS):
            hs = slice(h * HG_DK, (h + 1) * HG_DK)
            st = st_ref[r, h]
            o = (jnp.dot(a[h].astype(BF16), vb[:, hs], preferred_element_type=F32)
                 + lax.dot_general(q_in[:, hs], st.astype(BF16), _NT, preferred_element_type=F32))
            st_new = st * carry[:, hs] + lax.dot_general(vb[:, hs], k_out[:, hs], _TN,
                                                         preferred_element_type=F32)
            st_ref[r, h] = st_new
            o_ref[r, :, hs] = (_rmsnorm(o, gout_ref[...]) * gate[:, hs]).astype(BF16)

    @pl.when(c == pl.num_programs(1) - 1)
    def _():
        for r in range(group):
            for h in range(HG_HEADS):
                sfin_ref[r, h] = st_ref[r, h].T


def _hgrn_prompt(zhg, lb_logits, gout, batch, layer):
    n = zhg.shape[0]
    seq = n // batch
    nc = seq // HG_CHUNK
    group = HG_GROUP if batch % HG_GROUP == 0 else 1
    o, s_fin = pl.pallas_call(
        functools.partial(_hgrn_chunk_kernel, layer=layer),
        grid=(batch // group, nc),
        in_specs=[pl.BlockSpec((group, HG_CHUNK, HG_COLS), lambda g, c: (g, c, 0)),
                  pl.BlockSpec(lb_logits.shape, lambda g, c: (0, 0)),
                  pl.BlockSpec((1, HG_DV), lambda g, c: (0, 0))],
        out_specs=[pl.BlockSpec((group, HG_CHUNK, HG_WIDTH), lambda g, c: (g, c, 0)),
                   pl.BlockSpec((group, HG_HEADS, HG_DK, HG_DV), lambda g, c: (g, 0, 0, 0))],
        out_shape=[jax.ShapeDtypeStruct((batch, seq, HG_WIDTH), BF16),
                   jax.ShapeDtypeStruct((batch, HG_HEADS, HG_DK, HG_DV), F32)],
        scratch_shapes=[pltpu.VMEM((group, HG_HEADS, HG_DV, HG_DK), F32)],
        compiler_params=pltpu.CompilerParams(dimension_semantics=("arbitrary", "arbitrary")),
        name="hgrn_prompt",
    )(zhg.reshape(batch, seq, HG_COLS), lb_logits, gout)
    return o.reshape(n, HG_WIDTH), s_fin


def _hgrn_step_kernel(z_ref, s_ref, lbl_ref, gout_ref, o_ref, snew_ref, *, layer):
    lb = _lower_bound(lbl_ref[...], layer)
    q, k, f, v, gate = _hgrn_gates(z_ref[0], lb)
    outs = []
    for h in range(HG_HEADS):
        hs = slice(h * HG_DK, (h + 1) * HG_DK)
        s_new = _col_bcast(f[:, hs]) * s_ref[0, h] + _col_bcast(k[:, hs]) * v[:, hs]
        snew_ref[0, h] = s_new
        o = jnp.sum(_col_bcast(q[:, hs]) * s_new, axis=0, keepdims=True)
        outs.append(_rmsnorm(o, gout_ref[...]) * gate[:, hs])
    o_ref[0] = jnp.concatenate(outs, axis=-1).astype(BF16)


def _hgrn_sample(zhg, state, lb_logits, gout, layer):
    nb = zhg.shape[0]
    return pl.pallas_call(
        functools.partial(_hgrn_step_kernel, layer=layer),
        grid=(nb,),
        in_specs=[pl.BlockSpec((1, 1, HG_COLS), lambda b: (b, 0, 0)),
                  pl.BlockSpec((1, HG_HEADS, HG_DK, HG_DV), lambda b: (b, 0, 0, 0)),
                  pl.BlockSpec(lb_logits.shape, lambda b: (0, 0)),
                  pl.BlockSpec((1, HG_DV), lambda b: (0, 0))],
        out_specs=[pl.BlockSpec((1, 1, HG_WIDTH), lambda b: (b, 0, 0)),
                   pl.BlockSpec((1, HG_HEADS, HG_DK, HG_DV), lambda b: (b, 0, 0, 0))],
        out_shape=[jax.ShapeDtypeStruct((nb, 1, HG_WIDTH), BF16),
                   jax.ShapeDtypeStruct(state.shape, F32)],
        compiler_params=pltpu.CompilerParams(dimension_semantics=("arbitrary",)),
        name="hgrn_sample",
    )(zhg.reshape(nb, 1, HG_COLS), state, lb_logits, gout)


ATT_TILE = 512


def _head_slope(h):
    return jnp.exp2(jnp.full((1, 1), -8.0 / DA_HEADS, F32) * (h + 1).astype(F32))


def _diff_combine(acc1, l1, acc2, l2, lam, gsub, lam_init):
    out = acc1 / l1 - lam * (acc2 / l2)
    return _rmsnorm(out, gsub) * (1.0 - lam_init)


ATT_VROWS = DA_DV + 16


def _attn_prompt_kernel(q_ref, k_ref, v_ref, lq1_ref, lk1_ref, lq2_ref, lk2_ref, gsub_ref, o_ref,
                        kaug_ref, vt_ref, m_ref, acc_ref, *, lam_init):
    h = pl.program_id(1)
    qi = pl.program_id(2)
    t = q_ref.shape[0]
    nt = kaug_ref.shape[1]
    slope = _head_slope(h)
    lane = lax.broadcasted_iota(jnp.int32, (t, 2 * DA_DH), 1)
    loc = lax.broadcasted_iota(jnp.int32, (t, 2 * DA_DH), 0)
    loc_lo = (loc % 256).astype(F32)
    loc_hi = (loc - loc % 256).astype(F32)
    slot = [(1 - i) * DA_DH for i in range(2)]

    def augment(x, i, extras):
        out = jnp.zeros_like(x)
        for n, e in enumerate(extras):
            out = jnp.where(lane == slot[i] + n, e, out)
        return jnp.where((lane // DA_DH) == i, x, out)

    @pl.when(qi == 0)
    def _():
        for j in range(nt):
            k = k_ref[j * t:(j + 1) * t, :].astype(F32)
            for i in range(2):
                kaug_ref[i, j] = augment(k, i, (slope * loc_lo, slope * loc_hi, 1.0, 1.0)
                                         ).astype(BF16)
            vt_ref[j, :DA_DV, :] = v_ref[j * t:(j + 1) * t, :].T.astype(BF16)
            vt_ref[j, DA_DV:, :] = jnp.ones((ATT_VROWS - DA_DV, t), BF16)

    q = q_ref[...].astype(F32)
    qts = [augment(q, i, (1.0, 1.0, -slope * loc_lo, -slope * loc_hi)).T.astype(BF16)
           for i in range(2)]
    m_ref[...] = jnp.full_like(m_ref, NEG_INF)
    acc_ref[...] = jnp.zeros_like(acc_ref)
    causal = (lax.broadcasted_iota(jnp.int32, (t, t), 0)
              <= lax.broadcasted_iota(jnp.int32, (t, t), 1))

    def tile(ki, masked):
        shift = -slope * ((qi - ki) * t).astype(F32)
        ss = [jnp.dot(kaug_ref[i, ki], qts[i], preferred_element_type=F32)
              for i in range(2)]
        if masked:
            ss = [jnp.where(causal, s, NEG_INF) for s in ss]
        ps, alphas = [], []
        for i in range(2):
            m_old = m_ref[i]
            m_new = jnp.maximum(m_old, jnp.max(ss[i], axis=0, keepdims=True) + shift)
            alphas.append(jnp.exp(m_old - m_new))
            ps.append(jnp.exp(ss[i] - (m_new - shift)).astype(BF16))
            m_ref[i] = m_new
        vt = vt_ref[ki]
        for i in range(2):
            acc_ref[i] = alphas[i] * acc_ref[i] + jnp.dot(vt, ps[i], preferred_element_type=F32)

    def body(ki, carry):
        tile(ki, False)
        return carry

    lax.fori_loop(0, qi, body, 0)
    tile(qi, True)
    lam = _lambda(lq1_ref, lk1_ref, lq2_ref, lk2_ref, lam_init)
    a1, a2 = acc_ref[0], acc_ref[1]
    out = a1[:DA_DV] / a1[DA_DV:DA_DV + 1] - lam * (a2[:DA_DV] / a2[DA_DV:DA_DV + 1])
    ms = jnp.mean(out * out, axis=0, keepdims=True)
    out = out * lax.rsqrt(ms + NORM_EPS) * _col_bcast(gsub_ref[...], t) * (1.0 - lam_init)
    o_ref[...] = out.T.astype(BF16)


def _attn_prompt(qb, kb, v, lams, gsub, batch, lam_init):
    n = qb.shape[0]
    seq = n // batch
    nq = seq // ATT_TILE
    small = lambda a: pl.BlockSpec(a.shape, lambda b, h, i: (0, 0))
    return pl.pallas_call(
        functools.partial(_attn_prompt_kernel, lam_init=lam_init),
        grid=(batch, DA_HEADS, nq),
        in_specs=[pl.BlockSpec((ATT_TILE, 2 * DA_DH), lambda b, h, i: (b * nq + i, h)),
                  pl.BlockSpec((seq, 2 * DA_DH), lambda b, h, i: (b, h)),
                  pl.BlockSpec((seq, DA_DV), lambda b, h, i: (b, h))]
                 + [small(a) for a in lams] + [small(gsub)],
        out_specs=pl.BlockSpec((ATT_TILE, DA_DV), lambda b, h, i: (b * nq + i, h)),
        out_shape=jax.ShapeDtypeStruct((n, DA_HEADS * DA_DV), BF16),
        scratch_shapes=[pltpu.VMEM((2, nq, ATT_TILE, 2 * DA_DH), BF16),
                        pltpu.VMEM((nq, ATT_VROWS, ATT_TILE), BF16),
                        pltpu.VMEM((2, 1, ATT_TILE), F32),
                        pltpu.VMEM((2, ATT_VROWS, ATT_TILE), F32)],
        compiler_params=pltpu.CompilerParams(
            dimension_semantics=("arbitrary", "arbitrary", "arbitrary")),
        name="attn_prompt",
    )(qb, kb, v, *lams, gsub)


PAGES_PER_STEP = 8


def _attn_sample_kernel(pt_ref, q_ref, kn_ref, vn_ref, lq1_ref, lk1_ref, lq2_ref, lk2_ref,
                        gsub_ref, *refs, lam_init, past_len):
    del pt_ref
    k_refs = refs[:PAGES_PER_STEP]
    v_refs = refs[PAGES_PER_STEP:2 * PAGES_PER_STEP]
    o_ref, qcol_ref, expand_ref, m_ref, l_ref, acc_ref = refs[2 * PAGES_PER_STEP:]
    j = pl.program_id(1)
    nmap = 2 * DA_HEADS
    page = k_refs[0].shape[-1]
    vrows = v_refs[0].shape[-2]
    mrow = lax.broadcasted_iota(jnp.int32, (nmap, 1), 0)
    slope = jnp.exp2((-8.0 / DA_HEADS) * ((mrow // 2) + 1).astype(F32))
    own = (lax.broadcasted_iota(jnp.int32, (1, vrows), 1) % DA_HEADS) == (mrow // 2)

    @pl.when(j == 0)
    def _():
        tok = lax.broadcasted_iota(jnp.int32, (page, vrows), 0)
        vrow = lax.broadcasted_iota(jnp.int32, (page, vrows), 1)
        expand_ref[...] = jnp.where(vrow // DA_HEADS == tok, 1.0, 0.0).astype(BF16)
        q = q_ref[0].astype(F32)
        qcol = _col_bcast(q, page)
        qcol_ref[...] = qcol
        kcol = _col_bcast(kn_ref[0], page)
        s_new = jnp.sum((qcol * kcol).reshape(nmap, DA_DH, page), axis=1)
        m_ref[...] = s_new
        l_ref[...] = jnp.ones_like(l_ref)
        vn = vn_ref[0]
        acc_ref[...] = jnp.concatenate(
            [vn[:, (r // 2) * DA_DV:(r // 2 + 1) * DA_DV] for r in range(nmap)], axis=0)

    qcol = qcol_ref[...].reshape(nmap, DA_DH, page)
    lane = lax.broadcasted_iota(jnp.int32, (1, page), 1)
    scores = []
    for i in range(PAGES_PER_STEP):
        s = jnp.sum(k_refs[i][0, 0] * qcol, axis=1)
        pos = (j * PAGES_PER_STEP + i) * page + lane
        scores.append(s - slope * (past_len - pos).astype(F32))
    m_old = m_ref[...]
    m_new = m_old
    for s in scores:
        m_new = jnp.maximum(m_new, jnp.max(s, axis=-1, keepdims=True))
    alpha = jnp.exp(m_old - m_new)
    l_new = alpha * l_ref[...]
    acc = alpha * acc_ref[...]
    probs = []
    for s in scores:
        p = jnp.exp(s - m_new)
        l_new = l_new + jnp.sum(p, axis=-1, keepdims=True)
        probs.append(p.astype(BF16))
    spread = jnp.dot(jnp.concatenate(probs, axis=0), expand_ref[...],
                     preferred_element_type=F32)
    for i in range(PAGES_PER_STEP):
        w = jnp.where(own, spread[i * nmap:(i + 1) * nmap], 0.0).astype(BF16)
        acc = acc + jnp.dot(w, v_refs[i][0, 0].astype(BF16), preferred_element_type=F32)
    m_ref[...] = m_new
    l_ref[...] = l_new
    acc_ref[...] = acc

    @pl.when(j == pl.num_programs(1) - 1)
    def _():
        lam = _lambda(lq1_ref, lk1_ref, lq2_ref, lk2_ref, lam_init)
        outs = []
        for h in range(DA_HEADS):
            r1, r2 = 2 * h, 2 * h + 1
            outs.append(_diff_combine(acc[r1:r1 + 1], l_new[r1:r1 + 1], acc[r2:r2 + 1],
                                      l_new[r2:r2 + 1], lam, gsub_ref[...], lam_init))
        o_ref[0] = jnp.concatenate(outs, axis=-1).astype(BF16)


def _attn_sample(qb, k_new, v_new, kt_pages, v_pages, page_table, lams, gsub, lam_init):
    nb, n_pages = page_table.shape
    page = kt_pages.shape[-1]
    vrows = v_pages.shape[-2]
    steps = n_pages // PAGES_PER_STEP
    nmap = 2 * DA_HEADS
    row = lambda w: pl.BlockSpec((1, 1, w), lambda b, j, pt: (b, 0, 0))
    small = lambda a: pl.BlockSpec(a.shape, lambda b, j, pt: (0, 0))

    def paged(block, i):
        zeros = (0,) * (len(block) - 2)
        return pl.BlockSpec(block, lambda b, j, pt: (0, pt[b, j * PAGES_PER_STEP + i]) + zeros)

    k_specs = [paged((1, 1, nmap, DA_DH, page), i) for i in range(PAGES_PER_STEP)]
    v_specs = [paged((1, 1, vrows, DA_DV), i) for i in range(PAGES_PER_STEP)]
    grid_spec = pltpu.PrefetchScalarGridSpec(
        num_scalar_prefetch=1,
        grid=(nb, steps),
        in_specs=[row(DA_WIDTH), row(DA_WIDTH), row(DA_HEADS * DA_DV)]
                 + [small(a) for a in lams] + [small(gsub)] + k_specs + v_specs,
        out_specs=row(DA_HEADS * DA_DV),
        scratch_shapes=[pltpu.VMEM((DA_WIDTH, page), F32), pltpu.VMEM((page, vrows), BF16),
                        pltpu.VMEM((nmap, page), F32), pltpu.VMEM((nmap, page), F32),
                        pltpu.VMEM((nmap, DA_DV), F32)],
    )
    out = pl.pallas_call(
        functools.partial(_attn_sample_kernel, lam_init=lam_init, past_len=n_pages * page),
        grid_spec=grid_spec,
        out_shape=jax.ShapeDtypeStruct((nb, 1, DA_HEADS * DA_DV), BF16),
        compiler_params=pltpu.CompilerParams(dimension_semantics=("arbitrary", "arbitrary")),
        name="attn_sample",
    )(page_table, qb.reshape(nb, 1, DA_WIDTH), k_new.reshape(nb, 1, DA_WIDTH),
      v_new.reshape(nb, 1, DA_HEADS * DA_DV), *lams, gsub,
      *([kt_pages] * PAGES_PER_STEP), *([v_pages] * PAGES_PER_STEP))
    return out.reshape(nb, DA_HEADS * DA_DV)


PROMPT_ROWS = 256


def kernel(x_prompt, x_sample, cache_k, cache_v, state_hgrn, page_table, ffn1_norm, ffn1_w_gate, ffn1_w_up, ffn1_w_down, mix_norm, w_in, hg_lb_logits, hg_out_norm, da_q_norm, da_k_norm, da_lambda_q1, da_lambda_k1, da_lambda_q2, da_lambda_k2, da_subln, w_out, ffn2_norm, ffn2_w_gate, ffn2_w_up, ffn2_w_down):
    batch, seq, d = x_prompt.shape
    nb = x_sample.shape[0]
    depth = ffn1_norm.shape[0]
    nmap = 2 * DA_HEADS
    yp = x_prompt.reshape(batch * seq, d)
    ys = x_sample.reshape(nb, d)
    kt_cache = jnp.transpose(cache_k, (0, 1, 3, 4, 2))
    v_cache = cache_v.reshape(cache_v.shape[:2] + (-1, DA_DV))
    outs = [[] for _ in range(6)]
    for l in range(depth):
        lam_init = 0.8 - 0.6 * math.exp(-0.3 * l)
        bf = lambda w: w[l].astype(BF16)
        w1 = (bf(ffn1_w_gate), bf(ffn1_w_up), bf(ffn1_w_down))
        w2 = (bf(ffn2_w_gate), bf(ffn2_w_up), bf(ffn2_w_down))
        win, wo = bf(w_in), bf(w_out)
        gq = jnp.tile(da_q_norm[l:l + 1], (1, nmap))
        gk = jnp.tile(da_k_norm[l:l + 1], (1, nmap))
        lams = (da_lambda_q1[l:l + 1], da_lambda_k1[l:l + 1], da_lambda_q2[l:l + 1],
                da_lambda_k2[l:l + 1])
        gout, gsub = hg_out_norm[l:l + 1], da_subln[l:l + 1]

        def inproj(x, tm):
            return _ffn_inproj(x, ffn1_norm[l:l + 1], *w1, mix_norm[l:l + 1], win, gq, gk, tm)

        def outproj(h, ohg, oda, tm):
            return _outproj_ffn(h, ohg, oda, wo, ffn2_norm[l:l + 1], *w2, tm)

        h, zhg, qb, k, kb, v = inproj(yp, PROMPT_ROWS)
        ohg, s_fin = _hgrn_prompt(zhg, hg_lb_logits, gout, batch, l)
        oda = _attn_prompt(qb, kb, v, lams, gsub, batch, lam_init)
        yp = outproj(h, ohg, oda, PROMPT_ROWS)
        outs[0].append(k.reshape(batch, seq, nmap, DA_DH))
        outs[1].append(v.reshape(batch, seq, DA_HEADS, DA_DV))
        outs[2].append(s_fin)
        h, zhg, qb, k, _, v = inproj(ys, nb)
        ohg, s_new = _hgrn_sample(zhg, state_hgrn[l], hg_lb_logits, gout, l)
        oda = _attn_sample(qb, k, v, kt_cache[l:l + 1], v_cache[l:l + 1], page_table, lams, gsub,
                           lam_init)
        ys = outproj(h, ohg.reshape(nb, HG_WIDTH), oda, nb)
        outs[3].append(k.reshape(nb, 1, nmap, DA_DH))
        outs[4].append(v.reshape(nb, 1, DA_HEADS, DA_DV))
        outs[5].append(s_new)
    stacked = [jnp.stack(o, axis=0) for o in outs]
    return (yp.reshape(batch, seq, d), ys.reshape(nb, 1, d), *stacked)
```

```python
import functools
import math

import jax
import jax.numpy as jnp
from jax import lax
from jax.experimental import pallas as pl
from jax.experimental.pallas import tpu as pltpu

F32 = jnp.float32
BF16 = jnp.bfloat16

NORM_EPS = 1e-6
NEG_INF = -1e30
LANES = 128
SUBLANES = 8
HG_HEADS = 4
HG_DK = 128
HG_DV = 128
HG_WIDTH = HG_HEADS * HG_DK
DA_HEADS = 4
DA_DH = 64
DA_DV = 128
DA_WIDTH = 2 * DA_HEADS * DA_DH
DA_SCALE = DA_DH ** -0.5
HG_COLS = 4 * HG_WIDTH
VMEM_LIMIT = 56 * 1024 * 1024

_NT = (((1,), (1,)), ((), ()))
_TN = (((0,), (0,)), ((), ()))


def _rmsnorm(x, g):
    ms = jnp.mean(x * x, axis=-1, keepdims=True)
    return x * lax.rsqrt(ms + NORM_EPS) * g


def _sigmoid(x):
    return 0.5 * jnp.tanh(0.5 * x) + 0.5


def _silu(x):
    return x * _sigmoid(x)


def _split3(x):
    hi = x.astype(BF16)
    r1 = x - hi.astype(F32)
    mid = r1.astype(BF16)
    lo = (r1 - mid.astype(F32)).astype(BF16)
    return hi, mid, lo


def _dot_exact_rhs(sel, x):
    hi, mid, lo = _split3(x)
    return (jnp.dot(sel, hi, preferred_element_type=F32)
            + jnp.dot(sel, mid, preferred_element_type=F32)
            + jnp.dot(sel, lo, preferred_element_type=F32))


def _dot_exact_lhs(x, sel):
    hi, mid, lo = _split3(x)
    return (jnp.dot(hi, sel, preferred_element_type=F32)
            + jnp.dot(mid, sel, preferred_element_type=F32)
            + jnp.dot(lo, sel, preferred_element_type=F32))


def _col_bcast(row, width=LANES):
    n = row.shape[1]
    r = lax.broadcasted_iota(jnp.int32, (n, n), 0)
    c = lax.broadcasted_iota(jnp.int32, (n, n), 1)
    diag = jnp.where(r == c, jnp.broadcast_to(row, (n, n)), 0.0)
    return _dot_exact_lhs(diag, jnp.ones((n, width), BF16))


def _group64_rmsnorm(x, g):
    m, w = x.shape
    r = lax.broadcasted_iota(jnp.int32, (LANES, LANES), 0) // DA_DH
    c = lax.broadcasted_iota(jnp.int32, (LANES, LANES), 1) // DA_DH
    same = jnp.where(r == c, 1.0, 0.0).astype(BF16)
    x2 = x * x
    parts = [_dot_exact_lhs(x2[:, i * LANES:(i + 1) * LANES], same) for i in range(w // LANES)]
    ss = jnp.concatenate(parts, axis=-1)
    return x * lax.rsqrt(ss * (1.0 / DA_DH) + NORM_EPS) * g


def _swiglu(xn, wg_ref, wu_ref, wd_ref):
    g = jnp.dot(xn, wg_ref[...], preferred_element_type=F32)
    u = jnp.dot(xn, wu_ref[...], preferred_element_type=F32)
    a = (_silu(g) * u).astype(BF16)
    return jnp.dot(a, wd_ref[...], preferred_element_type=F32)


def _lower_bound(logits, layer):
    m = jnp.max(logits, axis=0, keepdims=True)
    e = jnp.exp(logits - m)
    return jnp.sum(e[:layer + 1], axis=0, keepdims=True) / jnp.sum(e, axis=0, keepdims=True)


def _lambda(lq1_ref, lk1_ref, lq2_ref, lk2_ref, lam_init):
    a = jnp.sum(lq1_ref[...] * lk1_ref[...], axis=-1, keepdims=True)
    b = jnp.sum(lq2_ref[...] * lk2_ref[...], axis=-1, keepdims=True)
    return jnp.exp(a) - jnp.exp(b) + lam_init


def _ffn_inproj_kernel(x_ref, g1_ref, wg_ref, wu_ref, wd_ref, gm_ref, win_ref, gq_ref, gk_ref,
                       h_ref, zhg_ref, q_ref, kt_ref, kb_ref, v_ref):
    tm = x_ref.shape[0]
    x = x_ref[...]
    h = x + 0.5 * _swiglu(_rmsnorm(x, g1_ref[...]).astype(BF16), wg_ref, wu_ref, wd_ref)
    h_ref[...] = h
    z = jnp.dot(_rmsnorm(h, gm_ref[...]).astype(BF16), win_ref[...], preferred_element_type=F32)
    zhg_ref[...] = z[:, :HG_COLS]
    dq = z[:, HG_COLS:HG_COLS + DA_WIDTH]
    dk = z[:, HG_COLS + DA_WIDTH:HG_COLS + 2 * DA_WIDTH]
    dv = z[:, HG_COLS + 2 * DA_WIDTH:]
    q_ref[...] = (_group64_rmsnorm(dq, gq_ref[...]) * DA_SCALE).astype(BF16)
    k = _group64_rmsnorm(dk, gk_ref[...])
    kt_ref[0] = k.T
    kb_ref[...] = k.astype(BF16)
    for hd in range(DA_HEADS):
        v_ref[pl.ds(hd, tm, stride=DA_HEADS), :] = dv[:, hd * DA_DV:(hd + 1) * DA_DV]


def _resident(shape):
    return pl.BlockSpec(shape, lambda *_: (0,) * len(shape), pipeline_mode=pl.Buffered(1))


def _ffn_inproj(x, g1, wg, wu, wd, gm, win, gq, gk, tm, seq):
    n, d = x.shape
    dff = wg.shape[1]
    tiles = seq // tm
    rows = lambda w: pl.BlockSpec((tm, w), lambda i: (i, 0))
    out_specs = [rows(d), rows(HG_COLS), rows(DA_WIDTH),
                 pl.BlockSpec((1, DA_WIDTH, tm), lambda i: (i // tiles, 0, i % tiles)),
                 rows(DA_WIDTH),
                 pl.BlockSpec((tm * DA_HEADS, DA_DV), lambda i: (i, 0))]
    out_shape = [jax.ShapeDtypeStruct((n, d), F32), jax.ShapeDtypeStruct((n, HG_COLS), F32),
                 jax.ShapeDtypeStruct((n, DA_WIDTH), BF16),
                 jax.ShapeDtypeStruct((n // seq, DA_WIDTH, seq), F32),
                 jax.ShapeDtypeStruct((n, DA_WIDTH), BF16),
                 jax.ShapeDtypeStruct((n * DA_HEADS, DA_DV), F32)]
    return pl.pallas_call(
        _ffn_inproj_kernel,
        grid=(n // tm,),
        in_specs=[rows(d), _resident((1, d)), _resident((d, dff)), _resident((d, dff)),
                  _resident((dff, d)), _resident((1, d)), _resident(win.shape),
                  _resident((1, DA_WIDTH)), _resident((1, DA_WIDTH))],
        out_specs=out_specs,
        out_shape=out_shape,
        compiler_params=pltpu.CompilerParams(dimension_semantics=("arbitrary",),
                                             vmem_limit_bytes=VMEM_LIMIT),
        name="ffn_inproj",
    )(x, g1, wg, wu, wd, gm, win, gq, gk)


def _outproj_ffn_kernel(h_ref, ohg_ref, oda_ref, wo_ref, g2_ref, wg_ref, wu_ref, wd_ref, y_ref):
    h = (h_ref[...]
         + jnp.dot(ohg_ref[...], wo_ref[:HG_WIDTH, :], preferred_element_type=F32)
         + jnp.dot(oda_ref[...], wo_ref[HG_WIDTH:, :], preferred_element_type=F32))
    y_ref[...] = h + 0.5 * _swiglu(_rmsnorm(h, g2_ref[...]).astype(BF16), wg_ref, wu_ref, wd_ref)


def _outproj_ffn(h, ohg, oda, wo, g2, wg, wu, wd, tm):
    n, d = h.shape
    dff = wg.shape[1]
    rows = lambda w: pl.BlockSpec((tm, w), lambda i: (i, 0))
    return pl.pallas_call(
        _outproj_ffn_kernel,
        grid=(n // tm,),
        in_specs=[rows(d), rows(HG_WIDTH), rows(DA_WIDTH), _resident(wo.shape), _resident((1, d)),
                  _resident((d, dff)), _resident((d, dff)), _resident((dff, d))],
        out_specs=rows(d),
        out_shape=jax.ShapeDtypeStruct((n, d), F32),
        compiler_params=pltpu.CompilerParams(dimension_semantics=("arbitrary",),
                                             vmem_limit_bytes=VMEM_LIMIT),
        name="outproj_ffn",
    )(h, ohg, oda, wo, g2, wg, wu, wd)


HG_CHUNK = 128
HG_BLOCK = 4
HG_GROUP = 4


def _hgrn_gates(z, lb):
    xq = z[:, :HG_WIDTH]
    xf = z[:, HG_WIDTH:2 * HG_WIDTH]
    xi = z[:, 2 * HG_WIDTH:3 * HG_WIDTH]
    xg = z[:, 3 * HG_WIDTH:]
    q = _silu(xq) * (HG_DK ** -0.5)
    f = lb + (1.0 - lb) * _sigmoid(xf)
    return q, 1.0 - f, f, xi, _silu(xg)


def _hgrn_chunk_kernel(z_ref, lbl_ref, gout_ref, o_ref, sfin_ref, st_ref, *, layer):
    c = pl.program_id(1)
    group, n = z_ref.shape[0], z_ref.shape[1]

    @pl.when(c == 0)
    def _():
        st_ref[...] = jnp.zeros_like(st_ref)

    lb = _lower_bound(lbl_ref[...], layer)
    row = lax.broadcasted_iota(jnp.int32, (n, n), 0)
    col = lax.broadcasted_iota(jnp.int32, (n, n), 1)
    tri = jnp.where(row >= col, 1.0, 0.0).astype(BF16)
    trow = lax.broadcasted_iota(jnp.int32, (n, 1), 0)
    spans = []
    w = HG_BLOCK
    while 2 * w <= n:
        span = 2 * w
        spans.append((w, span, (trow % span) >= w,
                      ((row // span) == (col // span)) & ((row % span) >= w) & ((col % span) < w)))
        w = span
    nears = [((row - col) == d) & ((row % HG_BLOCK) >= d) for d in range(HG_BLOCK)]

    for r in range(group):
        q, k, f, v, gate = _hgrn_gates(z_ref[r], lb)
        b = _dot_exact_rhs(tri, jnp.log2(f))
        a = [jnp.zeros((n, n), F32) for _ in range(HG_HEADS)]
        for w, span, right, lvl in spans:
            ref = jnp.concatenate(
                [jnp.broadcast_to(b[p * span + w - 1:p * span + w, :], (span, HG_WIDTH))
                 for p in range(n // span)], axis=0)
            e = jnp.exp2(-jnp.abs(b - ref))
            qw = jnp.where(right, q * e, 0.0).astype(BF16)
            kw = jnp.where(right, 0.0, k * e).astype(BF16)
            for h in range(HG_HEADS):
                hs = slice(h * HG_DK, (h + 1) * HG_DK)
                p = lax.dot_general(qw[:, hs], kw[:, hs], _NT, preferred_element_type=F32)
                a[h] = jnp.where(lvl, p, a[h])
        for d in range(HG_BLOCK):
            kd = pltpu.roll(k, d, axis=0) if d else k
            bd = pltpu.roll(b, d, axis=0) if d else b
            p = q * kd * jnp.exp2(jnp.minimum(b - bd, 0.0))
            for h in range(HG_HEADS):
                hs = slice(h * HG_DK, (h + 1) * HG_DK)
                a[h] = jnp.where(nears[d], jnp.sum(p[:, hs], axis=-1, keepdims=True), a[h])

        b_last = b[n - 1:n, :]
        q_in = (q * jnp.exp2(b)).astype(BF16)
        k_out = (k * jnp.exp2(b_last - b)).astype(BF16)
        carry = jnp.exp2(b_last)
        vb = v.astype(BF16)
        for h in range(HG_HEADS):
            hs = slice(h * HG_DK, (h + 1) * HG_DK)
            st = st_ref[r, h]
            o = (jnp.dot(a[h].astype(BF16), vb[:, hs], preferred_element_type=F32)
                 + lax.dot_general(q_in[:, hs], st.astype(BF16), _NT, preferred_element_type=F32))
            st_new = st * carry[:, hs] + lax.dot_general(vb[:, hs], k_out[:, hs], _TN,
                                                         preferred_element_type=F32)
            st_ref[r, h] = st_new
            o_ref[r, :, hs] = (_rmsnorm(o, gout_ref[...]) * gate[:, hs]).astype(BF16)

    @pl.when(c == pl.num_programs(1) - 1)
    def _():
        for r in range(group):
            for h in range(HG_HEADS):
                sfin_ref[r, h] = st_ref[r, h].T


def _hgrn_prompt(zhg, lb_logits, gout, batch, layer):
    n = zhg.shape[0]
    seq = n // batch
    nc = seq // HG_CHUNK
    group = HG_GROUP if batch % HG_GROUP == 0 else 1
    o, s_fin = pl.pallas_call(
        functools.partial(_hgrn_chunk_kernel, layer=layer),
        grid=(batch // group, nc),
        in_specs=[pl.BlockSpec((group, HG_CHUNK, HG_COLS), lambda g, c: (g, c, 0)),
                  pl.BlockSpec(lb_logits.shape, lambda g, c: (0, 0)),
                  pl.BlockSpec((1, HG_DV), lambda g, c: (0, 0))],
        out_specs=[pl.BlockSpec((group, HG_CHUNK, HG_WIDTH), lambda g, c: (g, c, 0)),
                   pl.BlockSpec((group, HG_HEADS, HG_DK, HG_DV), lambda g, c: (g, 0, 0, 0))],
        out_shape=[jax.ShapeDtypeStruct((batch, seq, HG_WIDTH), BF16),
                   jax.ShapeDtypeStruct((batch, HG_HEADS, HG_DK, HG_DV), F32)],
        scratch_shapes=[pltpu.VMEM((group, HG_HEADS, HG_DV, HG_DK), F32)],
        compiler_params=pltpu.CompilerParams(dimension_semantics=("arbitrary", "arbitrary")),
        name="hgrn_prompt",
    )(zhg.reshape(batch, seq, HG_COLS), lb_logits, gout)
    return o.reshape(n, HG_WIDTH), s_fin


def _hgrn_step_kernel(z_ref, s_ref, lbl_ref, gout_ref, o_ref, snew_ref, *, layer):
    lb = _lower_bound(lbl_ref[...], layer)
    q, k, f, v, gate = _hgrn_gates(z_ref[0], lb)
    outs = []
    for h in range(HG_HEADS):
        hs = slice(h * HG_DK, (h + 1) * HG_DK)
        s_new = _col_bcast(f[:, hs]) * s_ref[0, h] + _col_bcast(k[:, hs]) * v[:, hs]
        snew_ref[0, h] = s_new
        o = jnp.sum(_col_bcast(q[:, hs]) * s_new, axis=0, keepdims=True)
        outs.append(_rmsnorm(o, gout_ref[...]) * gate[:, hs])
    o_ref[0] = jnp.concatenate(outs, axis=-1).astype(BF16)


def _hgrn_sample(zhg, state, lb_logits, gout, layer):
    nb = zhg.shape[0]
    return pl.pallas_call(
        functools.partial(_hgrn_step_kernel, layer=layer),
        grid=(nb,),
        in_specs=[pl.BlockSpec((1, 1, HG_COLS), lambda b: (b, 0, 0)),
                  pl.BlockSpec((1, HG_HEADS, HG_DK, HG_DV), lambda b: (b, 0, 0, 0)),
                  pl.BlockSpec(lb_logits.shape, lambda b: (0, 0)),
                  pl.BlockSpec((1, HG_DV), lambda b: (0, 0))],
        out_specs=[pl.BlockSpec((1, 1, HG_WIDTH), lambda b: (b, 0, 0)),
                   pl.BlockSpec((1, HG_HEADS, HG_DK, HG_DV), lambda b: (b, 0, 0, 0))],
        out_shape=[jax.ShapeDtypeStruct((nb, 1, HG_WIDTH), BF16),
                   jax.ShapeDtypeStruct(state.shape, F32)],
        compiler_params=pltpu.CompilerParams(dimension_semantics=("arbitrary",)),
        name="hgrn_sample",
    )(zhg.reshape(nb, 1, HG_COLS), state, lb_logits, gout)


ATT_TILE = 512


def _head_slope(h):
    return jnp.exp2(jnp.full((1, 1), -8.0 / DA_HEADS, F32) * (h + 1).astype(F32))


def _diff_combine(acc1, l1, acc2, l2, lam, gsub, lam_init):
    out = acc1 / l1 - lam * (acc2 / l2)
    return _rmsnorm(out, gsub) * (1.0 - lam_init)


ATT_VROWS = DA_DV + 16


def _attn_prompt_kernel(q_ref, k_ref, v_ref, lq1_ref, lk1_ref, lq2_ref, lk2_ref, gsub_ref, o_ref,
                        kaug_ref, vt_ref, m_ref, acc_ref, p_ref, alpha_ref, *, lam_init):
    h = pl.program_id(1)
    qi = pl.program_id(2)
    t = q_ref.shape[0]
    nt = kaug_ref.shape[1]
    slope = _head_slope(h)
    lane = lax.broadcasted_iota(jnp.int32, (t, 2 * DA_DH), 1)
    loc = lax.broadcasted_iota(jnp.int32, (t, 2 * DA_DH), 0)
    loc_lo = (loc % 256).astype(F32)
    loc_hi = (loc - loc % 256).astype(F32)
    slot = [(1 - i) * DA_DH for i in range(2)]

    def augment(x, i, extras):
        out = jnp.zeros_like(x)
        for n, e in enumerate(extras):
            out = jnp.where(lane == slot[i] + n, e, out)
        return jnp.where((lane // DA_DH) == i, x, out)

    @pl.when(qi == 0)
    def _():
        for j in range(nt):
            k = k_ref[j * t:(j + 1) * t, :].astype(F32)
            for i in range(2):
                kaug_ref[i, j] = augment(k, i, (slope * loc_lo, slope * loc_hi, 1.0, 1.0)
                                         ).astype(BF16)
            vj = v_ref[pl.ds(j * t * DA_HEADS + h, t, stride=DA_HEADS), :]
            vt_ref[j, :DA_DV, :] = vj.T.astype(BF16)
            vt_ref[j, DA_DV:, :] = jnp.ones((ATT_VROWS - DA_DV, t), BF16)

    q = q_ref[...].astype(F32)
    qts = [augment(q, i, (1.0, 1.0, -slope * loc_lo, -slope * loc_hi)).T.astype(BF16)
           for i in range(2)]
    m_ref[...] = jnp.full_like(m_ref, NEG_INF)
    acc_ref[...] = jnp.zeros_like(acc_ref)
    causal = (lax.broadcasted_iota(jnp.int32, (t, t), 0)
              <= lax.broadcasted_iota(jnp.int32, (t, t), 1))

    def values(ki, slot):
        vt = vt_ref[ki]
        for i in range(2):
            acc_ref[i] = (alpha_ref[slot, i] * acc_ref[i]
                          + jnp.dot(vt, p_ref[slot, i], preferred_element_type=F32))

    def step(ki, slot, keep=None, first=False):
        ss = [jnp.dot(kaug_ref[i, ki], qts[i], preferred_element_type=F32)
              for i in range(2)]
        if not first:
            values(ki - 1, 1 - slot)
        shift = -slope * ((qi - ki) * t).astype(F32)
        for i in range(2):
            s = ss[i] if keep is None else jnp.where(keep, ss[i], NEG_INF)
            m_old = m_ref[i]
            m_new = jnp.maximum(m_old, jnp.max(s, axis=0, keepdims=True) + shift)
            alpha_ref[slot, i] = jnp.exp(m_old - m_new)
            p_ref[slot, i] = jnp.exp(s - (m_new - shift)).astype(BF16)
            m_ref[i] = m_new

    step(0, 0, keep=jnp.logical_or(causal, qi > 0), first=True)
    inner = qi - 1

    def pair(j, carry):
        step(1 + 2 * j, 1)
        step(2 + 2 * j, 0)
        return carry

    lax.fori_loop(0, inner // 2, pair, 0)
    for parity in range(2):
        @pl.when(jnp.logical_and(qi > 0, qi % 2 == parity))
        def _():
            if parity == 0:
                step(qi - 1, 1)
            step(qi, parity, keep=causal)

        @pl.when(qi % 2 == parity)
        def _():
            values(qi, parity)

    lam = _lambda(lq1_ref, lk1_ref, lq2_ref, lk2_ref, lam_init)
    a1, a2 = acc_ref[0], acc_ref[1]
    out = a1[:DA_DV] / a1[DA_DV:DA_DV + 1] - lam * (a2[:DA_DV] / a2[DA_DV:DA_DV + 1])
    ms = jnp.mean(out * out, axis=0, keepdims=True)
    out = out * lax.rsqrt(ms + NORM_EPS) * _col_bcast(gsub_ref[...], t) * (1.0 - lam_init)
    o_ref[...] = out.T.astype(BF16)


def _attn_prompt(qb, kb, v, lams, gsub, batch, lam_init):
    n = qb.shape[0]
    seq = n // batch
    nq = seq // ATT_TILE
    small = lambda a: pl.BlockSpec(a.shape, lambda b, h, i: (0, 0))
    return pl.pallas_call(
        functools.partial(_attn_prompt_kernel, lam_init=lam_init),
        grid=(batch, DA_HEADS, nq),
        in_specs=[pl.BlockSpec((ATT_TILE, 2 * DA_DH), lambda b, h, i: (b * nq + i, h)),
                  pl.BlockSpec((seq, 2 * DA_DH), lambda b, h, i: (b, h)),
                  pl.BlockSpec((seq * DA_HEADS, DA_DV), lambda b, h, i: (b, 0))]
                 + [small(a) for a in lams] + [small(gsub)],
        out_specs=pl.BlockSpec((ATT_TILE, DA_DV), lambda b, h, i: (b * nq + i, h)),
        out_shape=jax.ShapeDtypeStruct((n, DA_HEADS * DA_DV), BF16),
        scratch_shapes=[pltpu.VMEM((2, nq, ATT_TILE, 2 * DA_DH), BF16),
                        pltpu.VMEM((nq, ATT_VROWS, ATT_TILE), BF16),
                        pltpu.VMEM((2, 1, ATT_TILE), F32),
                        pltpu.VMEM((2, ATT_VROWS, ATT_TILE), F32),
                        pltpu.VMEM((2, 2, ATT_TILE, ATT_TILE), BF16),
                        pltpu.VMEM((2, 2, 1, ATT_TILE), F32)],
        compiler_params=pltpu.CompilerParams(
            dimension_semantics=("arbitrary", "arbitrary", "arbitrary")),
        name="attn_prompt",
    )(qb, kb, v, *lams, gsub)


PAGE_GROUP = 8


def _attn_sample_kernel(pt_ref, q_ref, kn_ref, vn_ref, lq1_ref, lk1_ref, lq2_ref, lk2_ref,
                        gsub_ref, kt_hbm, v_hbm, o_ref, kbuf, vbuf, sem, qcol_ref, expand_ref,
                        m_ref, l_ref, acc_ref, *, lam_init):
    b = pl.program_id(0)
    nb = pl.num_programs(0)
    nmap = 2 * DA_HEADS
    page = kbuf.shape[-1]
    vrows = vbuf.shape[-2]
    n_groups = pt_ref.shape[1] // PAGE_GROUP
    past_len = pt_ref.shape[1] * page
    mrow = lax.broadcasted_iota(jnp.int32, (nmap, 1), 0)
    slope = jnp.exp2((-8.0 / DA_HEADS) * ((mrow // 2) + 1).astype(F32))
    own = (lax.broadcasted_iota(jnp.int32, (1, vrows), 1) % DA_HEADS) == (mrow // 2)

    def group_copies(seq, g, slot):
        copies = []
        for i in range(PAGE_GROUP):
            pg = pt_ref[seq, g * PAGE_GROUP + i]
            copies.append(pltpu.make_async_copy(kt_hbm.at[0, pg], kbuf.at[slot, i], sem.at[slot, 0]))
            copies.append(pltpu.make_async_copy(v_hbm.at[0, pg], vbuf.at[slot, i], sem.at[slot, 1]))
        return copies

    @pl.when(b == 0)
    def _():
        for c in group_copies(0, 0, 0):
            c.start()
        tok = lax.broadcasted_iota(jnp.int32, (page, vrows), 0)
        vrow = lax.broadcasted_iota(jnp.int32, (page, vrows), 1)
        expand_ref[...] = jnp.where(vrow // DA_HEADS == tok, 1.0, 0.0).astype(BF16)

    qcol = _col_bcast(q_ref[0].astype(F32), page)
    qcol_ref[...] = qcol
    kcol = _col_bcast(kn_ref[0], page)
    m_ref[...] = jnp.sum((qcol * kcol).reshape(nmap, DA_DH, page), axis=1)
    l_ref[...] = jnp.ones_like(l_ref)
    vn = vn_ref[0]
    acc_ref[...] = jnp.concatenate(
        [vn[:, (r // 2) * DA_DV:(r // 2 + 1) * DA_DV] for r in range(nmap)], axis=0)
    lane = lax.broadcasted_iota(jnp.int32, (1, page), 1)

    def group(g, carry):
        slot = g % 2

        @pl.when(g + 1 < n_groups)
        def _():
            for c in group_copies(b, g + 1, 1 - slot):
                c.start()

        @pl.when(jnp.logical_and(g + 1 == n_groups, b + 1 < nb))
        def _():
            for c in group_copies(b + 1, 0, 1 - slot):
                c.start()

        for c in group_copies(b, g, slot):
            c.wait()
        qc = qcol_ref[...].reshape(nmap, DA_DH, page)
        scores = []
        for i in range(PAGE_GROUP):
            s = jnp.sum(kbuf[slot, i] * qc, axis=1)
            pos = (g * PAGE_GROUP + i) * page + lane
            scores.append(s - slope * (past_len - pos).astype(F32))
        m_old = m_ref[...]
        m_new = m_old
        for s in scores:
            m_new = jnp.maximum(m_new, jnp.max(s, axis=-1, keepdims=True))
        alpha = jnp.exp(m_old - m_new)
        l_new = alpha * l_ref[...]
        acc = alpha * acc_ref[...]
        probs = []
        for s in scores:
            p = jnp.exp(s - m_new)
            l_new = l_new + jnp.sum(p, axis=-1, keepdims=True)
            probs.append(p.astype(BF16))
        spread = jnp.dot(jnp.concatenate(probs, axis=0), expand_ref[...],
                         preferred_element_type=F32)
        for i in range(PAGE_GROUP):
            w = jnp.where(own, spread[i * nmap:(i + 1) * nmap], 0.0).astype(BF16)
            acc = acc + jnp.dot(w, vbuf[slot, i].astype(BF16), preferred_element_type=F32)
        m_ref[...] = m_new
        l_ref[...] = l_new
        acc_ref[...] = acc
        return carry

    lax.fori_loop(0, n_groups, group, 0)
    lam = _lambda(lq1_ref, lk1_ref, lq2_ref, lk2_ref, lam_init)
    acc, l_fin = acc_ref[...], l_ref[...]
    outs = []
    for h in range(DA_HEADS):
        r1, r2 = 2 * h, 2 * h + 1
        outs.append(_diff_combine(acc[r1:r1 + 1], l_fin[r1:r1 + 1], acc[r2:r2 + 1],
                                  l_fin[r2:r2 + 1], lam, gsub_ref[...], lam_init))
    o_ref[0] = jnp.concatenate(outs, axis=-1).astype(BF16)


def _attn_sample(qb, k_new, v_new, kt_pages, v_pages, page_table, lams, gsub, lam_init):
    nb, n_pages = page_table.shape
    page = kt_pages.shape[-1]
    vrows = v_pages.shape[-2]
    nmap = 2 * DA_HEADS
    assert (n_pages // PAGE_GROUP) % 2 == 0 and n_pages % PAGE_GROUP == 0
    row = lambda w: pl.BlockSpec((1, 1, w), lambda b, pt: (b, 0, 0))
    small = lambda a: pl.BlockSpec(a.shape, lambda b, pt: (0, 0))
    grid_spec = pltpu.PrefetchScalarGridSpec(
        num_scalar_prefetch=1,
        grid=(nb,),
        in_specs=[row(DA_WIDTH), row(DA_WIDTH), row(DA_HEADS * DA_DV)]
                 + [small(a) for a in lams] + [small(gsub)]
                 + [pl.BlockSpec(memory_space=pl.ANY), pl.BlockSpec(memory_space=pl.ANY)],
        out_specs=row(DA_HEADS * DA_DV),
        scratch_shapes=[pltpu.VMEM((2, PAGE_GROUP, nmap, DA_DH, page), F32),
                        pltpu.VMEM((2, PAGE_GROUP, vrows, DA_DV), F32),
                        pltpu.SemaphoreType.DMA((2, 2)),
                        pltpu.VMEM((DA_WIDTH, page), F32), pltpu.VMEM((page, vrows), BF16),
                        pltpu.VMEM((nmap, page), F32), pltpu.VMEM((nmap, page), F32),
                        pltpu.VMEM((nmap, DA_DV), F32)],
    )
    out = pl.pallas_call(
        functools.partial(_attn_sample_kernel, lam_init=lam_init),
        grid_spec=grid_spec,
        out_shape=jax.ShapeDtypeStruct((nb, 1, DA_HEADS * DA_DV), BF16),
        compiler_params=pltpu.CompilerParams(dimension_semantics=("arbitrary",)),
        name="attn_sample",
    )(page_table, qb.reshape(nb, 1, DA_WIDTH), k_new.reshape(nb, 1, DA_WIDTH),
      v_new.reshape(nb, 1, DA_HEADS * DA_DV), *lams, gsub, kt_pages, v_pages)
    return out.reshape(nb, DA_HEADS * DA_DV)


PROMPT_ROWS = 256


def kernel(x_prompt, x_sample, cache_k, cache_v, state_hgrn, page_table, ffn1_norm, ffn1_w_gate, ffn1_w_up, ffn1_w_down, mix_norm, w_in, hg_lb_logits, hg_out_norm, da_q_norm, da_k_norm, da_lambda_q1, da_lambda_k1, da_lambda_q2, da_lambda_k2, da_subln, w_out, ffn2_norm, ffn2_w_gate, ffn2_w_up, ffn2_w_down):
    batch, seq, d = x_prompt.shape
    nb = x_sample.shape[0]
    depth = ffn1_norm.shape[0]
    nmap = 2 * DA_HEADS
    yp = x_prompt.reshape(batch * seq, d)
    ys = x_sample.reshape(nb, d)
    kt_cache = jnp.transpose(cache_k, (0, 1, 3, 4, 2))
    v_cache = cache_v.reshape(cache_v.shape[:2] + (-1, DA_DV))
    outs = [[] for _ in range(6)]
    for l in range(depth):
        lam_init = 0.8 - 0.6 * math.exp(-0.3 * l)
        bf = lambda w: w[l].astype(BF16)
        w1 = (bf(ffn1_w_gate), bf(ffn1_w_up), bf(ffn1_w_down))
        w2 = (bf(ffn2_w_gate), bf(ffn2_w_up), bf(ffn2_w_down))
        win, wo = bf(w_in), bf(w_out)
        gq = jnp.tile(da_q_norm[l:l + 1], (1, nmap))
        gk = jnp.tile(da_k_norm[l:l + 1], (1, nmap))
        lams = (da_lambda_q1[l:l + 1], da_lambda_k1[l:l + 1], da_lambda_q2[l:l + 1],
                da_lambda_k2[l:l + 1])
        gout, gsub = hg_out_norm[l:l + 1], da_subln[l:l + 1]

        def inproj(x, tm, rows_per_seq):
            return _ffn_inproj(x, ffn1_norm[l:l + 1], *w1, mix_norm[l:l + 1], win, gq, gk, tm,
                               rows_per_seq)

        def outproj(h, ohg, oda, tm):
            return _outproj_ffn(h, ohg, oda, wo, ffn2_norm[l:l + 1], *w2, tm)

        def keys(kt):
            return jnp.transpose(kt.reshape(kt.shape[0], nmap, DA_DH, kt.shape[2]), (0, 3, 1, 2))

        h, zhg, qb, kt, kb, v = inproj(yp, PROMPT_ROWS, seq)
        ohg, s_fin = _hgrn_prompt(zhg, hg_lb_logits, gout, batch, l)
        oda = _attn_prompt(qb, kb, v, lams, gsub, batch, lam_init)
        yp = outproj(h, ohg, oda, PROMPT_ROWS)
        outs[0].append(keys(kt))
        outs[1].append(v.reshape(batch, seq, DA_HEADS, DA_DV))
        outs[2].append(s_fin)
        h, zhg, qb, kt, _, v = inproj(ys, nb, nb)
        k = keys(kt).reshape(nb, DA_WIDTH)
        ohg, s_new = _hgrn_sample(zhg, state_hgrn[l], hg_lb_logits, gout, l)
        oda = _attn_sample(qb, k, v, kt_cache[l:l + 1], v_cache[l:l + 1], page_table, lams, gsub,
                           lam_init)
        ys = outproj(h, ohg.reshape(nb, HG_WIDTH), oda, nb)
        outs[3].append(k.reshape(nb, 1, nmap, DA_DH))
        outs[4].append(v.reshape(nb, 1, DA_HEADS, DA_DV))
        outs[5].append(s_new)
    stacked = [jnp.stack(o, axis=0) for o in outs]
    return (yp.reshape(batch, seq, d), ys.reshape(nb, 1, d), *stacked)
```

```python
import functools
import math

import jax
import jax.numpy as jnp
from jax import lax
from jax.experimental import pallas as pl
from jax.experimental.pallas import tpu as pltpu

F32 = jnp.float32
BF16 = jnp.bfloat16

NORM_EPS = 1e-6
NEG_INF = -1e30
LANES = 128
SUBLANES = 8
HG_HEADS = 4
HG_DK = 128
HG_DV = 128
HG_WIDTH = HG_HEADS * HG_DK
DA_HEADS = 4
DA_DH = 64
DA_DV = 128
DA_WIDTH = 2 * DA_HEADS * DA_DH
DA_SCALE = DA_DH ** -0.5
HG_COLS = 4 * HG_WIDTH
VMEM_LIMIT = 56 * 1024 * 1024

_NT = (((1,), (1,)), ((), ()))
_TN = (((0,), (0,)), ((), ()))


def _rmsnorm(x, g):
    ms = jnp.mean(x * x, axis=-1, keepdims=True)
    return x * lax.rsqrt(ms + NORM_EPS) * g


def _sigmoid(x):
    return 0.5 * jnp.tanh(0.5 * x) + 0.5


def _silu(x):
    return x * _sigmoid(x)


def _split3(x):
    hi = x.astype(BF16)
    r1 = x - hi.astype(F32)
    mid = r1.astype(BF16)
    lo = (r1 - mid.astype(F32)).astype(BF16)
    return hi, mid, lo


def _dot_exact_rhs(sel, x):
    hi, mid, lo = _split3(x)
    return (jnp.dot(sel, hi, preferred_element_type=F32)
            + jnp.dot(sel, mid, preferred_element_type=F32)
            + jnp.dot(sel, lo, preferred_element_type=F32))


def _dot_exact_lhs(x, sel):
    hi, mid, lo = _split3(x)
    return (jnp.dot(hi, sel, preferred_element_type=F32)
            + jnp.dot(mid, sel, preferred_element_type=F32)
            + jnp.dot(lo, sel, preferred_element_type=F32))


def _col_bcast(row, width=LANES):
    n = row.shape[1]
    r = lax.broadcasted_iota(jnp.int32, (n, n), 0)
    c = lax.broadcasted_iota(jnp.int32, (n, n), 1)
    diag = jnp.where(r == c, jnp.broadcast_to(row, (n, n)), 0.0)
    return _dot_exact_lhs(diag, jnp.ones((n, width), BF16))


def _group64_rmsnorm(x, g):
    m, w = x.shape
    r = lax.broadcasted_iota(jnp.int32, (LANES, LANES), 0) // DA_DH
    c = lax.broadcasted_iota(jnp.int32, (LANES, LANES), 1) // DA_DH
    same = jnp.where(r == c, 1.0, 0.0).astype(BF16)
    x2 = x * x
    parts = [_dot_exact_lhs(x2[:, i * LANES:(i + 1) * LANES], same) for i in range(w // LANES)]
    ss = jnp.concatenate(parts, axis=-1)
    return x * lax.rsqrt(ss * (1.0 / DA_DH) + NORM_EPS) * g


def _swiglu(xn, wg_ref, wu_ref, wd_ref):
    g = jnp.dot(xn, wg_ref[...], preferred_element_type=F32)
    u = jnp.dot(xn, wu_ref[...], preferred_element_type=F32)
    a = (_silu(g) * u).astype(BF16)
    return jnp.dot(a, wd_ref[...], preferred_element_type=F32)


def _lower_bound(logits, layer):
    m = jnp.max(logits, axis=0, keepdims=True)
    e = jnp.exp(logits - m)
    return jnp.sum(e[:layer + 1], axis=0, keepdims=True) / jnp.sum(e, axis=0, keepdims=True)


def _lambda(lq1_ref, lk1_ref, lq2_ref, lk2_ref, lam_init):
    a = jnp.sum(lq1_ref[...] * lk1_ref[...], axis=-1, keepdims=True)
    b = jnp.sum(lq2_ref[...] * lk2_ref[...], axis=-1, keepdims=True)
    return jnp.exp(a) - jnp.exp(b) + lam_init


def _ffn_inproj_kernel(x_ref, g1_ref, wg_ref, wu_ref, wd_ref, gm_ref, win_ref, gq_ref, gk_ref,
                       h_ref, zhg_ref, q_ref, kt_ref, kb_ref, v_ref):
    tm = x_ref.shape[0]
    x = x_ref[...]
    h = x + 0.5 * _swiglu(_rmsnorm(x, g1_ref[...]).astype(BF16), wg_ref, wu_ref, wd_ref)
    h_ref[...] = h
    z = jnp.dot(_rmsnorm(h, gm_ref[...]).astype(BF16), win_ref[...], preferred_element_type=F32)
    zhg_ref[...] = z[:, :HG_COLS]
    dq = z[:, HG_COLS:HG_COLS + DA_WIDTH]
    dk = z[:, HG_COLS + DA_WIDTH:HG_COLS + 2 * DA_WIDTH]
    dv = z[:, HG_COLS + 2 * DA_WIDTH:]
    q_ref[...] = (_group64_rmsnorm(dq, gq_ref[...]) * DA_SCALE).astype(BF16)
    k = _group64_rmsnorm(dk, gk_ref[...])
    kt_ref[0] = k.T
    kb_ref[...] = k.astype(BF16)
    for hd in range(DA_HEADS):
        v_ref[pl.ds(hd, tm, stride=DA_HEADS), :] = dv[:, hd * DA_DV:(hd + 1) * DA_DV]


def _resident(shape):
    return pl.BlockSpec(shape, lambda *_: (0,) * len(shape), pipeline_mode=pl.Buffered(1))


def _ffn_inproj(x, g1, wg, wu, wd, gm, win, gq, gk, tm, seq):
    n, d = x.shape
    dff = wg.shape[1]
    tiles = seq // tm
    rows = lambda w: pl.BlockSpec((tm, w), lambda i: (i, 0))
    out_specs = [rows(d), rows(HG_COLS), rows(DA_WIDTH),
                 pl.BlockSpec((1, DA_WIDTH, tm), lambda i: (i // tiles, 0, i % tiles)),
                 rows(DA_WIDTH),
                 pl.BlockSpec((tm * DA_HEADS, DA_DV), lambda i: (i, 0))]
    out_shape = [jax.ShapeDtypeStruct((n, d), F32), jax.ShapeDtypeStruct((n, HG_COLS), F32),
                 jax.ShapeDtypeStruct((n, DA_WIDTH), BF16),
                 jax.ShapeDtypeStruct((n // seq, DA_WIDTH, seq), F32),
                 jax.ShapeDtypeStruct((n, DA_WIDTH), BF16),
                 jax.ShapeDtypeStruct((n * DA_HEADS, DA_DV), F32)]
    return pl.pallas_call(
        _ffn_inproj_kernel,
        grid=(n // tm,),
        in_specs=[rows(d), _resident((1, d)), _resident((d, dff)), _resident((d, dff)),
                  _resident((dff, d)), _resident((1, d)), _resident(win.shape),
                  _resident((1, DA_WIDTH)), _resident((1, DA_WIDTH))],
        out_specs=out_specs,
        out_shape=out_shape,
        compiler_params=pltpu.CompilerParams(dimension_semantics=("arbitrary",),
                                             vmem_limit_bytes=VMEM_LIMIT),
        name="ffn_inproj",
    )(x, g1, wg, wu, wd, gm, win, gq, gk)


def _outproj_ffn_kernel(h_ref, ohg_ref, oda_ref, wo_ref, g2_ref, wg_ref, wu_ref, wd_ref, y_ref):
    h = (h_ref[...]
         + jnp.dot(ohg_ref[...], wo_ref[:HG_WIDTH, :], preferred_element_type=F32)
         + jnp.dot(oda_ref[...], wo_ref[HG_WIDTH:, :], preferred_element_type=F32))
    y_ref[...] = h + 0.5 * _swiglu(_rmsnorm(h, g2_ref[...]).astype(BF16), wg_ref, wu_ref, wd_ref)


def _outproj_ffn(h, ohg, oda, wo, g2, wg, wu, wd, tm):
    n, d = h.shape
    dff = wg.shape[1]
    rows = lambda w: pl.BlockSpec((tm, w), lambda i: (i, 0))
    return pl.pallas_call(
        _outproj_ffn_kernel,
        grid=(n // tm,),
        in_specs=[rows(d), rows(HG_WIDTH), rows(DA_WIDTH), _resident(wo.shape), _resident((1, d)),
                  _resident((d, dff)), _resident((d, dff)), _resident((dff, d))],
        out_specs=rows(d),
        out_shape=jax.ShapeDtypeStruct((n, d), F32),
        compiler_params=pltpu.CompilerParams(dimension_semantics=("arbitrary",),
                                             vmem_limit_bytes=VMEM_LIMIT),
        name="outproj_ffn",
    )(h, ohg, oda, wo, g2, wg, wu, wd)


HG_CHUNK = 128
HG_BLOCK = 4
HG_GROUP = 4


def _hgrn_gates(z, lb):
    xq = z[:, :HG_WIDTH]
    xf = z[:, HG_WIDTH:2 * HG_WIDTH]
    xi = z[:, 2 * HG_WIDTH:3 * HG_WIDTH]
    xg = z[:, 3 * HG_WIDTH:]
    q = _silu(xq) * (HG_DK ** -0.5)
    f = lb + (1.0 - lb) * _sigmoid(xf)
    return q, 1.0 - f, f, xi, _silu(xg)


def _hgrn_chunk_kernel(z_ref, lbl_ref, gout_ref, o_ref, sfin_ref, st_ref, *, layer):
    c = pl.program_id(1)
    group, n = z_ref.shape[0], z_ref.shape[1]

    @pl.when(c == 0)
    def _():
        st_ref[...] = jnp.zeros_like(st_ref)

    lb = _lower_bound(lbl_ref[...], layer)
    row = lax.broadcasted_iota(jnp.int32, (n, n), 0)
    col = lax.broadcasted_iota(jnp.int32, (n, n), 1)
    tri = jnp.where(row >= col, 1.0, 0.0).astype(BF16)
    trow = lax.broadcasted_iota(jnp.int32, (n, 1), 0)
    spans = []
    w = HG_BLOCK
    while 2 * w <= n:
        span = 2 * w
        spans.append((w, span, (trow % span) >= w,
                      ((row // span) == (col // span)) & ((row % span) >= w) & ((col % span) < w)))
        w = span
    nears = [((row - col) == d) & ((row % HG_BLOCK) >= d) for d in range(HG_BLOCK)]

    for r in range(group):
        q, k, f, v, gate = _hgrn_gates(z_ref[r], lb)
        b = _dot_exact_rhs(tri, jnp.log2(f))
        a = [jnp.zeros((n, n), F32) for _ in range(HG_HEADS)]
        for w, span, right, lvl in spans:
            ref = jnp.concatenate(
                [jnp.broadcast_to(b[p * span + w - 1:p * span + w, :], (span, HG_WIDTH))
                 for p in range(n // span)], axis=0)
            e = jnp.exp2(-jnp.abs(b - ref))
            qw = jnp.where(right, q * e, 0.0).astype(BF16)
            kw = jnp.where(right, 0.0, k * e).astype(BF16)
            for h in range(HG_HEADS):
                hs = slice(h * HG_DK, (h + 1) * HG_DK)
                p = lax.dot_general(qw[:, hs], kw[:, hs], _NT, preferred_element_type=F32)
                a[h] = jnp.where(lvl, p, a[h])
        for d in range(HG_BLOCK):
            kd = pltpu.roll(k, d, axis=0) if d else k
            bd = pltpu.roll(b, d, axis=0) if d else b
            p = q * kd * jnp.exp2(jnp.minimum(b - bd, 0.0))
            for h in range(HG_HEADS):
                hs = slice(h * HG_DK, (h + 1) * HG_DK)
                a[h] = jnp.where(nears[d], jnp.sum(p[:, hs], axis=-1, keepdims=True), a[h])

        b_last = b[n - 1:n, :]
        q_in = (q * jnp.exp2(b)).astype(BF16)
        k_out = (k * jnp.exp2(b_last - b)).astype(BF16)
        carry = jnp.exp2(b_last)
        vb = v.astype(BF16)
        for h in range(HG_HEADS):
            hs = slice(h * HG_DK, (h + 1) * HG_DK)
            st = st_ref[r, h]
            o = (jnp.dot(a[h].astype(BF16), vb[:, hs], preferred_element_type=F32)
                 + lax.dot_general(q_in[:, hs], st.astype(BF16), _NT, preferred_element_type=F32))
            st_new = st * carry[:, hs] + lax.dot_general(vb[:, hs], k_out[:, hs], _TN,
                                                         preferred_element_type=F32)
            st_ref[r, h] = st_new
            o_ref[r, :, hs] = (_rmsnorm(o, gout_ref[...]) * gate[:, hs]).astype(BF16)

    @pl.when(c == pl.num_programs(1) - 1)
    def _():
        for r in range(group):
            for h in range(HG_HEADS):
                sfin_ref[r, h] = st_ref[r, h].T


def _hgrn_prompt(zhg, lb_logits, gout, batch, layer):
    n = zhg.shape[0]
    seq = n // batch
    nc = seq // HG_CHUNK
    group = HG_GROUP if batch % HG_GROUP == 0 else 1
    o, s_fin = pl.pallas_call(
        functools.partial(_hgrn_chunk_kernel, layer=layer),
        grid=(batch // group, nc),
        in_specs=[pl.BlockSpec((group, HG_CHUNK, HG_COLS), lambda g, c: (g, c, 0)),
                  pl.BlockSpec(lb_logits.shape, lambda g, c: (0, 0)),
                  pl.BlockSpec((1, HG_DV), lambda g, c: (0, 0))],
        out_specs=[pl.BlockSpec((group, HG_CHUNK, HG_WIDTH), lambda g, c: (g, c, 0)),
                   pl.BlockSpec((group, HG_HEADS, HG_DK, HG_DV), lambda g, c: (g, 0, 0, 0))],
        out_shape=[jax.ShapeDtypeStruct((batch, seq, HG_WIDTH), BF16),
                   jax.ShapeDtypeStruct((batch, HG_HEADS, HG_DK, HG_DV), F32)],
        scratch_shapes=[pltpu.VMEM((group, HG_HEADS, HG_DV, HG_DK), F32)],
        compiler_params=pltpu.CompilerParams(dimension_semantics=("arbitrary", "arbitrary")),
        name="hgrn_prompt",
    )(zhg.reshape(batch, seq, HG_COLS), lb_logits, gout)
    return o.reshape(n, HG_WIDTH), s_fin


def _hgrn_step_kernel(z_ref, s_ref, lbl_ref, gout_ref, o_ref, snew_ref, *, layer):
    lb = _lower_bound(lbl_ref[...], layer)
    q, k, f, v, gate = _hgrn_gates(z_ref[0], lb)
    outs = []
    for h in range(HG_HEADS):
        hs = slice(h * HG_DK, (h + 1) * HG_DK)
        s_new = _col_bcast(f[:, hs]) * s_ref[0, h] + _col_bcast(k[:, hs]) * v[:, hs]
        snew_ref[0, h] = s_new
        o = jnp.sum(_col_bcast(q[:, hs]) * s_new, axis=0, keepdims=True)
        outs.append(_rmsnorm(o, gout_ref[...]) * gate[:, hs])
    o_ref[0] = jnp.concatenate(outs, axis=-1).astype(BF16)


def _hgrn_sample(zhg, state, lb_logits, gout, layer):
    nb = zhg.shape[0]
    return pl.pallas_call(
        functools.partial(_hgrn_step_kernel, layer=layer),
        grid=(nb,),
        in_specs=[pl.BlockSpec((1, 1, HG_COLS), lambda b: (b, 0, 0)),
                  pl.BlockSpec((1, HG_HEADS, HG_DK, HG_DV), lambda b: (b, 0, 0, 0)),
                  pl.BlockSpec(lb_logits.shape, lambda b: (0, 0)),
                  pl.BlockSpec((1, HG_DV), lambda b: (0, 0))],
        out_specs=[pl.BlockSpec((1, 1, HG_WIDTH), lambda b: (b, 0, 0)),
                   pl.BlockSpec((1, HG_HEADS, HG_DK, HG_DV), lambda b: (b, 0, 0, 0))],
        out_shape=[jax.ShapeDtypeStruct((nb, 1, HG_WIDTH), BF16),
                   jax.ShapeDtypeStruct(state.shape, F32)],
        compiler_params=pltpu.CompilerParams(dimension_semantics=("arbitrary",)),
        name="hgrn_sample",
    )(zhg.reshape(nb, 1, HG_COLS), state, lb_logits, gout)


ATT_TILE = 512


def _head_slope(h):
    return jnp.exp2(jnp.full((1, 1), -8.0 / DA_HEADS, F32) * (h + 1).astype(F32))


def _diff_combine(acc1, l1, acc2, l2, lam, gsub, lam_init):
    out = acc1 / l1 - lam * (acc2 / l2)
    return _rmsnorm(out, gsub) * (1.0 - lam_init)


ATT_VROWS = DA_DV + 16


def _attn_prompt_kernel(q_ref, k_ref, v_ref, lq1_ref, lk1_ref, lq2_ref, lk2_ref, gsub_ref, o_ref,
                        kaug_ref, vt_ref, m_ref, acc_ref, p_ref, alpha_ref, *, lam_init):
    h = pl.program_id(1)
    qi = pl.program_id(2)
    t = q_ref.shape[0]
    nt = kaug_ref.shape[1]
    slope = _head_slope(h)
    lane = lax.broadcasted_iota(jnp.int32, (t, 2 * DA_DH), 1)
    loc = lax.broadcasted_iota(jnp.int32, (t, 2 * DA_DH), 0)
    loc_lo = (loc % 256).astype(F32)
    loc_hi = (loc - loc % 256).astype(F32)
    slot = [(1 - i) * DA_DH for i in range(2)]

    def augment(x, i, extras):
        out = jnp.zeros_like(x)
        for n, e in enumerate(extras):
            out = jnp.where(lane == slot[i] + n, e, out)
        return jnp.where((lane // DA_DH) == i, x, out)

    @pl.when(qi == 0)
    def _():
        for j in range(nt):
            k = k_ref[j * t:(j + 1) * t, :].astype(F32)
            for i in range(2):
                kaug_ref[i, j] = augment(k, i, (slope * loc_lo, slope * loc_hi, 1.0, 1.0)
                                         ).astype(BF16)
            vj = v_ref[pl.ds(j * t * DA_HEADS + h, t, stride=DA_HEADS), :]
            vt_ref[j, :DA_DV, :] = vj.T.astype(BF16)
            vt_ref[j, DA_DV:, :] = jnp.ones((ATT_VROWS - DA_DV, t), BF16)

    q = q_ref[...].astype(F32)
    qts = [augment(q, i, (1.0, 1.0, -slope * loc_lo, -slope * loc_hi)).T.astype(BF16)
           for i in range(2)]
    m_ref[...] = jnp.full_like(m_ref, NEG_INF)
    acc_ref[...] = jnp.zeros_like(acc_ref)
    causal = (lax.broadcasted_iota(jnp.int32, (t, t), 0)
              <= lax.broadcasted_iota(jnp.int32, (t, t), 1))

    def values(ki, slot):
        vt = vt_ref[ki]
        for i in range(2):
            acc_ref[i] = (alpha_ref[slot, i] * acc_ref[i]
                          + jnp.dot(vt, p_ref[slot, i], preferred_element_type=F32))

    def step(ki, slot, keep=None, first=False):
        ss = [jnp.dot(kaug_ref[i, ki], qts[i], preferred_element_type=F32)
              for i in range(2)]
        if not first:
            values(ki - 1, 1 - slot)
        shift = -slope * ((qi - ki) * t).astype(F32)
        for i in range(2):
            s = ss[i] if keep is None else jnp.where(keep, ss[i], NEG_INF)
            m_old = m_ref[i]
            m_new = jnp.maximum(m_old, jnp.max(s, axis=0, keepdims=True) + shift)
            alpha_ref[slot, i] = jnp.exp(m_old - m_new)
            p_ref[slot, i] = jnp.exp(s - (m_new - shift)).astype(BF16)
            m_ref[i] = m_new

    step(0, 0, keep=jnp.logical_or(causal, qi > 0), first=True)
    inner = qi - 1

    def pair(j, carry):
        step(1 + 2 * j, 1)
        step(2 + 2 * j, 0)
        return carry

    lax.fori_loop(0, inner // 2, pair, 0)
    for parity in range(2):
        @pl.when(jnp.logical_and(qi > 0, qi % 2 == parity))
        def _():
            if parity == 0:
                step(qi - 1, 1)
            step(qi, parity, keep=causal)

        @pl.when(qi % 2 == parity)
        def _():
            values(qi, parity)

    lam = _lambda(lq1_ref, lk1_ref, lq2_ref, lk2_ref, lam_init)
    a1, a2 = acc_ref[0], acc_ref[1]
    out = a1[:DA_DV] / a1[DA_DV:DA_DV + 1] - lam * (a2[:DA_DV] / a2[DA_DV:DA_DV + 1])
    ms = jnp.mean(out * out, axis=0, keepdims=True)
    out = out * lax.rsqrt(ms + NORM_EPS) * _col_bcast(gsub_ref[...], t) * (1.0 - lam_init)
    o_ref[...] = out.T.astype(BF16)


def _attn_prompt(qb, kb, v, lams, gsub, batch, lam_init):
    n = qb.shape[0]
    seq = n // batch
    nq = seq // ATT_TILE
    small = lambda a: pl.BlockSpec(a.shape, lambda b, h, i: (0, 0))
    return pl.pallas_call(
        functools.partial(_attn_prompt_kernel, lam_init=lam_init),
        grid=(batch, DA_HEADS, nq),
        in_specs=[pl.BlockSpec((ATT_TILE, 2 * DA_DH), lambda b, h, i: (b * nq + i, h)),
                  pl.BlockSpec((seq, 2 * DA_DH), lambda b, h, i: (b, h)),
                  pl.BlockSpec((seq * DA_HEADS, DA_DV), lambda b, h, i: (b, 0))]
                 + [small(a) for a in lams] + [small(gsub)],
        out_specs=pl.BlockSpec((ATT_TILE, DA_DV), lambda b, h, i: (b * nq + i, h)),
        out_shape=jax.ShapeDtypeStruct((n, DA_HEADS * DA_DV), BF16),
        scratch_shapes=[pltpu.VMEM((2, nq, ATT_TILE, 2 * DA_DH), BF16),
                        pltpu.VMEM((nq, ATT_VROWS, ATT_TILE), BF16),
                        pltpu.VMEM((2, 1, ATT_TILE), F32),
                        pltpu.VMEM((2, ATT_VROWS, ATT_TILE), F32),
                        pltpu.VMEM((2, 2, ATT_TILE, ATT_TILE), BF16),
                        pltpu.VMEM((2, 2, 1, ATT_TILE), F32)],
        compiler_params=pltpu.CompilerParams(
            dimension_semantics=("arbitrary", "arbitrary", "arbitrary")),
        name="attn_prompt",
    )(qb, kb, v, *lams, gsub)


PAGE_GROUP = 8


def _attn_sample_kernel(pt_ref, q_ref, kn_ref, vn_ref, lq1_ref, lk1_ref, lq2_ref, lk2_ref,
                        gsub_ref, kt_hbm, v_hbm, o_ref, kbuf, vbuf, sem, qcol_ref, expand_ref,
                        m_ref, l_ref, acc_ref, *, lam_init):
    b = pl.program_id(0)
    nb = pl.num_programs(0)
    nmap = 2 * DA_HEADS
    page = kbuf.shape[-1]
    vrows = vbuf.shape[-2]
    n_groups = pt_ref.shape[1] // PAGE_GROUP
    past_len = pt_ref.shape[1] * page
    mrow = lax.broadcasted_iota(jnp.int32, (nmap, 1), 0)
    slope = jnp.exp2((-8.0 / DA_HEADS) * ((mrow // 2) + 1).astype(F32))
    own = (lax.broadcasted_iota(jnp.int32, (1, vrows), 1) % DA_HEADS) == (mrow // 2)

    def group_copies(seq, g, slot):
        copies = []
        for i in range(PAGE_GROUP):
            pg = pt_ref[seq, g * PAGE_GROUP + i]
            copies.append(pltpu.make_async_copy(kt_hbm.at[0, pg], kbuf.at[slot, i], sem.at[slot, 0]))
            copies.append(pltpu.make_async_copy(v_hbm.at[0, pg], vbuf.at[slot, i], sem.at[slot, 1]))
        return copies

    def start_group(seq, g, slot):
        for n, c in enumerate(group_copies(seq, g, slot)):
            c.start(priority=n % 2)

    @pl.when(b == 0)
    def _():
        start_group(0, 0, 0)
        tok = lax.broadcasted_iota(jnp.int32, (page, vrows), 0)
        vrow = lax.broadcasted_iota(jnp.int32, (page, vrows), 1)
        expand_ref[...] = jnp.where(vrow // DA_HEADS == tok, 1.0, 0.0).astype(BF16)

    qcol = _col_bcast(q_ref[0].astype(F32), page)
    qcol_ref[...] = qcol
    kcol = _col_bcast(kn_ref[0], page)
    m_ref[...] = jnp.sum((qcol * kcol).reshape(nmap, DA_DH, page), axis=1)
    l_ref[...] = jnp.ones_like(l_ref)
    vn = vn_ref[0]
    acc_ref[...] = jnp.concatenate(
        [vn[:, (r // 2) * DA_DV:(r // 2 + 1) * DA_DV] for r in range(nmap)], axis=0)
    lane = lax.broadcasted_iota(jnp.int32, (1, page), 1)

    def group(g, carry):
        slot = g % 2

        @pl.when(g + 1 < n_groups)
        def _():
            start_group(b, g + 1, 1 - slot)

        @pl.when(jnp.logical_and(g + 1 == n_groups, b + 1 < nb))
        def _():
            start_group(b + 1, 0, 1 - slot)

        for c in group_copies(b, g, slot):
            c.wait()
        qc = qcol_ref[...].reshape(nmap, DA_DH, page)
        scores = []
        for i in range(PAGE_GROUP):
            s = jnp.sum(kbuf[slot, i] * qc, axis=1)
            pos = (g * PAGE_GROUP + i) * page + lane
            scores.append(s - slope * (past_len - pos).astype(F32))
        m_old = m_ref[...]
        m_new = m_old
        for s in scores:
            m_new = jnp.maximum(m_new, jnp.max(s, axis=-1, keepdims=True))
        alpha = jnp.exp(m_old - m_new)
        l_new = alpha * l_ref[...]
        acc = alpha * acc_ref[...]
        probs = []
        for s in scores:
            p = jnp.exp(s - m_new)
            l_new = l_new + jnp.sum(p, axis=-1, keepdims=True)
            probs.append(p.astype(BF16))
        spread = jnp.dot(jnp.concatenate(probs, axis=0), expand_ref[...],
                         preferred_element_type=F32)
        for i in range(PAGE_GROUP):
            w = jnp.where(own, spread[i * nmap:(i + 1) * nmap], 0.0).astype(BF16)
            acc = acc + jnp.dot(w, vbuf[slot, i].astype(BF16), preferred_element_type=F32)
        m_ref[...] = m_new
        l_ref[...] = l_new
        acc_ref[...] = acc
        return carry

    lax.fori_loop(0, n_groups, group, 0)
    lam = _lambda(lq1_ref, lk1_ref, lq2_ref, lk2_ref, lam_init)
    acc, l_fin = acc_ref[...], l_ref[...]
    outs = []
    for h in range(DA_HEADS):
        r1, r2 = 2 * h, 2 * h + 1
        outs.append(_diff_combine(acc[r1:r1 + 1], l_fin[r1:r1 + 1], acc[r2:r2 + 1],
                                  l_fin[r2:r2 + 1], lam, gsub_ref[...], lam_init))
    o_ref[0] = jnp.concatenate(outs, axis=-1).astype(BF16)


def _attn_sample(qb, k_new, v_new, kt_pages, v_pages, page_table, lams, gsub, lam_init):
    nb, n_pages = page_table.shape
    page = kt_pages.shape[-1]
    vrows = v_pages.shape[-2]
    nmap = 2 * DA_HEADS
    assert (n_pages // PAGE_GROUP) % 2 == 0 and n_pages % PAGE_GROUP == 0
    row = lambda w: pl.BlockSpec((1, 1, w), lambda b, pt: (b, 0, 0))
    small = lambda a: pl.BlockSpec(a.shape, lambda b, pt: (0, 0))
    grid_spec = pltpu.PrefetchScalarGridSpec(
        num_scalar_prefetch=1,
        grid=(nb,),
        in_specs=[row(DA_WIDTH), row(DA_WIDTH), row(DA_HEADS * DA_DV)]
                 + [small(a) for a in lams] + [small(gsub)]
                 + [pl.BlockSpec(memory_space=pl.ANY), pl.BlockSpec(memory_space=pl.ANY)],
        out_specs=row(DA_HEADS * DA_DV),
        scratch_shapes=[pltpu.VMEM((2, PAGE_GROUP, nmap, DA_DH, page), F32),
                        pltpu.VMEM((2, PAGE_GROUP, vrows, DA_DV), F32),
                        pltpu.SemaphoreType.DMA((2, 2)),
                        pltpu.VMEM((DA_WIDTH, page), F32), pltpu.VMEM((page, vrows), BF16),
                        pltpu.VMEM((nmap, page), F32), pltpu.VMEM((nmap, page), F32),
                        pltpu.VMEM((nmap, DA_DV), F32)],
    )
    out = pl.pallas_call(
        functools.partial(_attn_sample_kernel, lam_init=lam_init),
        grid_spec=grid_spec,
        out_shape=jax.ShapeDtypeStruct((nb, 1, DA_HEADS * DA_DV), BF16),
        compiler_params=pltpu.CompilerParams(dimension_semantics=("arbitrary",)),
        name="attn_sample",
    )(page_table, qb.reshape(nb, 1, DA_WIDTH), k_new.reshape(nb, 1, DA_WIDTH),
      v_new.reshape(nb, 1, DA_HEADS * DA_DV), *lams, gsub, kt_pages, v_pages)
    return out.reshape(nb, DA_HEADS * DA_DV)


PROMPT_ROWS = 256


def kernel(x_prompt, x_sample, cache_k, cache_v, state_hgrn, page_table, ffn1_norm, ffn1_w_gate, ffn1_w_up, ffn1_w_down, mix_norm, w_in, hg_lb_logits, hg_out_norm, da_q_norm, da_k_norm, da_lambda_q1, da_lambda_k1, da_lambda_q2, da_lambda_k2, da_subln, w_out, ffn2_norm, ffn2_w_gate, ffn2_w_up, ffn2_w_down):
    batch, seq, d = x_prompt.shape
    nb = x_sample.shape[0]
    depth = ffn1_norm.shape[0]
    nmap = 2 * DA_HEADS
    yp = x_prompt.reshape(batch * seq, d)
    ys = x_sample.reshape(nb, d)
    kt_cache = jnp.transpose(cache_k, (0, 1, 3, 4, 2))
    v_cache = cache_v.reshape(cache_v.shape[:2] + (-1, DA_DV))
    outs = [[] for _ in range(6)]
    for l in range(depth):
        lam_init = 0.8 - 0.6 * math.exp(-0.3 * l)
        bf = lambda w: w[l].astype(BF16)
        w1 = (bf(ffn1_w_gate), bf(ffn1_w_up), bf(ffn1_w_down))
        w2 = (bf(ffn2_w_gate), bf(ffn2_w_up), bf(ffn2_w_down))
        win, wo = bf(w_in), bf(w_out)
        gq = jnp.tile(da_q_norm[l:l + 1], (1, nmap))
        gk = jnp.tile(da_k_norm[l:l + 1], (1, nmap))
        lams = (da_lambda_q1[l:l + 1], da_lambda_k1[l:l + 1], da_lambda_q2[l:l + 1],
                da_lambda_k2[l:l + 1])
        gout, gsub = hg_out_norm[l:l + 1], da_subln[l:l + 1]

        def inproj(x, tm, rows_per_seq):
            return _ffn_inproj(x, ffn1_norm[l:l + 1], *w1, mix_norm[l:l + 1], win, gq, gk, tm,
                               rows_per_seq)

        def outproj(h, ohg, oda, tm):
            return _outproj_ffn(h, ohg, oda, wo, ffn2_norm[l:l + 1], *w2, tm)

        def keys(kt):
            return jnp.transpose(kt.reshape(kt.shape[0], nmap, DA_DH, kt.shape[2]), (0, 3, 1, 2))

        h, zhg, qb, kt, kb, v = inproj(yp, PROMPT_ROWS, seq)
        ohg, s_fin = _hgrn_prompt(zhg, hg_lb_logits, gout, batch, l)
        oda = _attn_prompt(qb, kb, v, lams, gsub, batch, lam_init)
        yp = outproj(h, ohg, oda, PROMPT_ROWS)
        outs[0].append(keys(kt))
        outs[1].append(v.reshape(batch, seq, DA_HEADS, DA_DV))
        outs[2].append(s_fin)
        h, zhg, qb, kt, _, v = inproj(ys, nb, nb)
        k = keys(kt).reshape(nb, DA_WIDTH)
        ohg, s_new = _hgrn_sample(zhg, state_hgrn[l], hg_lb_logits, gout, l)
        oda = _attn_sample(qb, k, v, kt_cache[l:l + 1], v_cache[l:l + 1], page_table, lams, gsub,
                           lam_init)
        ys = outproj(h, ohg.reshape(nb, HG_WIDTH), oda, nb)
        outs[3].append(k.reshape(nb, 1, nmap, DA_DH))
        outs[4].append(v.reshape(nb, 1, DA_HEADS, DA_DV))
        outs[5].append(s_new)
    stacked = [jnp.stack(o, axis=0) for o in outs]
    return (yp.reshape(batch, seq, d), ys.reshape(nb, 1, d), *stacked)
```

```python
import functools
import math

import jax
import jax.numpy as jnp
from jax import lax
from jax.experimental import pallas as pl
from jax.experimental.pallas import tpu as pltpu

F32 = jnp.float32
BF16 = jnp.bfloat16

NORM_EPS = 1e-6
NEG_INF = -1e30
LANES = 128
SUBLANES = 8
HG_HEADS = 4
HG_DK = 128
HG_DV = 128
HG_WIDTH = HG_HEADS * HG_DK
DA_HEADS = 4
DA_DH = 64
DA_DV = 128
DA_WIDTH = 2 * DA_HEADS * DA_DH
DA_SCALE = DA_DH ** -0.5
HG_COLS = 4 * HG_WIDTH
VMEM_LIMIT = 56 * 1024 * 1024

_NT = (((1,), (1,)), ((), ()))
_TN = (((0,), (0,)), ((), ()))


def _rmsnorm(x, g):
    ms = jnp.mean(x * x, axis=-1, keepdims=True)
    return x * lax.rsqrt(ms + NORM_EPS) * g


def _sigmoid(x):
    return 0.5 * jnp.tanh(0.5 * x) + 0.5


def _silu(x):
    return x * _sigmoid(x)


def _split3(x):
    hi = x.astype(BF16)
    r1 = x - hi.astype(F32)
    mid = r1.astype(BF16)
    lo = (r1 - mid.astype(F32)).astype(BF16)
    return hi, mid, lo


def _dot_exact_rhs(sel, x):
    hi, mid, lo = _split3(x)
    return (jnp.dot(sel, hi, preferred_element_type=F32)
            + jnp.dot(sel, mid, preferred_element_type=F32)
            + jnp.dot(sel, lo, preferred_element_type=F32))


def _dot_exact_lhs(x, sel):
    hi, mid, lo = _split3(x)
    return (jnp.dot(hi, sel, preferred_element_type=F32)
            + jnp.dot(mid, sel, preferred_element_type=F32)
            + jnp.dot(lo, sel, preferred_element_type=F32))


def _col_bcast(row, width=LANES):
    n = row.shape[1]
    r = lax.broadcasted_iota(jnp.int32, (n, n), 0)
    c = lax.broadcasted_iota(jnp.int32, (n, n), 1)
    diag = jnp.where(r == c, jnp.broadcast_to(row, (n, n)), 0.0)
    return _dot_exact_lhs(diag, jnp.ones((n, width), BF16))


def _group64_rmsnorm(x, g):
    m, w = x.shape
    r = lax.broadcasted_iota(jnp.int32, (LANES, LANES), 0) // DA_DH
    c = lax.broadcasted_iota(jnp.int32, (LANES, LANES), 1) // DA_DH
    same = jnp.where(r == c, 1.0, 0.0).astype(BF16)
    x2 = x * x
    parts = [_dot_exact_lhs(x2[:, i * LANES:(i + 1) * LANES], same) for i in range(w // LANES)]
    ss = jnp.concatenate(parts, axis=-1)
    return x * lax.rsqrt(ss * (1.0 / DA_DH) + NORM_EPS) * g


def _lower_bound(logits, layer):
    m = jnp.max(logits, axis=0, keepdims=True)
    e = jnp.exp(logits - m)
    return jnp.sum(e[:layer + 1], axis=0, keepdims=True) / jnp.sum(e, axis=0, keepdims=True)


def _lambda(lq1_ref, lk1_ref, lq2_ref, lk2_ref, lam_init):
    a = jnp.sum(lq1_ref[...] * lk1_ref[...], axis=-1, keepdims=True)
    b = jnp.sum(lq2_ref[...] * lk2_ref[...], axis=-1, keepdims=True)
    return jnp.exp(a) - jnp.exp(b) + lam_init


FFN_PART = 1024


def _ffn_parts(dff):
    return [(lo, min(lo + FFN_PART, dff)) for lo in range(0, dff, FFN_PART)]


class _NoSide:
    def pre(self, j):
        pass

    def post(self, j):
        pass


def _swiglu_parts(xn, wg_ref, wu_ref, wd_ref, side):
    y = None
    for j, (lo, hi) in enumerate(_ffn_parts(wg_ref.shape[1])):
        side.pre(j)
        g = jnp.dot(xn, wg_ref[:, lo:hi], preferred_element_type=F32)
        u = jnp.dot(xn, wu_ref[:, lo:hi], preferred_element_type=F32)
        a = (_silu(g) * u).astype(BF16)
        part = jnp.dot(a, wd_ref[lo:hi, :], preferred_element_type=F32)
        y = part if y is None else y + part
        side.post(j)
    return y


def _ffn_inproj_body(x_ref, g1_ref, wg_ref, wu_ref, wd_ref, gm_ref, win_ref, gq_ref, gk_ref,
                     h_ref, zhg_ref, q_ref, kt_ref, kb_ref, v_ref, side):
    tm = x_ref.shape[0]
    last = len(_ffn_parts(wg_ref.shape[1]))
    x = x_ref[...]
    y = _swiglu_parts(_rmsnorm(x, g1_ref[...]).astype(BF16), wg_ref, wu_ref, wd_ref, side)
    side.pre(last)
    h = x + 0.5 * y
    h_ref[...] = h
    z = jnp.dot(_rmsnorm(h, gm_ref[...]).astype(BF16), win_ref[...], preferred_element_type=F32)
    zhg_ref[...] = z[:, :HG_COLS]
    dq = z[:, HG_COLS:HG_COLS + DA_WIDTH]
    dk = z[:, HG_COLS + DA_WIDTH:HG_COLS + 2 * DA_WIDTH]
    dv = z[:, HG_COLS + 2 * DA_WIDTH:]
    q_ref[...] = (_group64_rmsnorm(dq, gq_ref[...]) * DA_SCALE).astype(BF16)
    k = _group64_rmsnorm(dk, gk_ref[...])
    kt_ref[0] = k.T
    kb_ref[...] = k.astype(BF16)
    for hd in range(DA_HEADS):
        v_ref[pl.ds(hd, tm, stride=DA_HEADS), :] = dv[:, hd * DA_DV:(hd + 1) * DA_DV]
    side.post(last)


def _outproj_ffn_body(h_ref, ohg_ref, oda_ref, wo_ref, g2_ref, wg_ref, wu_ref, wd_ref, y_ref, side):
    last = len(_ffn_parts(wg_ref.shape[1]))
    h = (h_ref[...]
         + jnp.dot(ohg_ref[...], wo_ref[:HG_WIDTH, :], preferred_element_type=F32)
         + jnp.dot(oda_ref[...], wo_ref[HG_WIDTH:, :], preferred_element_type=F32))
    y = _swiglu_parts(_rmsnorm(h, g2_ref[...]).astype(BF16), wg_ref, wu_ref, wd_ref, side)
    side.pre(last)
    y_ref[...] = h + 0.5 * y
    side.post(last)


def _ffn_inproj_kernel(*refs):
    _ffn_inproj_body(*refs, _NoSide())


def _outproj_ffn_kernel(*refs):
    _outproj_ffn_body(*refs, _NoSide())


PAGE_GROUP = 8


class _PageWalk:
    def __init__(self, s, n_steps, pt_ref, kt_hbm, v_hbm, kbuf, vbuf, sem, first_page, n_pages):
        nb = pt_ref.shape[0]
        assert n_steps % nb == 0
        self.steps_per_seq = n_steps // nb
        assert n_pages % (PAGE_GROUP * self.steps_per_seq) == 0
        self.groups = n_pages // (PAGE_GROUP * self.steps_per_seq)
        assert self.groups % 2 == 0
        self.s, self.n_steps, self.first_page = s, n_steps, first_page
        self.pt_ref, self.kt_hbm, self.v_hbm = pt_ref, kt_hbm, v_hbm
        self.kbuf, self.vbuf, self.sem = kbuf, vbuf, sem
        self.seq_first = (s % self.steps_per_seq) == 0
        self.seq_last = (s % self.steps_per_seq) == self.steps_per_seq - 1

    def first_page_of(self, step, g):
        return self.first_page + ((step % self.steps_per_seq) * self.groups + g) * PAGE_GROUP

    def _copies(self, step, g):
        seq = step // self.steps_per_seq
        page0 = self.first_page_of(step, g)
        slot = g % 2
        copies = []
        for i in range(PAGE_GROUP):
            pg = self.pt_ref[seq, page0 + i]
            copies.append(pltpu.make_async_copy(self.kt_hbm.at[0, pg], self.kbuf.at[slot, i],
                                                self.sem.at[slot, 0]))
            copies.append(pltpu.make_async_copy(self.v_hbm.at[0, pg], self.vbuf.at[slot, i],
                                                self.sem.at[slot, 1]))
        return copies

    def prime(self):
        for g in range(2):
            for c in self._copies(0, g):
                c.start()

    def wait(self, g):
        for c in self._copies(self.s, g):
            c.wait()

    def start_ahead(self, g):
        step = self.s + (g + 2) // self.groups

        @pl.when(step < self.n_steps)
        def _():
            for c in self._copies(step, (g + 2) % self.groups):
                c.start()


def _fold_pages(kbuf, vbuf, slot, first_pos, past_len, qcol_ref, expand_ref, m_ref, l_ref, acc_ref):
    nmap = 2 * DA_HEADS
    page = kbuf.shape[-1]
    vrows = vbuf.shape[-2]
    mrow = lax.broadcasted_iota(jnp.int32, (nmap, 1), 0)
    slope = jnp.exp2((-8.0 / DA_HEADS) * ((mrow // 2) + 1).astype(F32))
    own = (lax.broadcasted_iota(jnp.int32, (1, vrows), 1) % DA_HEADS) == (mrow // 2)
    lane = lax.broadcasted_iota(jnp.int32, (1, page), 1)
    qc = qcol_ref[...].reshape(nmap, DA_DH, page)
    scores = []
    for i in range(PAGE_GROUP):
        s = jnp.sum(kbuf[slot, i] * qc, axis=1)
        pos = first_pos + i * page + lane
        scores.append(s - slope * (past_len - pos).astype(F32))
    m_old = m_ref[...]
    m_new = m_old
    for s in scores:
        m_new = jnp.maximum(m_new, jnp.max(s, axis=-1, keepdims=True))
    alpha = jnp.exp(m_old - m_new)
    l_new = alpha * l_ref[...]
    acc = alpha * acc_ref[...]
    probs = []
    for s in scores:
        p = jnp.exp(s - m_new)
        l_new = l_new + jnp.sum(p, axis=-1, keepdims=True)
        probs.append(p.astype(BF16))
    spread = jnp.dot(jnp.concatenate(probs, axis=0), expand_ref[...],
                     preferred_element_type=F32)
    for i in range(PAGE_GROUP):
        w = jnp.where(own, spread[i * nmap:(i + 1) * nmap], 0.0).astype(BF16)
        acc = acc + jnp.dot(w, vbuf[slot, i].astype(BF16), preferred_element_type=F32)
    m_ref[...] = m_new
    l_ref[...] = l_new
    acc_ref[...] = acc


class _PageSide:
    def __init__(self, walk, n_stages, past_len, qcol_ref, expand_ref, m_ref, l_ref, acc_ref,
                 begin, finish):
        self.walk, self.n_stages, self.past_len = walk, n_stages, past_len
        self.state = (qcol_ref, expand_ref, m_ref, l_ref, acc_ref)
        self.begin, self.finish = begin, finish

    def _groups(self, j):
        return range(j, self.walk.groups, self.n_stages)

    def pre(self, j):
        walk = self.walk
        if j == 0:
            page, vrows = walk.kbuf.shape[-1], walk.vbuf.shape[-2]
            expand_ref = self.state[1]

            @pl.when(walk.s == 0)
            def _():
                walk.prime()
                tok = lax.broadcasted_iota(jnp.int32, (page, vrows), 0)
                vrow = lax.broadcasted_iota(jnp.int32, (page, vrows), 1)
                expand_ref[...] = jnp.where(vrow // DA_HEADS == tok, 1.0, 0.0).astype(BF16)

            pl.when(walk.seq_first)(self.begin)
        for g in self._groups(j):
            walk.wait(g)

    def post(self, j):
        walk = self.walk
        page = walk.kbuf.shape[-1]
        for g in self._groups(j):
            _fold_pages(walk.kbuf, walk.vbuf, g % 2, walk.first_page_of(walk.s, g) * page,
                        self.past_len, *self.state)
            walk.start_ahead(g)
        if j == self.n_stages - 1:
            pl.when(walk.seq_last)(self.finish)


def _ffn_inproj_pages_kernel(pt_ref, x_ref, g1_ref, wg_ref, wu_ref, wd_ref, gm_ref, win_ref, gq_ref,
                             gk_ref, qs_ref, kn_ref, vn_ref, kt_hbm, v_hbm,
                             h_ref, zhg_ref, q_ref, kt_ref, kb_ref, v_ref, mo_ref, lo_ref, ao_ref,
                             kbuf, vbuf, sem, qcol_ref, expand_ref, m_ref, l_ref, acc_ref, *, n_steps, n_pages):
    nmap = 2 * DA_HEADS
    page = kbuf.shape[-1]
    walk = _PageWalk(pl.program_id(0), n_steps, pt_ref, kt_hbm, v_hbm, kbuf, vbuf, sem,
                     0, n_pages)

    def begin():
        qcol = _col_bcast(qs_ref[0].astype(F32), page)
        qcol_ref[...] = qcol
        kcol = _col_bcast(kn_ref[0], page)
        m_ref[...] = jnp.sum((qcol * kcol).reshape(nmap, DA_DH, page), axis=1)
        l_ref[...] = jnp.ones_like(l_ref)
        vn = vn_ref[0]
        acc_ref[...] = jnp.concatenate(
            [vn[:, (r // 2) * DA_DV:(r // 2 + 1) * DA_DV] for r in range(nmap)], axis=0)

    def finish():
        mo_ref[0] = m_ref[...]
        lo_ref[0] = l_ref[...]
        ao_ref[0] = acc_ref[...]

    n_stages = len(_ffn_parts(wg_ref.shape[1])) + 1
    side = _PageSide(walk, n_stages, pt_ref.shape[1] * page, qcol_ref, expand_ref, m_ref, l_ref,
                     acc_ref, begin, finish)
    _ffn_inproj_body(x_ref, g1_ref, wg_ref, wu_ref, wd_ref, gm_ref, win_ref, gq_ref, gk_ref,
                     h_ref, zhg_ref, q_ref, kt_ref, kb_ref, v_ref, side)


def _outproj_ffn_pages_kernel(pt_ref, h_ref, ohg_ref, oda_ref, wo_ref, g2_ref, wg_ref, wu_ref, wd_ref,
                              qs_ref, mi_ref, li_ref, ai_ref, lq1_ref, lk1_ref, lq2_ref, lk2_ref,
                              gsub_ref, kt_hbm, v_hbm, y_ref, os_ref,
                              kbuf, vbuf, sem, qcol_ref, expand_ref, m_ref, l_ref, acc_ref,
                              *, n_steps, first_page, lam_init):
    page = kbuf.shape[-1]
    n_pages = pt_ref.shape[1]
    walk = _PageWalk(pl.program_id(0), n_steps, pt_ref, kt_hbm, v_hbm, kbuf, vbuf, sem,
                     first_page, n_pages - first_page)

    def begin():
        qcol_ref[...] = _col_bcast(qs_ref[0].astype(F32), page)
        m_ref[...] = mi_ref[0]
        l_ref[...] = li_ref[0]
        acc_ref[...] = ai_ref[0]

    def finish():
        lam = _lambda(lq1_ref, lk1_ref, lq2_ref, lk2_ref, lam_init)
        acc, l_fin = acc_ref[...], l_ref[...]
        outs = []
        for hd in range(DA_HEADS):
            r1, r2 = 2 * hd, 2 * hd + 1
            outs.append(_diff_combine(acc[r1:r1 + 1], l_fin[r1:r1 + 1], acc[r2:r2 + 1],
                                      l_fin[r2:r2 + 1], lam, gsub_ref[...], lam_init))
        os_ref[0] = jnp.concatenate(outs, axis=-1).astype(BF16)

    n_stages = len(_ffn_parts(wg_ref.shape[1])) + 1
    side = _PageSide(walk, n_stages, n_pages * page, qcol_ref, expand_ref, m_ref, l_ref, acc_ref,
                     begin, finish)
    _outproj_ffn_body(h_ref, ohg_ref, oda_ref, wo_ref, g2_ref, wg_ref, wu_ref, wd_ref, y_ref, side)


def _resident(shape):
    return pl.BlockSpec(shape, lambda *_: (0,) * len(shape), pipeline_mode=pl.Buffered(1))


def _page_scratch(page, vrows):
    nmap = 2 * DA_HEADS
    return [pltpu.VMEM((2, PAGE_GROUP, nmap, DA_DH, page), F32),
            pltpu.VMEM((2, PAGE_GROUP, vrows, DA_DV), F32),
            pltpu.SemaphoreType.DMA((2, 2)),
            pltpu.VMEM((DA_WIDTH, page), F32), pltpu.VMEM((page, vrows), BF16),
            pltpu.VMEM((nmap, page), F32), pltpu.VMEM((nmap, page), F32),
            pltpu.VMEM((nmap, DA_DV), F32)]


def _ffn_inproj(x, g1, wg, wu, wd, gm, win, gq, gk, tm, seq, pages=None):
    n, d = x.shape
    dff = wg.shape[1]
    tiles = seq // tm
    rows = lambda w: pl.BlockSpec((tm, w), lambda i, *_: (i, 0))
    in_specs = [rows(d), _resident((1, d)), _resident((d, dff)), _resident((d, dff)),
                _resident((dff, d)), _resident((1, d)), _resident(win.shape),
                _resident((1, DA_WIDTH)), _resident((1, DA_WIDTH))]
    out_specs = [rows(d), rows(HG_COLS), rows(DA_WIDTH),
                 pl.BlockSpec((1, DA_WIDTH, tm), lambda i, *_: (i // tiles, 0, i % tiles)),
                 rows(DA_WIDTH),
                 pl.BlockSpec((tm * DA_HEADS, DA_DV), lambda i, *_: (i, 0))]
    out_shape = [jax.ShapeDtypeStruct((n, d), F32), jax.ShapeDtypeStruct((n, HG_COLS), F32),
                 jax.ShapeDtypeStruct((n, DA_WIDTH), BF16),
                 jax.ShapeDtypeStruct((n // seq, DA_WIDTH, seq), F32),
                 jax.ShapeDtypeStruct((n, DA_WIDTH), BF16),
                 jax.ShapeDtypeStruct((n * DA_HEADS, DA_DV), F32)]
    params = pltpu.CompilerParams(dimension_semantics=("arbitrary",), vmem_limit_bytes=VMEM_LIMIT)
    args = (x, g1, wg, wu, wd, gm, win, gq, gk)
    if pages is None:
        return pl.pallas_call(_ffn_inproj_kernel, grid=(n // tm,), in_specs=in_specs,
                              out_specs=out_specs, out_shape=out_shape, compiler_params=params,
                              name="ffn_inproj")(*args)
    page_table, qs, kn, vn, kt_pages, v_pages, n_pages = pages
    nb = page_table.shape[0]
    nmap = 2 * DA_HEADS
    page, vrows = kt_pages.shape[-1], v_pages.shape[-2]
    per_seq = (n // tm) // nb
    seq_row = lambda w: pl.BlockSpec((1, 1, w), lambda i, *_: (i // per_seq, 0, 0))
    state = lambda w: pl.BlockSpec((1, nmap, w), lambda i, *_: (i // per_seq, 0, 0))
    grid_spec = pltpu.PrefetchScalarGridSpec(
        num_scalar_prefetch=1, grid=(n // tm,),
        in_specs=in_specs + [seq_row(DA_WIDTH), seq_row(DA_WIDTH), seq_row(DA_HEADS * DA_DV),
                             pl.BlockSpec(memory_space=pl.ANY), pl.BlockSpec(memory_space=pl.ANY)],
        out_specs=out_specs + [state(page), state(page), state(DA_DV)],
        scratch_shapes=_page_scratch(page, vrows))
    out_shape = out_shape + [jax.ShapeDtypeStruct((nb, nmap, page), F32),
                             jax.ShapeDtypeStruct((nb, nmap, page), F32),
                             jax.ShapeDtypeStruct((nb, nmap, DA_DV), F32)]
    return pl.pallas_call(
        functools.partial(_ffn_inproj_pages_kernel, n_steps=n // tm, n_pages=n_pages),
        grid_spec=grid_spec, out_shape=out_shape, compiler_params=params, name="ffn_inproj_pages",
    )(page_table, *args, qs.reshape(nb, 1, DA_WIDTH), kn.reshape(nb, 1, DA_WIDTH),
      vn.reshape(nb, 1, DA_HEADS * DA_DV), kt_pages, v_pages)


def _outproj_ffn(h, ohg, oda, wo, g2, wg, wu, wd, tm, pages=None):
    n, d = h.shape
    dff = wg.shape[1]
    rows = lambda w: pl.BlockSpec((tm, w), lambda i, *_: (i, 0))
    in_specs = [rows(d), rows(HG_WIDTH), rows(DA_WIDTH), _resident(wo.shape), _resident((1, d)),
                _resident((d, dff)), _resident((d, dff)), _resident((dff, d))]
    params = pltpu.CompilerParams(dimension_semantics=("arbitrary",), vmem_limit_bytes=VMEM_LIMIT)
    args = (h, ohg, oda, wo, g2, wg, wu, wd)
    if pages is None:
        return pl.pallas_call(_outproj_ffn_kernel, grid=(n // tm,), in_specs=in_specs,
                              out_specs=rows(d), out_shape=jax.ShapeDtypeStruct((n, d), F32),
                              compiler_params=params, name="outproj_ffn")(*args)
    page_table, qs, m, l, acc, lams, gsub, kt_pages, v_pages, first_page, lam_init = pages
    nb = page_table.shape[0]
    nmap = 2 * DA_HEADS
    page, vrows = kt_pages.shape[-1], v_pages.shape[-2]
    per_seq = (n // tm) // nb
    seq_row = lambda w: pl.BlockSpec((1, 1, w), lambda i, *_: (i // per_seq, 0, 0))
    state = lambda w: pl.BlockSpec((1, nmap, w), lambda i, *_: (i // per_seq, 0, 0))
    small = lambda a: pl.BlockSpec(a.shape, lambda i, *_: (0, 0))
    grid_spec = pltpu.PrefetchScalarGridSpec(
        num_scalar_prefetch=1, grid=(n // tm,),
        in_specs=in_specs + [seq_row(DA_WIDTH), state(page), state(page), state(DA_DV)]
                 + [small(a) for a in lams] + [small(gsub)]
                 + [pl.BlockSpec(memory_space=pl.ANY), pl.BlockSpec(memory_space=pl.ANY)],
        out_specs=[rows(d), seq_row(DA_HEADS * DA_DV)],
        scratch_shapes=_page_scratch(page, vrows))
    y, o_s = pl.pallas_call(
        functools.partial(_outproj_ffn_pages_kernel, n_steps=n // tm, first_page=first_page,
                          lam_init=lam_init),
        grid_spec=grid_spec,
        out_shape=[jax.ShapeDtypeStruct((n, d), F32),
                   jax.ShapeDtypeStruct((nb, 1, DA_HEADS * DA_DV), BF16)],
        compiler_params=params, name="outproj_ffn_pages",
    )(page_table, *args, qs.reshape(nb, 1, DA_WIDTH), m, l, acc, *lams, gsub, kt_pages, v_pages)
    return y, o_s.reshape(nb, DA_HEADS * DA_DV)


HG_CHUNK = 128
HG_BLOCK = 4
HG_GROUP = 4


def _hgrn_gates(z, lb):
    xq = z[:, :HG_WIDTH]
    xf = z[:, HG_WIDTH:2 * HG_WIDTH]
    xi = z[:, 2 * HG_WIDTH:3 * HG_WIDTH]
    xg = z[:, 3 * HG_WIDTH:]
    q = _silu(xq) * (HG_DK ** -0.5)
    f = lb + (1.0 - lb) * _sigmoid(xf)
    return q, 1.0 - f, f, xi, _silu(xg)


def _hgrn_chunk_kernel(z_ref, lbl_ref, gout_ref, o_ref, sfin_ref, st_ref, *, layer):
    c = pl.program_id(1)
    group, n = z_ref.shape[0], z_ref.shape[1]

    @pl.when(c == 0)
    def _():
        st_ref[...] = jnp.zeros_like(st_ref)

    lb = _lower_bound(lbl_ref[...], layer)
    row = lax.broadcasted_iota(jnp.int32, (n, n), 0)
    col = lax.broadcasted_iota(jnp.int32, (n, n), 1)
    tri = jnp.where(row >= col, 1.0, 0.0).astype(BF16)
    trow = lax.broadcasted_iota(jnp.int32, (n, 1), 0)
    spans = []
    w = HG_BLOCK
    while 2 * w <= n:
        span = 2 * w
        spans.append((w, span, (trow % span) >= w,
                      ((row // span) == (col // span)) & ((row % span) >= w) & ((col % span) < w)))
        w = span
    nears = [((row - col) == d) & ((row % HG_BLOCK) >= d) for d in range(HG_BLOCK)]

    for r in range(group):
        q, k, f, v, gate = _hgrn_gates(z_ref[r], lb)
        b = _dot_exact_rhs(tri, jnp.log2(f))
        a = [jnp.zeros((n, n), F32) for _ in range(HG_HEADS)]
        for w, span, right, lvl in spans:
            ref = jnp.concatenate(
                [jnp.broadcast_to(b[p * span + w - 1:p * span + w, :], (span, HG_WIDTH))
                 for p in range(n // span)], axis=0)
            e = jnp.exp2(-jnp.abs(b - ref))
            qw = jnp.where(right, q * e, 0.0).astype(BF16)
            kw = jnp.where(right, 0.0, k * e).astype(BF16)
            for h in range(HG_HEADS):
                hs = slice(h * HG_DK, (h + 1) * HG_DK)
                p = lax.dot_general(qw[:, hs], kw[:, hs], _NT, preferred_element_type=F32)
                a[h] = jnp.where(lvl, p, a[h])
        for d in range(HG_BLOCK):
            kd = pltpu.roll(k, d, axis=0) if d else k
            bd = pltpu.roll(b, d, axis=0) if d else b
            p = q * kd * jnp.exp2(jnp.minimum(b - bd, 0.0))
            for h in range(HG_HEADS):
                hs = slice(h * HG_DK, (h + 1) * HG_DK)
                a[h] = jnp.where(nears[d], jnp.sum(p[:, hs], axis=-1, keepdims=True), a[h])

        b_last = b[n - 1:n, :]
        q_in = (q * jnp.exp2(b)).astype(BF16)
        k_out = (k * jnp.exp2(b_last - b)).astype(BF16)
        carry = jnp.exp2(b_last)
        vb = v.astype(BF16)
        for h in range(HG_HEADS):
            hs = slice(h * HG_DK, (h + 1) * HG_DK)
            st = st_ref[r, h]
            o = (jnp.dot(a[h].astype(BF16), vb[:, hs], preferred_element_type=F32)
                 + lax.dot_general(q_in[:, hs], st.astype(BF16), _NT, preferred_element_type=F32))
            st_new = st * carry[:, hs] + lax.dot_general(vb[:, hs], k_out[:, hs], _TN,
                                                         preferred_element_type=F32)
            st_ref[r, h] = st_new
            o_ref[r, :, hs] = (_rmsnorm(o, gout_ref[...]) * gate[:, hs]).astype(BF16)

    @pl.when(c == pl.num_programs(1) - 1)
    def _():
        for r in range(group):
            for h in range(HG_HEADS):
                sfin_ref[r, h] = st_ref[r, h].T


def _hgrn_prompt(zhg, lb_logits, gout, batch, layer):
    n = zhg.shape[0]
    seq = n // batch
    nc = seq // HG_CHUNK
    group = HG_GROUP if batch % HG_GROUP == 0 else 1
    o, s_fin = pl.pallas_call(
        functools.partial(_hgrn_chunk_kernel, layer=layer),
        grid=(batch // group, nc),
        in_specs=[pl.BlockSpec((group, HG_CHUNK, HG_COLS), lambda g, c: (g, c, 0)),
                  pl.BlockSpec(lb_logits.shape, lambda g, c: (0, 0)),
                  pl.BlockSpec((1, HG_DV), lambda g, c: (0, 0))],
        out_specs=[pl.BlockSpec((group, HG_CHUNK, HG_WIDTH), lambda g, c: (g, c, 0)),
                   pl.BlockSpec((group, HG_HEADS, HG_DK, HG_DV), lambda g, c: (g, 0, 0, 0))],
        out_shape=[jax.ShapeDtypeStruct((batch, seq, HG_WIDTH), BF16),
                   jax.ShapeDtypeStruct((batch, HG_HEADS, HG_DK, HG_DV), F32)],
        scratch_shapes=[pltpu.VMEM((group, HG_HEADS, HG_DV, HG_DK), F32)],
        compiler_params=pltpu.CompilerParams(dimension_semantics=("arbitrary", "arbitrary")),
        name="hgrn_prompt",
    )(zhg.reshape(batch, seq, HG_COLS), lb_logits, gout)
    return o.reshape(n, HG_WIDTH), s_fin


def _hgrn_step_kernel(z_ref, s_ref, lbl_ref, gout_ref, o_ref, snew_ref, *, layer):
    lb = _lower_bound(lbl_ref[...], layer)
    q, k, f, v, gate = _hgrn_gates(z_ref[0], lb)
    outs = []
    for h in range(HG_HEADS):
        hs = slice(h * HG_DK, (h + 1) * HG_DK)
        s_new = _col_bcast(f[:, hs]) * s_ref[0, h] + _col_bcast(k[:, hs]) * v[:, hs]
        snew_ref[0, h] = s_new
        o = jnp.sum(_col_bcast(q[:, hs]) * s_new, axis=0, keepdims=True)
        outs.append(_rmsnorm(o, gout_ref[...]) * gate[:, hs])
    o_ref[0] = jnp.concatenate(outs, axis=-1).astype(BF16)


def _hgrn_sample(zhg, state, lb_logits, gout, layer):
    nb = zhg.shape[0]
    return pl.pallas_call(
        functools.partial(_hgrn_step_kernel, layer=layer),
        grid=(nb,),
        in_specs=[pl.BlockSpec((1, 1, HG_COLS), lambda b: (b, 0, 0)),
                  pl.BlockSpec((1, HG_HEADS, HG_DK, HG_DV), lambda b: (b, 0, 0, 0)),
                  pl.BlockSpec(lb_logits.shape, lambda b: (0, 0)),
                  pl.BlockSpec((1, HG_DV), lambda b: (0, 0))],
        out_specs=[pl.BlockSpec((1, 1, HG_WIDTH), lambda b: (b, 0, 0)),
                   pl.BlockSpec((1, HG_HEADS, HG_DK, HG_DV), lambda b: (b, 0, 0, 0))],
        out_shape=[jax.ShapeDtypeStruct((nb, 1, HG_WIDTH), BF16),
                   jax.ShapeDtypeStruct(state.shape, F32)],
        compiler_params=pltpu.CompilerParams(dimension_semantics=("arbitrary",)),
        name="hgrn_sample",
    )(zhg.reshape(nb, 1, HG_COLS), state, lb_logits, gout)


ATT_TILE = 512


def _head_slope(h):
    return jnp.exp2(jnp.full((1, 1), -8.0 / DA_HEADS, F32) * (h + 1).astype(F32))


def _diff_combine(acc1, l1, acc2, l2, lam, gsub, lam_init):
    out = acc1 / l1 - lam * (acc2 / l2)
    return _rmsnorm(out, gsub) * (1.0 - lam_init)


ATT_VROWS = DA_DV + 16


def _attn_prompt_kernel(q_ref, k_ref, v_ref, lq1_ref, lk1_ref, lq2_ref, lk2_ref, gsub_ref, o_ref,
                        kaug_ref, vt_ref, m_ref, acc_ref, p_ref, alpha_ref, *, lam_init):
    h = pl.program_id(1)
    qi = pl.program_id(2)
    t = q_ref.shape[0]
    nt = kaug_ref.shape[1]
    slope = _head_slope(h)
    lane = lax.broadcasted_iota(jnp.int32, (t, 2 * DA_DH), 1)
    loc = lax.broadcasted_iota(jnp.int32, (t, 2 * DA_DH), 0)
    loc_lo = (loc % 256).astype(F32)
    loc_hi = (loc - loc % 256).astype(F32)
    slot = [(1 - i) * DA_DH for i in range(2)]

    def augment(x, i, extras):
        out = jnp.zeros_like(x)
        for n, e in enumerate(extras):
            out = jnp.where(lane == slot[i] + n, e, out)
        return jnp.where((lane // DA_DH) == i, x, out)

    @pl.when(qi == 0)
    def _():
        for j in range(nt):
            k = k_ref[j * t:(j + 1) * t, :].astype(F32)
            for i in range(2):
                kaug_ref[i, j] = augment(k, i, (slope * loc_lo, slope * loc_hi, 1.0, 1.0)
                                         ).astype(BF16)
            vj = v_ref[pl.ds(j * t * DA_HEADS + h, t, stride=DA_HEADS), :]
            vt_ref[j, :DA_DV, :] = vj.T.astype(BF16)
            vt_ref[j, DA_DV:, :] = jnp.ones((ATT_VROWS - DA_DV, t), BF16)

    q = q_ref[...].astype(F32)
    qts = [augment(q, i, (1.0, 1.0, -slope * loc_lo, -slope * loc_hi)).T.astype(BF16)
           for i in range(2)]
    m_ref[...] = jnp.full_like(m_ref, NEG_INF)
    acc_ref[...] = jnp.zeros_like(acc_ref)
    causal = (lax.broadcasted_iota(jnp.int32, (t, t), 0)
              <= lax.broadcasted_iota(jnp.int32, (t, t), 1))

    def values(ki, slot):
        vt = vt_ref[ki]
        for i in range(2):
            acc_ref[i] = (alpha_ref[slot, i] * acc_ref[i]
                          + jnp.dot(vt, p_ref[slot, i], preferred_element_type=F32))

    def step(ki, slot, keep=None, first=False):
        ss = [jnp.dot(kaug_ref[i, ki], qts[i], preferred_element_type=F32)
              for i in range(2)]
        if not first:
            values(ki - 1, 1 - slot)
        shift = -slope * ((qi - ki) * t).astype(F32)
        for i in range(2):
            s = ss[i] if keep is None else jnp.where(keep, ss[i], NEG_INF)
            m_old = m_ref[i]
            m_new = jnp.maximum(m_old, jnp.max(s, axis=0, keepdims=True) + shift)
            alpha_ref[slot, i] = jnp.exp(m_old - m_new)
            p_ref[slot, i] = jnp.exp(s - (m_new - shift)).astype(BF16)
            m_ref[i] = m_new

    step(0, 0, keep=jnp.logical_or(causal, qi > 0), first=True)
    inner = qi - 1

    def pair(j, carry):
        step(1 + 2 * j, 1)
        step(2 + 2 * j, 0)
        return carry

    lax.fori_loop(0, inner // 2, pair, 0)
    for parity in range(2):
        @pl.when(jnp.logical_and(qi > 0, qi % 2 == parity))
        def _():
            if parity == 0:
                step(qi - 1, 1)
            step(qi, parity, keep=causal)

        @pl.when(qi % 2 == parity)
        def _():
            values(qi, parity)

    lam = _lambda(lq1_ref, lk1_ref, lq2_ref, lk2_ref, lam_init)
    a1, a2 = acc_ref[0], acc_ref[1]
    out = a1[:DA_DV] / a1[DA_DV:DA_DV + 1] - lam * (a2[:DA_DV] / a2[DA_DV:DA_DV + 1])
    ms = jnp.mean(out * out, axis=0, keepdims=True)
    out = out * lax.rsqrt(ms + NORM_EPS) * _col_bcast(gsub_ref[...], t) * (1.0 - lam_init)
    o_ref[...] = out.T.astype(BF16)


def _attn_prompt(qb, kb, v, lams, gsub, batch, lam_init):
    n = qb.shape[0]
    seq = n // batch
    nq = seq // ATT_TILE
    small = lambda a: pl.BlockSpec(a.shape, lambda b, h, i: (0, 0))
    return pl.pallas_call(
        functools.partial(_attn_prompt_kernel, lam_init=lam_init),
        grid=(batch, DA_HEADS, nq),
        in_specs=[pl.BlockSpec((ATT_TILE, 2 * DA_DH), lambda b, h, i: (b * nq + i, h)),
                  pl.BlockSpec((seq, 2 * DA_DH), lambda b, h, i: (b, h)),
                  pl.BlockSpec((seq * DA_HEADS, DA_DV), lambda b, h, i: (b, 0))]
                 + [small(a) for a in lams] + [small(gsub)],
        out_specs=pl.BlockSpec((ATT_TILE, DA_DV), lambda b, h, i: (b * nq + i, h)),
        out_shape=jax.ShapeDtypeStruct((n, DA_HEADS * DA_DV), BF16),
        scratch_shapes=[pltpu.VMEM((2, nq, ATT_TILE, 2 * DA_DH), BF16),
                        pltpu.VMEM((nq, ATT_VROWS, ATT_TILE), BF16),
                        pltpu.VMEM((2, 1, ATT_TILE), F32),
                        pltpu.VMEM((2, ATT_VROWS, ATT_TILE), F32),
                        pltpu.VMEM((2, 2, ATT_TILE, ATT_TILE), BF16),
                        pltpu.VMEM((2, 2, 1, ATT_TILE), F32)],
        compiler_params=pltpu.CompilerParams(
            dimension_semantics=("arbitrary", "arbitrary", "arbitrary")),
        name="attn_prompt",
    )(qb, kb, v, *lams, gsub)


PROMPT_ROWS = 256


def kernel(x_prompt, x_sample, cache_k, cache_v, state_hgrn, page_table, ffn1_norm, ffn1_w_gate, ffn1_w_up, ffn1_w_down, mix_norm, w_in, hg_lb_logits, hg_out_norm, da_q_norm, da_k_norm, da_lambda_q1, da_lambda_k1, da_lambda_q2, da_lambda_k2, da_subln, w_out, ffn2_norm, ffn2_w_gate, ffn2_w_up, ffn2_w_down):
    batch, seq, d = x_prompt.shape
    nb = x_sample.shape[0]
    depth = ffn1_norm.shape[0]
    nmap = 2 * DA_HEADS
    yp = x_prompt.reshape(batch * seq, d)
    ys = x_sample.reshape(nb, d)
    kt_cache = jnp.transpose(cache_k, (0, 1, 3, 4, 2))
    v_cache = cache_v.reshape(cache_v.shape[:2] + (-1, DA_DV))
    outs = [[] for _ in range(6)]
    for l in range(depth):
        lam_init = 0.8 - 0.6 * math.exp(-0.3 * l)
        bf = lambda w: w[l].astype(BF16)
        w1 = (bf(ffn1_w_gate), bf(ffn1_w_up), bf(ffn1_w_down))
        w2 = (bf(ffn2_w_gate), bf(ffn2_w_up), bf(ffn2_w_down))
        win, wo = bf(w_in), bf(w_out)
        gq = jnp.tile(da_q_norm[l:l + 1], (1, nmap))
        gk = jnp.tile(da_k_norm[l:l + 1], (1, nmap))
        lams = (da_lambda_q1[l:l + 1], da_lambda_k1[l:l + 1], da_lambda_q2[l:l + 1],
                da_lambda_k2[l:l + 1])
        gout, gsub = hg_out_norm[l:l + 1], da_subln[l:l + 1]

        def inproj(x, tm, rows_per_seq, pages=None):
            return _ffn_inproj(x, ffn1_norm[l:l + 1], *w1, mix_norm[l:l + 1], win, gq, gk, tm,
                               rows_per_seq, pages)

        def outproj(h, ohg, oda, tm, pages=None):
            return _outproj_ffn(h, ohg, oda, wo, ffn2_norm[l:l + 1], *w2, tm, pages)

        def keys(kt):
            return jnp.transpose(kt.reshape(kt.shape[0], nmap, DA_DH, kt.shape[2]), (0, 3, 1, 2))

        hs, zhg_s, qs, kt_s, _, vs = inproj(ys, nb, nb)
        ks = keys(kt_s).reshape(nb, DA_WIDTH)
        ohg_s, s_new = _hgrn_sample(zhg_s, state_hgrn[l], hg_lb_logits, gout, l)
        kt_pages, v_pages = kt_cache[l:l + 1], v_cache[l:l + 1]
        half = page_table.shape[1] // 2
        h, zhg, qb, kt, kb, v, m_s, l_s, acc_s = inproj(
            yp, PROMPT_ROWS, seq, (page_table, qs, ks, vs, kt_pages, v_pages, half))
        ohg, s_fin = _hgrn_prompt(zhg, hg_lb_logits, gout, batch, l)
        oda = _attn_prompt(qb, kb, v, lams, gsub, batch, lam_init)
        yp, oda_s = outproj(h, ohg, oda, PROMPT_ROWS, (page_table, qs, m_s, l_s, acc_s, lams, gsub,
                                                       kt_pages, v_pages, half, lam_init))
        outs[0].append(keys(kt))
        outs[1].append(v.reshape(batch, seq, DA_HEADS, DA_DV))
        outs[2].append(s_fin)
        ys = outproj(hs, ohg_s.reshape(nb, HG_WIDTH), oda_s, nb)
        outs[3].append(ks.reshape(nb, 1, nmap, DA_DH))
        outs[4].append(vs.reshape(nb, 1, DA_HEADS, DA_DV))
        outs[5].append(s_new)
    stacked = [jnp.stack(o, axis=0) for o in outs]
    return (yp.reshape(batch, seq, d), ys.reshape(nb, 1, d), *stacked)
```

```python
import functools
import math

import jax
import jax.numpy as jnp
from jax import lax
from jax.experimental import pallas as pl
from jax.experimental.pallas import tpu as pltpu

F32 = jnp.float32
BF16 = jnp.bfloat16

NORM_EPS = 1e-6
NEG_INF = -1e30
LANES = 128
SUBLANES = 8
HG_HEADS = 4
HG_DK = 128
HG_DV = 128
HG_WIDTH = HG_HEADS * HG_DK
DA_HEADS = 4
DA_DH = 64
DA_DV = 128
DA_WIDTH = 2 * DA_HEADS * DA_DH
DA_SCALE = DA_DH ** -0.5
HG_COLS = 4 * HG_WIDTH
VMEM_LIMIT = 56 * 1024 * 1024

_NT = (((1,), (1,)), ((), ()))
_TN = (((0,), (0,)), ((), ()))


def _rmsnorm(x, g):
    ms = jnp.mean(x * x, axis=-1, keepdims=True)
    return x * lax.rsqrt(ms + NORM_EPS) * g


def _sigmoid(x):
    return 0.5 * jnp.tanh(0.5 * x) + 0.5


def _silu(x):
    return x * _sigmoid(x)


def _split3(x):
    hi = x.astype(BF16)
    r1 = x - hi.astype(F32)
    mid = r1.astype(BF16)
    lo = (r1 - mid.astype(F32)).astype(BF16)
    return hi, mid, lo


def _dot_exact_rhs(sel, x):
    hi, mid, lo = _split3(x)
    return (jnp.dot(sel, hi, preferred_element_type=F32)
            + jnp.dot(sel, mid, preferred_element_type=F32)
            + jnp.dot(sel, lo, preferred_element_type=F32))


def _dot_exact_lhs(x, sel):
    hi, mid, lo = _split3(x)
    return (jnp.dot(hi, sel, preferred_element_type=F32)
            + jnp.dot(mid, sel, preferred_element_type=F32)
            + jnp.dot(lo, sel, preferred_element_type=F32))


def _col_bcast(row, width=LANES):
    n = row.shape[1]
    k = 2 * SUBLANES
    first = lax.broadcasted_iota(jnp.int32, (k, n), 0) == 0
    rows = jnp.where(first, jnp.broadcast_to(row, (k, n)), 0.0)
    ones = jnp.ones((k, width), BF16)
    hi, mid, lo = _split3(rows)
    return (lax.dot_general(hi, ones, _TN, preferred_element_type=F32)
            + lax.dot_general(mid, ones, _TN, preferred_element_type=F32)
            + lax.dot_general(lo, ones, _TN, preferred_element_type=F32))


def _group64_rmsnorm(x, g):
    m, w = x.shape
    r = lax.broadcasted_iota(jnp.int32, (LANES, LANES), 0) // DA_DH
    c = lax.broadcasted_iota(jnp.int32, (LANES, LANES), 1) // DA_DH
    same = jnp.where(r == c, 1.0, 0.0).astype(BF16)
    x2 = x * x
    parts = [_dot_exact_lhs(x2[:, i * LANES:(i + 1) * LANES], same) for i in range(w // LANES)]
    ss = jnp.concatenate(parts, axis=-1)
    return x * lax.rsqrt(ss * (1.0 / DA_DH) + NORM_EPS) * g


def _lower_bound(logits, layer):
    m = jnp.max(logits, axis=0, keepdims=True)
    e = jnp.exp(logits - m)
    return jnp.sum(e[:layer + 1], axis=0, keepdims=True) / jnp.sum(e, axis=0, keepdims=True)


def _lambda(lq1_ref, lk1_ref, lq2_ref, lk2_ref, lam_init):
    a = jnp.sum(lq1_ref[...] * lk1_ref[...], axis=-1, keepdims=True)
    b = jnp.sum(lq2_ref[...] * lk2_ref[...], axis=-1, keepdims=True)
    return jnp.exp(a) - jnp.exp(b) + lam_init


FFN_PART = 1024


def _ffn_parts(dff):
    return [(lo, min(lo + FFN_PART, dff)) for lo in range(0, dff, FFN_PART)]


class _NoSide:
    def pre(self, j):
        pass

    def post(self, j):
        pass


def _swiglu_parts(xn, wg_ref, wu_ref, wd_ref, side):
    y = None
    for j, (lo, hi) in enumerate(_ffn_parts(wg_ref.shape[1])):
        side.pre(j)
        g = jnp.dot(xn, wg_ref[:, lo:hi], preferred_element_type=F32)
        u = jnp.dot(xn, wu_ref[:, lo:hi], preferred_element_type=F32)
        a = (_silu(g) * u).astype(BF16)
        part = jnp.dot(a, wd_ref[lo:hi, :], preferred_element_type=F32)
        y = part if y is None else y + part
        side.post(j)
    return y


def _ffn_inproj_body(x_ref, g1_ref, wg_ref, wu_ref, wd_ref, gm_ref, win_ref, gq_ref, gk_ref,
                     h_ref, zhg_ref, q_ref, kt_ref, kb_ref, v_ref, side):
    tm = x_ref.shape[0]
    last = len(_ffn_parts(wg_ref.shape[1]))
    x = x_ref[...]
    y = _swiglu_parts(_rmsnorm(x, g1_ref[...]).astype(BF16), wg_ref, wu_ref, wd_ref, side)
    side.pre(last)
    h = x + 0.5 * y
    h_ref[...] = h
    z = jnp.dot(_rmsnorm(h, gm_ref[...]).astype(BF16), win_ref[...], preferred_element_type=F32)
    zhg_ref[...] = z[:, :HG_COLS]
    dq = z[:, HG_COLS:HG_COLS + DA_WIDTH]
    dk = z[:, HG_COLS + DA_WIDTH:HG_COLS + 2 * DA_WIDTH]
    dv = z[:, HG_COLS + 2 * DA_WIDTH:]
    q_ref[...] = (_group64_rmsnorm(dq, gq_ref[...]) * DA_SCALE).astype(BF16)
    k = _group64_rmsnorm(dk, gk_ref[...])
    kt_ref[0] = k.T
    kb_ref[...] = k.astype(BF16)
    for hd in range(DA_HEADS):
        v_ref[pl.ds(hd, tm, stride=DA_HEADS), :] = dv[:, hd * DA_DV:(hd + 1) * DA_DV]
    side.post(last)


def _outproj_ffn_body(h_ref, ohg_ref, oda_ref, wo_ref, g2_ref, wg_ref, wu_ref, wd_ref, y_ref, side):
    last = len(_ffn_parts(wg_ref.shape[1]))
    h = (h_ref[...]
         + jnp.dot(ohg_ref[...], wo_ref[:HG_WIDTH, :], preferred_element_type=F32)
         + jnp.dot(oda_ref[...], wo_ref[HG_WIDTH:, :], preferred_element_type=F32))
    y = _swiglu_parts(_rmsnorm(h, g2_ref[...]).astype(BF16), wg_ref, wu_ref, wd_ref, side)
    side.pre(last)
    y_ref[...] = h + 0.5 * y
    side.post(last)


def _ffn_inproj_kernel(*refs):
    _ffn_inproj_body(*refs, _NoSide())


def _outproj_ffn_kernel(*refs):
    _outproj_ffn_body(*refs, _NoSide())


PAGE_GROUP = 8


class _PageWalk:
    def __init__(self, s, n_steps, pt_ref, kt_hbm, v_hbm, kbuf, vbuf, sem, first_page, n_pages):
        nb = pt_ref.shape[0]
        assert n_steps % nb == 0
        self.steps_per_seq = n_steps // nb
        assert n_pages % (PAGE_GROUP * self.steps_per_seq) == 0
        self.groups = n_pages // (PAGE_GROUP * self.steps_per_seq)
        assert self.groups % 2 == 0
        self.s, self.n_steps, self.first_page = s, n_steps, first_page
        self.pt_ref, self.kt_hbm, self.v_hbm = pt_ref, kt_hbm, v_hbm
        self.kbuf, self.vbuf, self.sem = kbuf, vbuf, sem
        self.seq_first = (s % self.steps_per_seq) == 0
        self.seq_last = (s % self.steps_per_seq) == self.steps_per_seq - 1

    def first_page_of(self, step, g):
        return self.first_page + ((step % self.steps_per_seq) * self.groups + g) * PAGE_GROUP

    def _copies(self, step, g):
        seq = step // self.steps_per_seq
        page0 = self.first_page_of(step, g)
        slot = g % 2
        copies = []
        for i in range(PAGE_GROUP):
            pg = self.pt_ref[seq, page0 + i]
            copies.append(pltpu.make_async_copy(self.kt_hbm.at[0, pg], self.kbuf.at[slot, i],
                                                self.sem.at[slot, 0]))
            copies.append(pltpu.make_async_copy(self.v_hbm.at[0, pg], self.vbuf.at[slot, i],
                                                self.sem.at[slot, 1]))
        return copies

    def prime(self):
        for g in range(2):
            for c in self._copies(0, g):
                c.start()

    def wait(self, g):
        for c in self._copies(self.s, g):
            c.wait()

    def start_ahead(self, g):
        step = self.s + (g + 2) // self.groups

        @pl.when(step < self.n_steps)
        def _():
            for c in self._copies(step, (g + 2) % self.groups):
                c.start()


def _fold_pages(kbuf, vbuf, slot, first_pos, past_len, qcol_ref, expand_ref, m_ref, l_ref, acc_ref):
    nmap = 2 * DA_HEADS
    page = kbuf.shape[-1]
    vrows = vbuf.shape[-2]
    mrow = lax.broadcasted_iota(jnp.int32, (nmap, 1), 0)
    slope = jnp.exp2((-8.0 / DA_HEADS) * ((mrow // 2) + 1).astype(F32))
    own = (lax.broadcasted_iota(jnp.int32, (1, vrows), 1) % DA_HEADS) == (mrow // 2)
    lane = lax.broadcasted_iota(jnp.int32, (1, page), 1)
    qc = qcol_ref[...].reshape(nmap, DA_DH, page)
    scores = []
    for i in range(PAGE_GROUP):
        s = jnp.sum(kbuf[slot, i] * qc, axis=1)
        pos = first_pos + i * page + lane
        scores.append(s - slope * (past_len - pos).astype(F32))
    m_old = m_ref[...]
    m_new = m_old
    for s in scores:
        m_new = jnp.maximum(m_new, jnp.max(s, axis=-1, keepdims=True))
    alpha = jnp.exp(m_old - m_new)
    l_new = alpha * l_ref[...]
    acc = alpha * acc_ref[...]
    probs = []
    for s in scores:
        p = jnp.exp(s - m_new)
        l_new = l_new + jnp.sum(p, axis=-1, keepdims=True)
        probs.append(p.astype(BF16))
    spread = jnp.dot(jnp.concatenate(probs, axis=0), expand_ref[...],
                     preferred_element_type=F32)
    for i in range(PAGE_GROUP):
        w = jnp.where(own, spread[i * nmap:(i + 1) * nmap], 0.0).astype(BF16)
        acc = acc + jnp.dot(w, vbuf[slot, i].astype(BF16), preferred_element_type=F32)
    m_ref[...] = m_new
    l_ref[...] = l_new
    acc_ref[...] = acc


class _PageSide:
    def __init__(self, walk, n_stages, past_len, qcol_ref, expand_ref, m_ref, l_ref, acc_ref,
                 begin, finish):
        self.walk, self.n_stages, self.past_len = walk, n_stages, past_len
        self.state = (qcol_ref, expand_ref, m_ref, l_ref, acc_ref)
        self.begin, self.finish = begin, finish

    def _groups(self, j):
        return range(j, self.walk.groups, self.n_stages)

    def pre(self, j):
        walk = self.walk
        if j == 0:
            page, vrows = walk.kbuf.shape[-1], walk.vbuf.shape[-2]
            expand_ref = self.state[1]

            @pl.when(walk.s == 0)
            def _():
                walk.prime()
                tok = lax.broadcasted_iota(jnp.int32, (page, vrows), 0)
                vrow = lax.broadcasted_iota(jnp.int32, (page, vrows), 1)
                expand_ref[...] = jnp.where(vrow // DA_HEADS == tok, 1.0, 0.0).astype(BF16)

            pl.when(walk.seq_first)(self.begin)
        for g in self._groups(j):
            walk.wait(g)

    def post(self, j):
        walk = self.walk
        page = walk.kbuf.shape[-1]
        for g in self._groups(j):
            _fold_pages(walk.kbuf, walk.vbuf, g % 2, walk.first_page_of(walk.s, g) * page,
                        self.past_len, *self.state)
            walk.start_ahead(g)
        if j == self.n_stages - 1:
            pl.when(walk.seq_last)(self.finish)


def _ffn_inproj_pages_kernel(pt_ref, x_ref, g1_ref, wg_ref, wu_ref, wd_ref, gm_ref, win_ref, gq_ref,
                             gk_ref, qs_ref, kn_ref, vn_ref, kt_hbm, v_hbm,
                             h_ref, zhg_ref, q_ref, kt_ref, kb_ref, v_ref, mo_ref, lo_ref, ao_ref,
                             kbuf, vbuf, sem, qcol_ref, expand_ref, m_ref, l_ref, acc_ref,
                             *, n_steps, n_pages):
    nmap = 2 * DA_HEADS
    page = kbuf.shape[-1]
    walk = _PageWalk(pl.program_id(0), n_steps, pt_ref, kt_hbm, v_hbm, kbuf, vbuf, sem,
                     0, n_pages)

    def begin():
        qcol = _col_bcast(qs_ref[0].astype(F32), page)
        qcol_ref[...] = qcol
        kcol = _col_bcast(kn_ref[0], page)
        m_ref[...] = jnp.sum((qcol * kcol).reshape(nmap, DA_DH, page), axis=1)
        l_ref[...] = jnp.ones_like(l_ref)
        vn = vn_ref[0]
        acc_ref[...] = jnp.concatenate(
            [vn[:, (r // 2) * DA_DV:(r // 2 + 1) * DA_DV] for r in range(nmap)], axis=0)

    def finish():
        mo_ref[0] = m_ref[...]
        lo_ref[0] = l_ref[...]
        ao_ref[0] = acc_ref[...]

    n_stages = len(_ffn_parts(wg_ref.shape[1])) + 1
    side = _PageSide(walk, n_stages, pt_ref.shape[1] * page, qcol_ref, expand_ref, m_ref, l_ref,
                     acc_ref, begin, finish)
    _ffn_inproj_body(x_ref, g1_ref, wg_ref, wu_ref, wd_ref, gm_ref, win_ref, gq_ref, gk_ref,
                     h_ref, zhg_ref, q_ref, kt_ref, kb_ref, v_ref, side)


def _outproj_ffn_pages_kernel(pt_ref, h_ref, ohg_ref, oda_ref, wo_ref, g2_ref, wg_ref, wu_ref, wd_ref,
                              qs_ref, mi_ref, li_ref, ai_ref, lq1_ref, lk1_ref, lq2_ref, lk2_ref,
                              gsub_ref, kt_hbm, v_hbm, y_ref, os_ref,
                              kbuf, vbuf, sem, qcol_ref, expand_ref, m_ref, l_ref, acc_ref,
                              *, n_steps, first_page, lam_init):
    page = kbuf.shape[-1]
    n_pages = pt_ref.shape[1]
    walk = _PageWalk(pl.program_id(0), n_steps, pt_ref, kt_hbm, v_hbm, kbuf, vbuf, sem,
                     first_page, n_pages - first_page)

    def begin():
        qcol_ref[...] = _col_bcast(qs_ref[0].astype(F32), page)
        m_ref[...] = mi_ref[0]
        l_ref[...] = li_ref[0]
        acc_ref[...] = ai_ref[0]

    def finish():
        lam = _lambda(lq1_ref, lk1_ref, lq2_ref, lk2_ref, lam_init)
        acc, l_fin = acc_ref[...], l_ref[...]
        outs = []
        for hd in range(DA_HEADS):
            r1, r2 = 2 * hd, 2 * hd + 1
            outs.append(_diff_combine(acc[r1:r1 + 1], l_fin[r1:r1 + 1], acc[r2:r2 + 1],
                                      l_fin[r2:r2 + 1], lam, gsub_ref[...], lam_init))
        os_ref[0] = jnp.concatenate(outs, axis=-1).astype(BF16)

    n_stages = len(_ffn_parts(wg_ref.shape[1])) + 1
    side = _PageSide(walk, n_stages, n_pages * page, qcol_ref, expand_ref, m_ref, l_ref, acc_ref,
                     begin, finish)
    _outproj_ffn_body(h_ref, ohg_ref, oda_ref, wo_ref, g2_ref, wg_ref, wu_ref, wd_ref, y_ref, side)


def _resident(shape):
    return pl.BlockSpec(shape, lambda *_: (0,) * len(shape), pipeline_mode=pl.Buffered(1))


def _page_scratch(page, vrows):
    nmap = 2 * DA_HEADS
    return [pltpu.VMEM((2, PAGE_GROUP, nmap, DA_DH, page), F32),
            pltpu.VMEM((2, PAGE_GROUP, vrows, DA_DV), F32),
            pltpu.SemaphoreType.DMA((2, 2)),
            pltpu.VMEM((DA_WIDTH, page), F32), pltpu.VMEM((page, vrows), BF16),
            pltpu.VMEM((nmap, page), F32), pltpu.VMEM((nmap, page), F32),
            pltpu.VMEM((nmap, DA_DV), F32)]


def _ffn_inproj(x, g1, wg, wu, wd, gm, win, gq, gk, tm, seq, pages=None):
    n, d = x.shape
    dff = wg.shape[1]
    tiles = seq // tm
    rows = lambda w: pl.BlockSpec((tm, w), lambda i, *_: (i, 0))
    in_specs = [rows(d), _resident((1, d)), _resident((d, dff)), _resident((d, dff)),
                _resident((dff, d)), _resident((1, d)), _resident(win.shape),
                _resident((1, DA_WIDTH)), _resident((1, DA_WIDTH))]
    out_specs = [rows(d), rows(HG_COLS), rows(DA_WIDTH),
                 pl.BlockSpec((1, DA_WIDTH, tm), lambda i, *_: (i // tiles, 0, i % tiles)),
                 rows(DA_WIDTH),
                 pl.BlockSpec((tm * DA_HEADS, DA_DV), lambda i, *_: (i, 0))]
    out_shape = [jax.ShapeDtypeStruct((n, d), F32), jax.ShapeDtypeStruct((n, HG_COLS), F32),
                 jax.ShapeDtypeStruct((n, DA_WIDTH), BF16),
                 jax.ShapeDtypeStruct((n // seq, DA_WIDTH, seq), F32),
                 jax.ShapeDtypeStruct((n, DA_WIDTH), BF16),
                 jax.ShapeDtypeStruct((n * DA_HEADS, DA_DV), F32)]
    params = pltpu.CompilerParams(dimension_semantics=("arbitrary",), vmem_limit_bytes=VMEM_LIMIT)
    args = (x, g1, wg, wu, wd, gm, win, gq, gk)
    if pages is None:
        return pl.pallas_call(_ffn_inproj_kernel, grid=(n // tm,), in_specs=in_specs,
                              out_specs=out_specs, out_shape=out_shape, compiler_params=params,
                              name="ffn_inproj")(*args)
    page_table, qs, kn, vn, kt_pages, v_pages, n_pages = pages
    nb = page_table.shape[0]
    nmap = 2 * DA_HEADS
    page, vrows = kt_pages.shape[-1], v_pages.shape[-2]
    per_seq = (n // tm) // nb
    seq_row = lambda w: pl.BlockSpec((1, 1, w), lambda i, *_: (i // per_seq, 0, 0))
    state = lambda w: pl.BlockSpec((1, nmap, w), lambda i, *_: (i // per_seq, 0, 0))
    grid_spec = pltpu.PrefetchScalarGridSpec(
        num_scalar_prefetch=1, grid=(n // tm,),
        in_specs=in_specs + [seq_row(DA_WIDTH), seq_row(DA_WIDTH), seq_row(DA_HEADS * DA_DV),
                             pl.BlockSpec(memory_space=pl.ANY), pl.BlockSpec(memory_space=pl.ANY)],
        out_specs=out_specs + [state(page), state(page), state(DA_DV)],
        scratch_shapes=_page_scratch(page, vrows))
    out_shape = out_shape + [jax.ShapeDtypeStruct((nb, nmap, page), F32),
                             jax.ShapeDtypeStruct((nb, nmap, page), F32),
                             jax.ShapeDtypeStruct((nb, nmap, DA_DV), F32)]
    return pl.pallas_call(
        functools.partial(_ffn_inproj_pages_kernel, n_steps=n // tm, n_pages=n_pages),
        grid_spec=grid_spec, out_shape=out_shape, compiler_params=params, name="ffn_inproj_pages",
    )(page_table, *args, qs.reshape(nb, 1, DA_WIDTH), kn.reshape(nb, 1, DA_WIDTH),
      vn.reshape(nb, 1, DA_HEADS * DA_DV), kt_pages, v_pages)


def _outproj_ffn(h, ohg, oda, wo, g2, wg, wu, wd, tm, pages=None):
    n, d = h.shape
    dff = wg.shape[1]
    rows = lambda w: pl.BlockSpec((tm, w), lambda i, *_: (i, 0))
    in_specs = [rows(d), rows(HG_WIDTH), rows(DA_WIDTH), _resident(wo.shape), _resident((1, d)),
                _resident((d, dff)), _resident((d, dff)), _resident((dff, d))]
    params = pltpu.CompilerParams(dimension_semantics=("arbitrary",), vmem_limit_bytes=VMEM_LIMIT)
    args = (h, ohg, oda, wo, g2, wg, wu, wd)
    if pages is None:
        return pl.pallas_call(_outproj_ffn_kernel, grid=(n // tm,), in_specs=in_specs,
                              out_specs=rows(d), out_shape=jax.ShapeDtypeStruct((n, d), F32),
                              compiler_params=params, name="outproj_ffn")(*args)
    page_table, qs, m, l, acc, lams, gsub, kt_pages, v_pages, first_page, lam_init = pages
    nb = page_table.shape[0]
    nmap = 2 * DA_HEADS
    page, vrows = kt_pages.shape[-1], v_pages.shape[-2]
    per_seq = (n // tm) // nb
    seq_row = lambda w: pl.BlockSpec((1, 1, w), lambda i, *_: (i // per_seq, 0, 0))
    state = lambda w: pl.BlockSpec((1, nmap, w), lambda i, *_: (i // per_seq, 0, 0))
    small = lambda a: pl.BlockSpec(a.shape, lambda i, *_: (0, 0))
    grid_spec = pltpu.PrefetchScalarGridSpec(
        num_scalar_prefetch=1, grid=(n // tm,),
        in_specs=in_specs + [seq_row(DA_WIDTH), state(page), state(page), state(DA_DV)]
                 + [small(a) for a in lams] + [small(gsub)]
                 + [pl.BlockSpec(memory_space=pl.ANY), pl.BlockSpec(memory_space=pl.ANY)],
        out_specs=[rows(d), seq_row(DA_HEADS * DA_DV)],
        scratch_shapes=_page_scratch(page, vrows))
    y, o_s = pl.pallas_call(
        functools.partial(_outproj_ffn_pages_kernel, n_steps=n // tm, first_page=first_page,
                          lam_init=lam_init),
        grid_spec=grid_spec,
        out_shape=[jax.ShapeDtypeStruct((n, d), F32),
                   jax.ShapeDtypeStruct((nb, 1, DA_HEADS * DA_DV), BF16)],
        compiler_params=params, name="outproj_ffn_pages",
    )(page_table, *args, qs.reshape(nb, 1, DA_WIDTH), m, l, acc, *lams, gsub, kt_pages, v_pages)
    return y, o_s.reshape(nb, DA_HEADS * DA_DV)


HG_CHUNK = 128
HG_BLOCK = 4
HG_GROUP = 4


def _hgrn_gates(z, lb):
    xq = z[:, :HG_WIDTH]
    xf = z[:, HG_WIDTH:2 * HG_WIDTH]
    xi = z[:, 2 * HG_WIDTH:3 * HG_WIDTH]
    xg = z[:, 3 * HG_WIDTH:]
    q = _silu(xq) * (HG_DK ** -0.5)
    f = lb + (1.0 - lb) * _sigmoid(xf)
    return q, 1.0 - f, f, xi, _silu(xg)


def _hgrn_chunk_kernel(z_ref, lbl_ref, gout_ref, o_ref, sfin_ref, st_ref, *, layer):
    c = pl.program_id(1)
    group, n = z_ref.shape[0], z_ref.shape[1]

    @pl.when(c == 0)
    def _():
        st_ref[...] = jnp.zeros_like(st_ref)

    lb = _lower_bound(lbl_ref[...], layer)
    row = lax.broadcasted_iota(jnp.int32, (n, n), 0)
    col = lax.broadcasted_iota(jnp.int32, (n, n), 1)
    tri = jnp.where(row >= col, 1.0, 0.0).astype(BF16)
    trow = lax.broadcasted_iota(jnp.int32, (n, 1), 0)
    spans = []
    w = HG_BLOCK
    while 2 * w <= n:
        span = 2 * w
        spans.append((w, span, (trow % span) >= w,
                      ((row // span) == (col // span)) & ((row % span) >= w) & ((col % span) < w)))
        w = span
    nears = [((row - col) == d) & ((row % HG_BLOCK) >= d) for d in range(HG_BLOCK)]

    for r in range(group):
        q, k, f, v, gate = _hgrn_gates(z_ref[r], lb)
        b = _dot_exact_rhs(tri, jnp.log2(f))
        a = [jnp.zeros((n, n), F32) for _ in range(HG_HEADS)]
        for w, span, right, lvl in spans:
            ref = jnp.concatenate(
                [jnp.broadcast_to(b[p * span + w - 1:p * span + w, :], (span, HG_WIDTH))
                 for p in range(n // span)], axis=0)
            e = jnp.exp2(-jnp.abs(b - ref))
            qw = jnp.where(right, q * e, 0.0).astype(BF16)
            kw = jnp.where(right, 0.0, k * e).astype(BF16)
            for h in range(HG_HEADS):
                hs = slice(h * HG_DK, (h + 1) * HG_DK)
                p = lax.dot_general(qw[:, hs], kw[:, hs], _NT, preferred_element_type=F32)
                a[h] = jnp.where(lvl, p, a[h])
        for d in range(HG_BLOCK):
            kd = pltpu.roll(k, d, axis=0) if d else k
            bd = pltpu.roll(b, d, axis=0) if d else b
            p = q * kd * jnp.exp2(jnp.minimum(b - bd, 0.0))
            for h in range(HG_HEADS):
                hs = slice(h * HG_DK, (h + 1) * HG_DK)
                a[h] = jnp.where(nears[d], jnp.sum(p[:, hs], axis=-1, keepdims=True), a[h])

        b_last = b[n - 1:n, :]
        q_in = (q * jnp.exp2(b)).astype(BF16)
        k_out = (k * jnp.exp2(b_last - b)).astype(BF16)
        carry = jnp.exp2(b_last)
        vb = v.astype(BF16)
        for h in range(HG_HEADS):
            hs = slice(h * HG_DK, (h + 1) * HG_DK)
            st = st_ref[r, h]
            o = (jnp.dot(a[h].astype(BF16), vb[:, hs], preferred_element_type=F32)
                 + lax.dot_general(q_in[:, hs], st.astype(BF16), _NT, preferred_element_type=F32))
            st_new = st * carry[:, hs] + lax.dot_general(vb[:, hs], k_out[:, hs], _TN,
                                                         preferred_element_type=F32)
            st_ref[r, h] = st_new
            o_ref[r, :, hs] = (_rmsnorm(o, gout_ref[...]) * gate[:, hs]).astype(BF16)

    @pl.when(c == pl.num_programs(1) - 1)
    def _():
        for r in range(group):
            for h in range(HG_HEADS):
                sfin_ref[r, h] = st_ref[r, h].T


def _hgrn_prompt(zhg, lb_logits, gout, batch, layer):
    n = zhg.shape[0]
    seq = n // batch
    nc = seq // HG_CHUNK
    group = HG_GROUP if batch % HG_GROUP == 0 else 1
    o, s_fin = pl.pallas_call(
        functools.partial(_hgrn_chunk_kernel, layer=layer),
        grid=(batch // group, nc),
        in_specs=[pl.BlockSpec((group, HG_CHUNK, HG_COLS), lambda g, c: (g, c, 0)),
                  pl.BlockSpec(lb_logits.shape, lambda g, c: (0, 0)),
                  pl.BlockSpec((1, HG_DV), lambda g, c: (0, 0))],
        out_specs=[pl.BlockSpec((group, HG_CHUNK, HG_WIDTH), lambda g, c: (g, c, 0)),
                   pl.BlockSpec((group, HG_HEADS, HG_DK, HG_DV), lambda g, c: (g, 0, 0, 0))],
        out_shape=[jax.ShapeDtypeStruct((batch, seq, HG_WIDTH), BF16),
                   jax.ShapeDtypeStruct((batch, HG_HEADS, HG_DK, HG_DV), F32)],
        scratch_shapes=[pltpu.VMEM((group, HG_HEADS, HG_DV, HG_DK), F32)],
        compiler_params=pltpu.CompilerParams(dimension_semantics=("arbitrary", "arbitrary")),
        name="hgrn_prompt",
    )(zhg.reshape(batch, seq, HG_COLS), lb_logits, gout)
    return o.reshape(n, HG_WIDTH), s_fin


HG_STEP_GROUP = 8


def _hgrn_step_kernel(z_ref, s_ref, lbl_ref, gout_ref, o_ref, snew_ref, *, layer):
    lb = _lower_bound(lbl_ref[...], layer)
    for r in range(z_ref.shape[0]):
        q, k, f, v, gate = _hgrn_gates(z_ref[r], lb)
        outs = []
        for h in range(HG_HEADS):
            hs = slice(h * HG_DK, (h + 1) * HG_DK)
            s_new = _col_bcast(f[:, hs]) * s_ref[r, h] + _col_bcast(k[:, hs]) * v[:, hs]
            snew_ref[r, h] = s_new
            o = jnp.sum(_col_bcast(q[:, hs]) * s_new, axis=0, keepdims=True)
            outs.append(_rmsnorm(o, gout_ref[...]) * gate[:, hs])
        o_ref[r] = jnp.concatenate(outs, axis=-1).astype(BF16)


def _hgrn_sample(zhg, state, lb_logits, gout, layer):
    nb = zhg.shape[0]
    grp = HG_STEP_GROUP if nb % HG_STEP_GROUP == 0 else 1
    return pl.pallas_call(
        functools.partial(_hgrn_step_kernel, layer=layer),
        grid=(nb // grp,),
        in_specs=[pl.BlockSpec((grp, 1, HG_COLS), lambda b: (b, 0, 0)),
                  pl.BlockSpec((grp, HG_HEADS, HG_DK, HG_DV), lambda b: (b, 0, 0, 0)),
                  pl.BlockSpec(lb_logits.shape, lambda b: (0, 0)),
                  pl.BlockSpec((1, HG_DV), lambda b: (0, 0))],
        out_specs=[pl.BlockSpec((grp, 1, HG_WIDTH), lambda b: (b, 0, 0)),
                   pl.BlockSpec((grp, HG_HEADS, HG_DK, HG_DV), lambda b: (b, 0, 0, 0))],
        out_shape=[jax.ShapeDtypeStruct((nb, 1, HG_WIDTH), BF16),
                   jax.ShapeDtypeStruct(state.shape, F32)],
        compiler_params=pltpu.CompilerParams(dimension_semantics=("arbitrary",)),
        name="hgrn_sample",
    )(zhg.reshape(nb, 1, HG_COLS), state, lb_logits, gout)


ATT_TILE = 512


def _head_slope(h):
    return jnp.exp2(jnp.full((1, 1), -8.0 / DA_HEADS, F32) * (h + 1).astype(F32))


def _diff_combine(acc1, l1, acc2, l2, lam, gsub, lam_init):
    out = acc1 / l1 - lam * (acc2 / l2)
    return _rmsnorm(out, gsub) * (1.0 - lam_init)


ATT_VROWS = DA_DV + 16


def _attn_prompt_kernel(q_ref, k_ref, v_ref, lq1_ref, lk1_ref, lq2_ref, lk2_ref, gsub_ref, o_ref,
                        kaug_ref, vt_ref, m_ref, acc_ref, p_ref, alpha_ref, gcol_ref, *, lam_init):
    qi = pl.program_id(2)
    t = q_ref.shape[0]
    nt = kaug_ref.shape[1]
    heads = vt_ref.shape[0]
    units = [(hh, i) for hh in range(heads) for i in range(2)]
    slopes = [_head_slope(pl.program_id(1) * heads + hh) for hh in range(heads)]
    lane = lax.broadcasted_iota(jnp.int32, (t, 2 * DA_DH), 1)
    loc = lax.broadcasted_iota(jnp.int32, (t, 2 * DA_DH), 0)
    loc_lo = (loc % 256).astype(F32)
    loc_hi = (loc - loc % 256).astype(F32)
    slot = [(1 - i) * DA_DH for i in range(2)]

    def augment(x, i, extras):
        out = jnp.zeros_like(x)
        for n, e in enumerate(extras):
            out = jnp.where(lane == slot[i] + n, e, out)
        return jnp.where((lane // DA_DH) == i, x, out)

    @pl.when(qi == 0)
    def _():
        for j in range(nt):
            for u, (hh, i) in enumerate(units):
                k = k_ref[j * t:(j + 1) * t, hh * 2 * DA_DH:(hh + 1) * 2 * DA_DH].astype(F32)
                kaug_ref[u, j] = augment(k, i, (slopes[hh] * loc_lo, slopes[hh] * loc_hi, 1.0, 1.0)
                                         ).astype(BF16)
            for hh in range(heads):
                head = pl.program_id(1) * heads + hh
                vj = v_ref[pl.ds(j * t * DA_HEADS + head, t, stride=DA_HEADS), :]
                vt_ref[hh, j, :DA_DV, :] = vj.T.astype(BF16)
                vt_ref[hh, j, DA_DV:, :] = jnp.ones((ATT_VROWS - DA_DV, t), BF16)
        gcol_ref[...] = _col_bcast(gsub_ref[...], t)

    qts = []
    for hh, i in units:
        q = q_ref[:, hh * 2 * DA_DH:(hh + 1) * 2 * DA_DH].astype(F32)
        qts.append(augment(q, i, (1.0, 1.0, -slopes[hh] * loc_lo, -slopes[hh] * loc_hi)
                           ).T.astype(BF16))
    m_ref[...] = jnp.full_like(m_ref, NEG_INF)
    acc_ref[...] = jnp.zeros_like(acc_ref)
    causal = (lax.broadcasted_iota(jnp.int32, (t, t), 0)
              <= lax.broadcasted_iota(jnp.int32, (t, t), 1))

    def values(ki, slot):
        for u, (hh, _) in enumerate(units):
            acc_ref[u] = (alpha_ref[slot, u] * acc_ref[u]
                          + jnp.dot(vt_ref[hh, ki], p_ref[slot, u], preferred_element_type=F32))

    def step(ki, slot, keep=None, first=False):
        ss = [jnp.dot(kaug_ref[u, ki], qts[u], preferred_element_type=F32)
              for u in range(len(units))]
        if not first:
            values(ki - 1, 1 - slot)
        for u, (hh, _) in enumerate(units):
            shift = -slopes[hh] * ((qi - ki) * t).astype(F32)
            s = ss[u] if keep is None else jnp.where(keep, ss[u], NEG_INF)
            m_old = m_ref[u]
            m_new = jnp.maximum(m_old, jnp.max(s, axis=0, keepdims=True) + shift)
            alpha_ref[slot, u] = jnp.exp(m_old - m_new)
            p_ref[slot, u] = jnp.exp(s - (m_new - shift)).astype(BF16)
            m_ref[u] = m_new

    pl.when(qi == 0)(lambda: step(0, 0, keep=causal, first=True))
    pl.when(qi > 0)(lambda: step(0, 0, first=True))
    inner = qi - 1

    def pair(j, carry):
        step(1 + 2 * j, 1)
        step(2 + 2 * j, 0)
        return carry

    lax.fori_loop(0, inner // 2, pair, 0)
    for parity in range(2):
        @pl.when(jnp.logical_and(qi > 0, qi % 2 == parity))
        def _():
            if parity == 0:
                step(qi - 1, 1)
            step(qi, parity, keep=causal)

        @pl.when(qi % 2 == parity)
        def _():
            values(qi, parity)

    lam = _lambda(lq1_ref, lk1_ref, lq2_ref, lk2_ref, lam_init)
    for hh in range(heads):
        a1, a2 = acc_ref[2 * hh], acc_ref[2 * hh + 1]
        inv1 = 1.0 / a1[DA_DV:DA_DV + 1]
        inv2 = lam / a2[DA_DV:DA_DV + 1]
        out = a1[:DA_DV] * inv1 - a2[:DA_DV] * inv2
        ms = jnp.mean(out * out, axis=0, keepdims=True)
        out = out * (lax.rsqrt(ms + NORM_EPS) * (1.0 - lam_init)) * gcol_ref[...]
        o_ref[:, hh * DA_DV:(hh + 1) * DA_DV] = out.T.astype(BF16)


ATT_HEADS = 1


def _attn_prompt(qb, kb, v, lams, gsub, batch, lam_init):
    n = qb.shape[0]
    seq = n // batch
    nq = seq // ATT_TILE
    units = 2 * ATT_HEADS
    small = lambda a: pl.BlockSpec(a.shape, lambda b, h, i: (0, 0))
    return pl.pallas_call(
        functools.partial(_attn_prompt_kernel, lam_init=lam_init),
        grid=(batch, DA_HEADS // ATT_HEADS, nq),
        in_specs=[pl.BlockSpec((ATT_TILE, ATT_HEADS * 2 * DA_DH), lambda b, h, i: (b * nq + i, h)),
                  pl.BlockSpec((seq, ATT_HEADS * 2 * DA_DH), lambda b, h, i: (b, h)),
                  pl.BlockSpec((seq * DA_HEADS, DA_DV), lambda b, h, i: (b, 0))]
                 + [small(a) for a in lams] + [small(gsub)],
        out_specs=pl.BlockSpec((ATT_TILE, ATT_HEADS * DA_DV), lambda b, h, i: (b * nq + i, h)),
        out_shape=jax.ShapeDtypeStruct((n, DA_HEADS * DA_DV), BF16),
        scratch_shapes=[pltpu.VMEM((units, nq, ATT_TILE, 2 * DA_DH), BF16),
                        pltpu.VMEM((ATT_HEADS, nq, ATT_VROWS, ATT_TILE), BF16),
                        pltpu.VMEM((units, 1, ATT_TILE), F32),
                        pltpu.VMEM((units, ATT_VROWS, ATT_TILE), F32),
                        pltpu.VMEM((2, units, ATT_TILE, ATT_TILE), BF16),
                        pltpu.VMEM((2, units, 1, ATT_TILE), F32),
                        pltpu.VMEM((DA_DV, ATT_TILE), F32)],
        compiler_params=pltpu.CompilerParams(
            dimension_semantics=("arbitrary", "arbitrary", "arbitrary"),
            vmem_limit_bytes=VMEM_LIMIT),
        name="attn_prompt",
    )(qb, kb, v, *lams, gsub)


PROMPT_ROWS = 256


def kernel(x_prompt, x_sample, cache_k, cache_v, state_hgrn, page_table, ffn1_norm, ffn1_w_gate, ffn1_w_up, ffn1_w_down, mix_norm, w_in, hg_lb_logits, hg_out_norm, da_q_norm, da_k_norm, da_lambda_q1, da_lambda_k1, da_lambda_q2, da_lambda_k2, da_subln, w_out, ffn2_norm, ffn2_w_gate, ffn2_w_up, ffn2_w_down):
    batch, seq, d = x_prompt.shape
    nb = x_sample.shape[0]
    depth = ffn1_norm.shape[0]
    nmap = 2 * DA_HEADS
    yp = x_prompt.reshape(batch * seq, d)
    ys = x_sample.reshape(nb, d)
    kt_cache = jnp.transpose(cache_k, (0, 1, 3, 4, 2))
    v_cache = cache_v.reshape(cache_v.shape[:2] + (-1, DA_DV))
    outs = [[] for _ in range(6)]
    for l in range(depth):
        lam_init = 0.8 - 0.6 * math.exp(-0.3 * l)
        bf = lambda w: w[l].astype(BF16)
        w1 = (bf(ffn1_w_gate), bf(ffn1_w_up), bf(ffn1_w_down))
        w2 = (bf(ffn2_w_gate), bf(ffn2_w_up), bf(ffn2_w_down))
        win, wo = bf(w_in), bf(w_out)
        gq = jnp.tile(da_q_norm[l:l + 1], (1, nmap))
        gk = jnp.tile(da_k_norm[l:l + 1], (1, nmap))
        lams = (da_lambda_q1[l:l + 1], da_lambda_k1[l:l + 1], da_lambda_q2[l:l + 1],
                da_lambda_k2[l:l + 1])
        gout, gsub = hg_out_norm[l:l + 1], da_subln[l:l + 1]

        def inproj(x, tm, rows_per_seq, pages=None):
            return _ffn_inproj(x, ffn1_norm[l:l + 1], *w1, mix_norm[l:l + 1], win, gq, gk, tm,
                               rows_per_seq, pages)

        def outproj(h, ohg, oda, tm, pages=None):
            return _outproj_ffn(h, ohg, oda, wo, ffn2_norm[l:l + 1], *w2, tm, pages)

        def keys(kt):
            return jnp.transpose(kt.reshape(kt.shape[0], nmap, DA_DH, kt.shape[2]), (0, 3, 1, 2))

        hs, zhg_s, qs, kt_s, _, vs = inproj(ys, nb, nb)
        ks = keys(kt_s).reshape(nb, DA_WIDTH)
        ohg_s, s_new = _hgrn_sample(zhg_s, state_hgrn[l], hg_lb_logits, gout, l)
        kt_pages, v_pages = kt_cache[l:l + 1], v_cache[l:l + 1]
        half = page_table.shape[1] // 2
        h, zhg, qb, kt, kb, v, m_s, l_s, acc_s = inproj(
            yp, PROMPT_ROWS, seq, (page_table, qs, ks, vs, kt_pages, v_pages, half))
        ohg, s_fin = _hgrn_prompt(zhg, hg_lb_logits, gout, batch, l)
        oda = _attn_prompt(qb, kb, v, lams, gsub, batch, lam_init)
        yp, oda_s = outproj(h, ohg, oda, PROMPT_ROWS, (page_table, qs, m_s, l_s, acc_s, lams, gsub,
                                                       kt_pages, v_pages, half, lam_init))
        outs[0].append(keys(kt))
        outs[1].append(v.reshape(batch, seq, DA_HEADS, DA_DV))
        outs[2].append(s_fin)
        ys = outproj(hs, ohg_s.reshape(nb, HG_WIDTH), oda_s, nb)
        outs[3].append(ks.reshape(nb, 1, nmap, DA_DH))
        outs[4].append(vs.reshape(nb, 1, DA_HEADS, DA_DV))
        outs[5].append(s_new)
    stacked = [jnp.stack(o, axis=0) for o in outs]
    return (yp.reshape(batch, seq, d), ys.reshape(nb, 1, d), *stacked)
```

```python
import functools
import math

import jax
import jax.numpy as jnp
from jax import lax
from jax.experimental import pallas as pl
from jax.experimental.pallas import tpu as pltpu

F32 = jnp.float32
BF16 = jnp.bfloat16

NORM_EPS = 1e-6
NEG_INF = -1e30
LANES = 128
SUBLANES = 8
HG_HEADS = 4
HG_DK = 128
HG_DV = 128
HG_WIDTH = HG_HEADS * HG_DK
DA_HEADS = 4
DA_DH = 64
DA_DV = 128
DA_WIDTH = 2 * DA_HEADS * DA_DH
DA_SCALE = DA_DH ** -0.5
HG_COLS = 4 * HG_WIDTH
VMEM_LIMIT = 56 * 1024 * 1024

_NT = (((1,), (1,)), ((), ()))
_TN = (((0,), (0,)), ((), ()))


def _rmsnorm(x, g):
    ms = jnp.mean(x * x, axis=-1, keepdims=True)
    return x * lax.rsqrt(ms + NORM_EPS) * g


def _sigmoid(x):
    return 0.5 * jnp.tanh(0.5 * x) + 0.5


def _silu(x):
    return x * _sigmoid(x)


def _split3(x):
    hi = x.astype(BF16)
    r1 = x - hi.astype(F32)
    mid = r1.astype(BF16)
    lo = (r1 - mid.astype(F32)).astype(BF16)
    return hi, mid, lo


def _dot_exact_rhs(sel, x):
    hi, mid, lo = _split3(x)
    return (jnp.dot(sel, hi, preferred_element_type=F32)
            + jnp.dot(sel, mid, preferred_element_type=F32)
            + jnp.dot(sel, lo, preferred_element_type=F32))


def _dot_exact_lhs(x, sel):
    hi, mid, lo = _split3(x)
    return (jnp.dot(hi, sel, preferred_element_type=F32)
            + jnp.dot(mid, sel, preferred_element_type=F32)
            + jnp.dot(lo, sel, preferred_element_type=F32))


def _col_bcast(row, width=LANES):
    n = row.shape[1]
    k = 2 * SUBLANES
    first = lax.broadcasted_iota(jnp.int32, (k, n), 0) == 0
    rows = jnp.where(first, jnp.broadcast_to(row, (k, n)), 0.0)
    ones = jnp.ones((k, width), BF16)
    hi, mid, lo = _split3(rows)
    return (lax.dot_general(hi, ones, _TN, preferred_element_type=F32)
            + lax.dot_general(mid, ones, _TN, preferred_element_type=F32)
            + lax.dot_general(lo, ones, _TN, preferred_element_type=F32))


def _group64_rmsnorm(x, g):
    m, w = x.shape
    r = lax.broadcasted_iota(jnp.int32, (LANES, LANES), 0) // DA_DH
    c = lax.broadcasted_iota(jnp.int32, (LANES, LANES), 1) // DA_DH
    same = jnp.where(r == c, 1.0, 0.0).astype(BF16)
    x2 = x * x
    parts = [_dot_exact_lhs(x2[:, i * LANES:(i + 1) * LANES], same) for i in range(w // LANES)]
    ss = jnp.concatenate(parts, axis=-1)
    return x * lax.rsqrt(ss * (1.0 / DA_DH) + NORM_EPS) * g


def _lower_bound(logits, layer):
    m = jnp.max(logits, axis=0, keepdims=True)
    e = jnp.exp(logits - m)
    return jnp.sum(e[:layer + 1], axis=0, keepdims=True) / jnp.sum(e, axis=0, keepdims=True)


def _lambda(lq1_ref, lk1_ref, lq2_ref, lk2_ref, lam_init):
    a = jnp.sum(lq1_ref[...] * lk1_ref[...], axis=-1, keepdims=True)
    b = jnp.sum(lq2_ref[...] * lk2_ref[...], axis=-1, keepdims=True)
    return jnp.exp(a) - jnp.exp(b) + lam_init


FFN_PART = 1024


def _ffn_parts(dff):
    return [(lo, min(lo + FFN_PART, dff)) for lo in range(0, dff, FFN_PART)]


class _NoSide:
    def pre(self, j):
        pass

    def post(self, j):
        pass


def _swiglu_parts(xn, wg_ref, wu_ref, wd_ref, side):
    y = None
    for j, (lo, hi) in enumerate(_ffn_parts(wg_ref.shape[1])):
        side.pre(j)
        g = jnp.dot(xn, wg_ref[:, lo:hi], preferred_element_type=F32)
        u = jnp.dot(xn, wu_ref[:, lo:hi], preferred_element_type=F32)
        a = (_silu(g) * u).astype(BF16)
        part = jnp.dot(a, wd_ref[lo:hi, :], preferred_element_type=F32)
        y = part if y is None else y + part
        side.post(j)
    return y


def _ffn_inproj_body(x_ref, g1_ref, wg_ref, wu_ref, wd_ref, gm_ref, win_ref, gq_ref, gk_ref,
                     h_ref, zhg_ref, q_ref, kt_ref, kb_ref, v_ref, side):
    tm = x_ref.shape[0]
    last = len(_ffn_parts(wg_ref.shape[1]))
    x = x_ref[...]
    y = _swiglu_parts(_rmsnorm(x, g1_ref[...]).astype(BF16), wg_ref, wu_ref, wd_ref, side)
    side.pre(last)
    h = x + 0.5 * y
    h_ref[...] = h
    z = jnp.dot(_rmsnorm(h, gm_ref[...]).astype(BF16), win_ref[...], preferred_element_type=F32)
    zhg_ref[...] = z[:, :HG_COLS]
    dq = z[:, HG_COLS:HG_COLS + DA_WIDTH]
    dk = z[:, HG_COLS + DA_WIDTH:HG_COLS + 2 * DA_WIDTH]
    dv = z[:, HG_COLS + 2 * DA_WIDTH:]
    q_ref[...] = (_group64_rmsnorm(dq, gq_ref[...]) * DA_SCALE).astype(BF16)
    k = _group64_rmsnorm(dk, gk_ref[...])
    kt_ref[0] = k.T
    kb_ref[...] = k.astype(BF16)
    for hd in range(DA_HEADS):
        v_ref[pl.ds(hd, tm, stride=DA_HEADS), :] = dv[:, hd * DA_DV:(hd + 1) * DA_DV]
    side.post(last)


def _outproj_ffn_body(h_ref, ohg_ref, oda_ref, wo_ref, g2_ref, wg_ref, wu_ref, wd_ref, y_ref, side):
    last = len(_ffn_parts(wg_ref.shape[1]))
    h = (h_ref[...]
         + jnp.dot(ohg_ref[...], wo_ref[:HG_WIDTH, :], preferred_element_type=F32)
         + jnp.dot(oda_ref[...], wo_ref[HG_WIDTH:, :], preferred_element_type=F32))
    y = _swiglu_parts(_rmsnorm(h, g2_ref[...]).astype(BF16), wg_ref, wu_ref, wd_ref, side)
    side.pre(last)
    y_ref[...] = h + 0.5 * y
    side.post(last)


def _ffn_inproj_kernel(*refs):
    _ffn_inproj_body(*refs, _NoSide())


def _outproj_ffn_kernel(*refs):
    _outproj_ffn_body(*refs, _NoSide())


PAGE_SHARE_FIRST = 2


class _PageWalk:
    def __init__(self, s, n_steps, pt_ref, kt_hbm, v_hbm, kbuf, vbuf, sem, first_page, n_pages):
        nb = pt_ref.shape[0]
        assert n_steps % nb == 0
        self.steps_per_seq = n_steps // nb
        self.group = kbuf.shape[1]
        assert n_pages % (self.group * self.steps_per_seq) == 0
        self.groups = n_pages // (self.group * self.steps_per_seq)
        assert self.groups % 2 == 0
        self.s, self.n_steps, self.first_page = s, n_steps, first_page
        self.pt_ref, self.kt_hbm, self.v_hbm = pt_ref, kt_hbm, v_hbm
        self.kbuf, self.vbuf, self.sem = kbuf, vbuf, sem
        self.seq_first = (s % self.steps_per_seq) == 0
        self.seq_last = (s % self.steps_per_seq) == self.steps_per_seq - 1

    def first_page_of(self, step, g):
        return self.first_page + ((step % self.steps_per_seq) * self.groups + g) * self.group

    def _copies(self, step, g):
        seq = step // self.steps_per_seq
        page0 = self.first_page_of(step, g)
        slot = g % 2
        copies = []
        for i in range(self.group):
            pg = self.pt_ref[seq, page0 + i]
            copies.append(pltpu.make_async_copy(self.kt_hbm.at[0, pg], self.kbuf.at[slot, i],
                                                self.sem.at[slot, 0]))
            copies.append(pltpu.make_async_copy(self.v_hbm.at[0, pg], self.vbuf.at[slot, i],
                                                self.sem.at[slot, 1]))
        return copies

    def prime(self):
        for g in range(2):
            for c in self._copies(0, g):
                c.start()

    def wait(self, g):
        for c in self._copies(self.s, g):
            c.wait()

    def start_ahead(self, g):
        step = self.s + (g + 2) // self.groups

        @pl.when(step < self.n_steps)
        def _():
            for c in self._copies(step, (g + 2) % self.groups):
                c.start()


def _fold_pages(kbuf, vbuf, slot, first_pos, past_len, qcol_ref, expand_ref, m_ref, l_ref, acc_ref):
    nmap = 2 * DA_HEADS
    page = kbuf.shape[-1]
    vrows = vbuf.shape[-2]
    mrow = lax.broadcasted_iota(jnp.int32, (nmap, 1), 0)
    slope = jnp.exp2((-8.0 / DA_HEADS) * ((mrow // 2) + 1).astype(F32))
    own = (lax.broadcasted_iota(jnp.int32, (1, vrows), 1) % DA_HEADS) == (mrow // 2)
    lane = lax.broadcasted_iota(jnp.int32, (1, page), 1)
    qc = qcol_ref[...].reshape(nmap, DA_DH, page)
    scores = []
    for i in range(kbuf.shape[1]):
        s = jnp.sum(kbuf[slot, i] * qc, axis=1)
        pos = first_pos + i * page + lane
        scores.append(s - slope * (past_len - pos).astype(F32))
    m_old = m_ref[...]
    m_new = m_old
    for s in scores:
        m_new = jnp.maximum(m_new, jnp.max(s, axis=-1, keepdims=True))
    alpha = jnp.exp(m_old - m_new)
    l_new = alpha * l_ref[...]
    acc = alpha * acc_ref[...]
    probs = []
    for s in scores:
        p = jnp.exp(s - m_new)
        l_new = l_new + jnp.sum(p, axis=-1, keepdims=True)
        probs.append(p.astype(BF16))
    spread = jnp.dot(jnp.concatenate(probs, axis=0), expand_ref[...],
                     preferred_element_type=F32)
    for i in range(kbuf.shape[1]):
        w = jnp.where(own, spread[i * nmap:(i + 1) * nmap], 0.0).astype(BF16)
        acc = acc + jnp.dot(w, vbuf[slot, i].astype(BF16), preferred_element_type=F32)
    m_ref[...] = m_new
    l_ref[...] = l_new
    acc_ref[...] = acc


class _PageSide:
    def __init__(self, walk, n_stages, past_len, qcol_ref, expand_ref, m_ref, l_ref, acc_ref,
                 begin, finish):
        self.walk, self.n_stages, self.past_len = walk, n_stages, past_len
        self.state = (qcol_ref, expand_ref, m_ref, l_ref, acc_ref)
        self.begin, self.finish = begin, finish

    def _groups(self, j):
        return range(j, self.walk.groups, self.n_stages)

    def pre(self, j):
        walk = self.walk
        if j == 0:
            page, vrows = walk.kbuf.shape[-1], walk.vbuf.shape[-2]
            expand_ref = self.state[1]

            @pl.when(walk.s == 0)
            def _():
                walk.prime()
                tok = lax.broadcasted_iota(jnp.int32, (page, vrows), 0)
                vrow = lax.broadcasted_iota(jnp.int32, (page, vrows), 1)
                expand_ref[...] = jnp.where(vrow // DA_HEADS == tok, 1.0, 0.0).astype(BF16)

            pl.when(walk.seq_first)(self.begin)
        for g in self._groups(j):
            walk.wait(g)

    def post(self, j):
        walk = self.walk
        page = walk.kbuf.shape[-1]
        for g in self._groups(j):
            _fold_pages(walk.kbuf, walk.vbuf, g % 2, walk.first_page_of(walk.s, g) * page,
                        self.past_len, *self.state)
            walk.start_ahead(g)
        if j == self.n_stages - 1:
            pl.when(walk.seq_last)(self.finish)


def _ffn_inproj_pages_kernel(pt_ref, x_ref, g1_ref, wg_ref, wu_ref, wd_ref, gm_ref, win_ref, gq_ref,
                             gk_ref, qs_ref, kn_ref, vn_ref, kt_hbm, v_hbm,
                             h_ref, zhg_ref, q_ref, kt_ref, kb_ref, v_ref, mo_ref, lo_ref, ao_ref,
                             kbuf, vbuf, sem, qcol_ref, expand_ref, m_ref, l_ref, acc_ref,
                             *, n_steps, n_pages):
    nmap = 2 * DA_HEADS
    page = kbuf.shape[-1]
    walk = _PageWalk(pl.program_id(0), n_steps, pt_ref, kt_hbm, v_hbm, kbuf, vbuf, sem,
                     0, n_pages)

    def begin():
        qcol = _col_bcast(qs_ref[0].astype(F32), page)
        qcol_ref[...] = qcol
        kcol = _col_bcast(kn_ref[0], page)
        m_ref[...] = jnp.sum((qcol * kcol).reshape(nmap, DA_DH, page), axis=1)
        l_ref[...] = jnp.ones_like(l_ref)
        vn = vn_ref[0]
        acc_ref[...] = jnp.concatenate(
            [vn[:, (r // 2) * DA_DV:(r // 2 + 1) * DA_DV] for r in range(nmap)], axis=0)

    def finish():
        mo_ref[0] = m_ref[...]
        lo_ref[0] = l_ref[...]
        ao_ref[0] = acc_ref[...]

    n_stages = len(_ffn_parts(wg_ref.shape[1])) + 1
    side = _PageSide(walk, n_stages, pt_ref.shape[1] * page, qcol_ref, expand_ref, m_ref, l_ref,
                     acc_ref, begin, finish)
    _ffn_inproj_body(x_ref, g1_ref, wg_ref, wu_ref, wd_ref, gm_ref, win_ref, gq_ref, gk_ref,
                     h_ref, zhg_ref, q_ref, kt_ref, kb_ref, v_ref, side)


def _outproj_ffn_pages_kernel(pt_ref, h_ref, ohg_ref, oda_ref, wo_ref, g2_ref, wg_ref, wu_ref, wd_ref,
                              qs_ref, mi_ref, li_ref, ai_ref, lq1_ref, lk1_ref, lq2_ref, lk2_ref,
                              gsub_ref, kt_hbm, v_hbm, y_ref, os_ref,
                              kbuf, vbuf, sem, qcol_ref, expand_ref, m_ref, l_ref, acc_ref,
                              *, n_steps, first_page, lam_init):
    page = kbuf.shape[-1]
    n_pages = pt_ref.shape[1]
    walk = _PageWalk(pl.program_id(0), n_steps, pt_ref, kt_hbm, v_hbm, kbuf, vbuf, sem,
                     first_page, n_pages - first_page)

    def begin():
        qcol_ref[...] = _col_bcast(qs_ref[0].astype(F32), page)
        m_ref[...] = mi_ref[0]
        l_ref[...] = li_ref[0]
        acc_ref[...] = ai_ref[0]

    def finish():
        lam = _lambda(lq1_ref, lk1_ref, lq2_ref, lk2_ref, lam_init)
        acc, l_fin = acc_ref[...], l_ref[...]
        outs = []
        for hd in range(DA_HEADS):
            r1, r2 = 2 * hd, 2 * hd + 1
            outs.append(_diff_combine(acc[r1:r1 + 1], l_fin[r1:r1 + 1], acc[r2:r2 + 1],
                                      l_fin[r2:r2 + 1], lam, gsub_ref[...], lam_init))
        os_ref[0] = jnp.concatenate(outs, axis=-1).astype(BF16)

    n_stages = len(_ffn_parts(wg_ref.shape[1])) + 1
    side = _PageSide(walk, n_stages, n_pages * page, qcol_ref, expand_ref, m_ref, l_ref, acc_ref,
                     begin, finish)
    _outproj_ffn_body(h_ref, ohg_ref, oda_ref, wo_ref, g2_ref, wg_ref, wu_ref, wd_ref, y_ref, side)


def _resident(shape):
    return pl.BlockSpec(shape, lambda *_: (0,) * len(shape), pipeline_mode=pl.Buffered(1))


def _page_scratch(page, vrows, pages_per_step, groups_per_step):
    nmap = 2 * DA_HEADS
    assert pages_per_step % groups_per_step == 0 and groups_per_step % 2 == 0
    group = pages_per_step // groups_per_step
    return [pltpu.VMEM((2, group, nmap, DA_DH, page), F32),
            pltpu.VMEM((2, group, vrows, DA_DV), F32),
            pltpu.SemaphoreType.DMA((2, 2)),
            pltpu.VMEM((DA_WIDTH, page), F32), pltpu.VMEM((page, vrows), BF16),
            pltpu.VMEM((nmap, page), F32), pltpu.VMEM((nmap, page), F32),
            pltpu.VMEM((nmap, DA_DV), F32)]


def _ffn_inproj(x, g1, wg, wu, wd, gm, win, gq, gk, tm, seq, pages=None):
    n, d = x.shape
    dff = wg.shape[1]
    tiles = seq // tm
    rows = lambda w: pl.BlockSpec((tm, w), lambda i, *_: (i, 0))
    in_specs = [rows(d), _resident((1, d)), _resident((d, dff)), _resident((d, dff)),
                _resident((dff, d)), _resident((1, d)), _resident(win.shape),
                _resident((1, DA_WIDTH)), _resident((1, DA_WIDTH))]
    out_specs = [rows(d), rows(HG_COLS), rows(DA_WIDTH),
                 pl.BlockSpec((1, DA_WIDTH, tm), lambda i, *_: (i // tiles, 0, i % tiles)),
                 rows(DA_WIDTH),
                 pl.BlockSpec((tm * DA_HEADS, DA_DV), lambda i, *_: (i, 0))]
    out_shape = [jax.ShapeDtypeStruct((n, d), F32), jax.ShapeDtypeStruct((n, HG_COLS), F32),
                 jax.ShapeDtypeStruct((n, DA_WIDTH), BF16),
                 jax.ShapeDtypeStruct((n // seq, DA_WIDTH, seq), F32),
                 jax.ShapeDtypeStruct((n, DA_WIDTH), BF16),
                 jax.ShapeDtypeStruct((n * DA_HEADS, DA_DV), F32)]
    params = pltpu.CompilerParams(dimension_semantics=("arbitrary",), vmem_limit_bytes=VMEM_LIMIT)
    args = (x, g1, wg, wu, wd, gm, win, gq, gk)
    if pages is None:
        return pl.pallas_call(_ffn_inproj_kernel, grid=(n // tm,), in_specs=in_specs,
                              out_specs=out_specs, out_shape=out_shape, compiler_params=params,
                              name="ffn_inproj")(*args)
    page_table, qs, kn, vn, kt_pages, v_pages, n_pages = pages
    nb = page_table.shape[0]
    nmap = 2 * DA_HEADS
    page, vrows = kt_pages.shape[-1], v_pages.shape[-2]
    per_seq = (n // tm) // nb
    seq_row = lambda w: pl.BlockSpec((1, 1, w), lambda i, *_: (i // per_seq, 0, 0))
    state = lambda w: pl.BlockSpec((1, nmap, w), lambda i, *_: (i // per_seq, 0, 0))
    grid_spec = pltpu.PrefetchScalarGridSpec(
        num_scalar_prefetch=1, grid=(n // tm,),
        in_specs=in_specs + [seq_row(DA_WIDTH), seq_row(DA_WIDTH), seq_row(DA_HEADS * DA_DV),
                             pl.BlockSpec(memory_space=pl.ANY), pl.BlockSpec(memory_space=pl.ANY)],
        out_specs=out_specs + [state(page), state(page), state(DA_DV)],
        scratch_shapes=_page_scratch(page, vrows, n_pages // per_seq, 4))
    out_shape = out_shape + [jax.ShapeDtypeStruct((nb, nmap, page), F32),
                             jax.ShapeDtypeStruct((nb, nmap, page), F32),
                             jax.ShapeDtypeStruct((nb, nmap, DA_DV), F32)]
    return pl.pallas_call(
        functools.partial(_ffn_inproj_pages_kernel, n_steps=n // tm, n_pages=n_pages),
        grid_spec=grid_spec, out_shape=out_shape, compiler_params=params, name="ffn_inproj_pages",
    )(page_table, *args, qs.reshape(nb, 1, DA_WIDTH), kn.reshape(nb, 1, DA_WIDTH),
      vn.reshape(nb, 1, DA_HEADS * DA_DV), kt_pages, v_pages)


def _outproj_ffn(h, ohg, oda, wo, g2, wg, wu, wd, tm, pages=None):
    n, d = h.shape
    dff = wg.shape[1]
    rows = lambda w: pl.BlockSpec((tm, w), lambda i, *_: (i, 0))
    in_specs = [rows(d), rows(HG_WIDTH), rows(DA_WIDTH), _resident(wo.shape), _resident((1, d)),
                _resident((d, dff)), _resident((d, dff)), _resident((dff, d))]
    params = pltpu.CompilerParams(dimension_semantics=("arbitrary",), vmem_limit_bytes=VMEM_LIMIT)
    args = (h, ohg, oda, wo, g2, wg, wu, wd)
    if pages is None:
        return pl.pallas_call(_outproj_ffn_kernel, grid=(n // tm,), in_specs=in_specs,
                              out_specs=rows(d), out_shape=jax.ShapeDtypeStruct((n, d), F32),
                              compiler_params=params, name="outproj_ffn")(*args)
    page_table, qs, m, l, acc, lams, gsub, kt_pages, v_pages, first_page, lam_init = pages
    nb = page_table.shape[0]
    nmap = 2 * DA_HEADS
    page, vrows = kt_pages.shape[-1], v_pages.shape[-2]
    per_seq = (n // tm) // nb
    seq_row = lambda w: pl.BlockSpec((1, 1, w), lambda i, *_: (i // per_seq, 0, 0))
    state = lambda w: pl.BlockSpec((1, nmap, w), lambda i, *_: (i // per_seq, 0, 0))
    small = lambda a: pl.BlockSpec(a.shape, lambda i, *_: (0, 0))
    grid_spec = pltpu.PrefetchScalarGridSpec(
        num_scalar_prefetch=1, grid=(n // tm,),
        in_specs=in_specs + [seq_row(DA_WIDTH), state(page), state(page), state(DA_DV)]
                 + [small(a) for a in lams] + [small(gsub)]
                 + [pl.BlockSpec(memory_space=pl.ANY), pl.BlockSpec(memory_space=pl.ANY)],
        out_specs=[rows(d), seq_row(DA_HEADS * DA_DV)],
        scratch_shapes=_page_scratch(page, vrows, (page_table.shape[1] - first_page) // per_seq, 2))
    y, o_s = pl.pallas_call(
        functools.partial(_outproj_ffn_pages_kernel, n_steps=n // tm, first_page=first_page,
                          lam_init=lam_init),
        grid_spec=grid_spec,
        out_shape=[jax.ShapeDtypeStruct((n, d), F32),
                   jax.ShapeDtypeStruct((nb, 1, DA_HEADS * DA_DV), BF16)],
        compiler_params=params, name="outproj_ffn_pages",
    )(page_table, *args, qs.reshape(nb, 1, DA_WIDTH), m, l, acc, *lams, gsub, kt_pages, v_pages)
    return y, o_s.reshape(nb, DA_HEADS * DA_DV)


HG_CHUNK = 128
HG_BLOCK = 2
HG_GROUP = 4


def _hgrn_gates(z, lb):
    xq = z[:, :HG_WIDTH]
    xf = z[:, HG_WIDTH:2 * HG_WIDTH]
    xi = z[:, 2 * HG_WIDTH:3 * HG_WIDTH]
    xg = z[:, 3 * HG_WIDTH:]
    q = _silu(xq) * (HG_DK ** -0.5)
    f = lb + (1.0 - lb) * _sigmoid(xf)
    return q, 1.0 - f, f, xi, _silu(xg)


def _hgrn_chunk_kernel(z_ref, lbl_ref, gout_ref, o_ref, sfin_ref, st_ref, *, layer):
    c = pl.program_id(1)
    group, n = z_ref.shape[0], z_ref.shape[1]

    @pl.when(c == 0)
    def _():
        st_ref[...] = jnp.zeros_like(st_ref)

    lb = _lower_bound(lbl_ref[...], layer)
    row = lax.broadcasted_iota(jnp.int32, (n, n), 0)
    col = lax.broadcasted_iota(jnp.int32, (n, n), 1)
    tri = jnp.where(row >= col, 1.0, 0.0).astype(BF16)
    trow = lax.broadcasted_iota(jnp.int32, (n, 1), 0)
    spans = []
    w = HG_BLOCK
    while 2 * w <= n:
        span = 2 * w
        spans.append((w, span, (trow % span) >= w,
                      ((row // span) == (col // span)) & ((row % span) >= w) & ((col % span) < w)))
        w = span
    nears = [((row - col) == d) & ((row % HG_BLOCK) >= d) for d in range(HG_BLOCK)]

    for r in range(group):
        q, k, f, v, gate = _hgrn_gates(z_ref[r], lb)
        b = _dot_exact_rhs(tri, jnp.log2(f))
        a = [jnp.zeros((n, n), F32) for _ in range(HG_HEADS)]
        for w, span, right, lvl in spans:
            ref = jnp.concatenate(
                [jnp.broadcast_to(b[p * span + w - 1:p * span + w, :], (span, HG_WIDTH))
                 for p in range(n // span)], axis=0)
            e = jnp.exp2(-jnp.abs(b - ref))
            qw = jnp.where(right, q * e, 0.0).astype(BF16)
            kw = jnp.where(right, 0.0, k * e).astype(BF16)
            for h in range(HG_HEADS):
                hs = slice(h * HG_DK, (h + 1) * HG_DK)
                p = lax.dot_general(qw[:, hs], kw[:, hs], _NT, preferred_element_type=F32)
                a[h] = jnp.where(lvl, p, a[h])
        for d in range(HG_BLOCK):
            kd = pltpu.roll(k, d, axis=0) if d else k
            bd = pltpu.roll(b, d, axis=0) if d else b
            p = q * kd * jnp.exp2(jnp.minimum(b - bd, 0.0))
            for h in range(HG_HEADS):
                hs = slice(h * HG_DK, (h + 1) * HG_DK)
                a[h] = jnp.where(nears[d], jnp.sum(p[:, hs], axis=-1, keepdims=True), a[h])

        b_last = b[n - 1:n, :]
        q_in = (q * jnp.exp2(b)).astype(BF16)
        k_out = (k * jnp.exp2(b_last - b)).astype(BF16)
        carry = jnp.exp2(b_last)
        vb = v.astype(BF16)
        for h in range(HG_HEADS):
            hs = slice(h * HG_DK, (h + 1) * HG_DK)
            st = st_ref[r, h]
            o = (jnp.dot(a[h].astype(BF16), vb[:, hs], preferred_element_type=F32)
                 + lax.dot_general(q_in[:, hs], st.astype(BF16), _NT, preferred_element_type=F32))
            st_new = st * carry[:, hs] + lax.dot_general(vb[:, hs], k_out[:, hs], _TN,
                                                         preferred_element_type=F32)
            st_ref[r, h] = st_new
            o_ref[r, :, hs] = (_rmsnorm(o, gout_ref[...]) * gate[:, hs]).astype(BF16)

    @pl.when(c == pl.num_programs(1) - 1)
    def _():
        for r in range(group):
            for h in range(HG_HEADS):
                sfin_ref[r, h] = st_ref[r, h].T


def _hgrn_prompt(zhg, lb_logits, gout, batch, layer):
    n = zhg.shape[0]
    seq = n // batch
    nc = seq // HG_CHUNK
    group = HG_GROUP if batch % HG_GROUP == 0 else 1
    o, s_fin = pl.pallas_call(
        functools.partial(_hgrn_chunk_kernel, layer=layer),
        grid=(batch // group, nc),
        in_specs=[pl.BlockSpec((group, HG_CHUNK, HG_COLS), lambda g, c: (g, c, 0)),
                  pl.BlockSpec(lb_logits.shape, lambda g, c: (0, 0)),
                  pl.BlockSpec((1, HG_DV), lambda g, c: (0, 0))],
        out_specs=[pl.BlockSpec((group, HG_CHUNK, HG_WIDTH), lambda g, c: (g, c, 0)),
                   pl.BlockSpec((group, HG_HEADS, HG_DK, HG_DV), lambda g, c: (g, 0, 0, 0))],
        out_shape=[jax.ShapeDtypeStruct((batch, seq, HG_WIDTH), BF16),
                   jax.ShapeDtypeStruct((batch, HG_HEADS, HG_DK, HG_DV), F32)],
        scratch_shapes=[pltpu.VMEM((group, HG_HEADS, HG_DV, HG_DK), F32)],
        compiler_params=pltpu.CompilerParams(dimension_semantics=("arbitrary", "arbitrary")),
        name="hgrn_prompt",
    )(zhg.reshape(batch, seq, HG_COLS), lb_logits, gout)
    return o.reshape(n, HG_WIDTH), s_fin


HG_STEP_GROUP = 8


def _hgrn_step_kernel(z_ref, s_ref, lbl_ref, gout_ref, o_ref, snew_ref, *, layer):
    lb = _lower_bound(lbl_ref[...], layer)
    for r in range(z_ref.shape[0]):
        q, k, f, v, gate = _hgrn_gates(z_ref[r], lb)
        outs = []
        for h in range(HG_HEADS):
            hs = slice(h * HG_DK, (h + 1) * HG_DK)
            s_new = _col_bcast(f[:, hs]) * s_ref[r, h] + _col_bcast(k[:, hs]) * v[:, hs]
            snew_ref[r, h] = s_new
            o = jnp.sum(_col_bcast(q[:, hs]) * s_new, axis=0, keepdims=True)
            outs.append(_rmsnorm(o, gout_ref[...]) * gate[:, hs])
        o_ref[r] = jnp.concatenate(outs, axis=-1).astype(BF16)


def _hgrn_sample(zhg, state, lb_logits, gout, layer):
    nb = zhg.shape[0]
    grp = HG_STEP_GROUP if nb % HG_STEP_GROUP == 0 else 1
    return pl.pallas_call(
        functools.partial(_hgrn_step_kernel, layer=layer),
        grid=(nb // grp,),
        in_specs=[pl.BlockSpec((grp, 1, HG_COLS), lambda b: (b, 0, 0)),
                  pl.BlockSpec((grp, HG_HEADS, HG_DK, HG_DV), lambda b: (b, 0, 0, 0)),
                  pl.BlockSpec(lb_logits.shape, lambda b: (0, 0)),
                  pl.BlockSpec((1, HG_DV), lambda b: (0, 0))],
        out_specs=[pl.BlockSpec((grp, 1, HG_WIDTH), lambda b: (b, 0, 0)),
                   pl.BlockSpec((grp, HG_HEADS, HG_DK, HG_DV), lambda b: (b, 0, 0, 0))],
        out_shape=[jax.ShapeDtypeStruct((nb, 1, HG_WIDTH), BF16),
                   jax.ShapeDtypeStruct(state.shape, F32)],
        compiler_params=pltpu.CompilerParams(dimension_semantics=("arbitrary",)),
        name="hgrn_sample",
    )(zhg.reshape(nb, 1, HG_COLS), state, lb_logits, gout)


ATT_TILE = 512


def _head_slope(h):
    return jnp.exp2(jnp.full((1, 1), -8.0 / DA_HEADS, F32) * (h + 1).astype(F32))


def _diff_combine(acc1, l1, acc2, l2, lam, gsub, lam_init):
    out = acc1 / l1 - lam * (acc2 / l2)
    return _rmsnorm(out, gsub) * (1.0 - lam_init)


ATT_VROWS = DA_DV + 16


def _attn_prompt_kernel(q_ref, k_ref, v_ref, lq1_ref, lk1_ref, lq2_ref, lk2_ref, gsub_ref, o_ref,
                        kaug_ref, vt_ref, m_ref, acc_ref, p_ref, alpha_ref, gcol_ref, *, lam_init):
    qi = pl.program_id(2)
    t = q_ref.shape[0]
    nt = kaug_ref.shape[1]
    heads = vt_ref.shape[0]
    units = [(hh, i) for hh in range(heads) for i in range(2)]
    slopes = [_head_slope(pl.program_id(1) * heads + hh) for hh in range(heads)]
    lane = lax.broadcasted_iota(jnp.int32, (t, 2 * DA_DH), 1)
    loc = lax.broadcasted_iota(jnp.int32, (t, 2 * DA_DH), 0)
    loc_lo = (loc % 256).astype(F32)
    loc_hi = (loc - loc % 256).astype(F32)
    slot = [(1 - i) * DA_DH for i in range(2)]

    def augment(x, i, extras):
        out = jnp.zeros_like(x)
        for n, e in enumerate(extras):
            out = jnp.where(lane == slot[i] + n, e, out)
        return jnp.where((lane // DA_DH) == i, x, out)

    @pl.when(qi == 0)
    def _():
        for j in range(nt):
            for u, (hh, i) in enumerate(units):
                k = k_ref[j * t:(j + 1) * t, hh * 2 * DA_DH:(hh + 1) * 2 * DA_DH].astype(F32)
                kaug_ref[u, j] = augment(k, i, (slopes[hh] * loc_lo, slopes[hh] * loc_hi, 1.0, 1.0)
                                         ).astype(BF16)
            for hh in range(heads):
                head = pl.program_id(1) * heads + hh
                vj = v_ref[pl.ds(j * t * DA_HEADS + head, t, stride=DA_HEADS), :]
                vt_ref[hh, j, :DA_DV, :] = vj.T.astype(BF16)
                vt_ref[hh, j, DA_DV:, :] = jnp.ones((ATT_VROWS - DA_DV, t), BF16)
        gcol_ref[...] = _col_bcast(gsub_ref[...], t)

    qts = []
    for hh, i in units:
        q = q_ref[:, hh * 2 * DA_DH:(hh + 1) * 2 * DA_DH].astype(F32)
        qts.append(augment(q, i, (1.0, 1.0, -slopes[hh] * loc_lo, -slopes[hh] * loc_hi)
                           ).T.astype(BF16))
    m_ref[...] = jnp.full_like(m_ref, NEG_INF)
    acc_ref[...] = jnp.zeros_like(acc_ref)
    causal = (lax.broadcasted_iota(jnp.int32, (t, t), 0)
              <= lax.broadcasted_iota(jnp.int32, (t, t), 1))

    def values(ki, slot):
        for u, (hh, _) in enumerate(units):
            acc_ref[u] = (alpha_ref[slot, u] * acc_ref[u]
                          + jnp.dot(vt_ref[hh, ki], p_ref[slot, u], preferred_element_type=F32))

    def step(ki, slot, keep=None, first=False):
        ss = [jnp.dot(kaug_ref[u, ki], qts[u], preferred_element_type=F32)
              for u in range(len(units))]
        if not first:
            values(ki - 1, 1 - slot)
        for u, (hh, _) in enumerate(units):
            shift = -slopes[hh] * ((qi - ki) * t).astype(F32)
            s = ss[u] if keep is None else jnp.where(keep, ss[u], NEG_INF)
            m_old = m_ref[u]
            m_new = jnp.maximum(m_old, jnp.max(s, axis=0, keepdims=True) + shift)
            alpha_ref[slot, u] = jnp.exp(m_old - m_new)
            p_ref[slot, u] = jnp.exp(s - (m_new - shift)).astype(BF16)
            m_ref[u] = m_new

    pl.when(qi == 0)(lambda: step(0, 0, keep=causal, first=True))
    pl.when(qi > 0)(lambda: step(0, 0, first=True))
    inner = qi - 1

    def pair(j, carry):
        step(1 + 2 * j, 1)
        step(2 + 2 * j, 0)
        return carry

    lax.fori_loop(0, inner // 2, pair, 0)
    for parity in range(2):
        @pl.when(jnp.logical_and(qi > 0, qi % 2 == parity))
        def _():
            if parity == 0:
                step(qi - 1, 1)
            step(qi, parity, keep=causal)

        @pl.when(qi % 2 == parity)
        def _():
            values(qi, parity)

    lam = _lambda(lq1_ref, lk1_ref, lq2_ref, lk2_ref, lam_init)
    for hh in range(heads):
        a1, a2 = acc_ref[2 * hh], acc_ref[2 * hh + 1]
        inv1 = 1.0 / a1[DA_DV:DA_DV + 1]
        inv2 = lam / a2[DA_DV:DA_DV + 1]
        out = a1[:DA_DV] * inv1 - a2[:DA_DV] * inv2
        ms = jnp.mean(out * out, axis=0, keepdims=True)
        out = out * (lax.rsqrt(ms + NORM_EPS) * (1.0 - lam_init)) * gcol_ref[...]
        o_ref[:, hh * DA_DV:(hh + 1) * DA_DV] = out.T.astype(BF16)


ATT_HEADS = 1


def _attn_prompt(qb, kb, v, lams, gsub, batch, lam_init):
    n = qb.shape[0]
    seq = n // batch
    nq = seq // ATT_TILE
    units = 2 * ATT_HEADS
    small = lambda a: pl.BlockSpec(a.shape, lambda b, h, i: (0, 0))
    return pl.pallas_call(
        functools.partial(_attn_prompt_kernel, lam_init=lam_init),
        grid=(batch, DA_HEADS // ATT_HEADS, nq),
        in_specs=[pl.BlockSpec((ATT_TILE, ATT_HEADS * 2 * DA_DH), lambda b, h, i: (b * nq + i, h)),
                  pl.BlockSpec((seq, ATT_HEADS * 2 * DA_DH), lambda b, h, i: (b, h)),
                  pl.BlockSpec((seq * DA_HEADS, DA_DV), lambda b, h, i: (b, 0))]
                 + [small(a) for a in lams] + [small(gsub)],
        out_specs=pl.BlockSpec((ATT_TILE, ATT_HEADS * DA_DV), lambda b, h, i: (b * nq + i, h)),
        out_shape=jax.ShapeDtypeStruct((n, DA_HEADS * DA_DV), BF16),
        scratch_shapes=[pltpu.VMEM((units, nq, ATT_TILE, 2 * DA_DH), BF16),
                        pltpu.VMEM((ATT_HEADS, nq, ATT_VROWS, ATT_TILE), BF16),
                        pltpu.VMEM((units, 1, ATT_TILE), F32),
                        pltpu.VMEM((units, ATT_VROWS, ATT_TILE), F32),
                        pltpu.VMEM((2, units, ATT_TILE, ATT_TILE), BF16),
                        pltpu.VMEM((2, units, 1, ATT_TILE), F32),
                        pltpu.VMEM((DA_DV, ATT_TILE), F32)],
        compiler_params=pltpu.CompilerParams(
            dimension_semantics=("arbitrary", "arbitrary", "arbitrary"),
            vmem_limit_bytes=VMEM_LIMIT),
        name="attn_prompt",
    )(qb, kb, v, *lams, gsub)


PROMPT_ROWS = 256


def kernel(x_prompt, x_sample, cache_k, cache_v, state_hgrn, page_table, ffn1_norm, ffn1_w_gate, ffn1_w_up, ffn1_w_down, mix_norm, w_in, hg_lb_logits, hg_out_norm, da_q_norm, da_k_norm, da_lambda_q1, da_lambda_k1, da_lambda_q2, da_lambda_k2, da_subln, w_out, ffn2_norm, ffn2_w_gate, ffn2_w_up, ffn2_w_down):
    batch, seq, d = x_prompt.shape
    nb = x_sample.shape[0]
    depth = ffn1_norm.shape[0]
    nmap = 2 * DA_HEADS
    yp = x_prompt.reshape(batch * seq, d)
    ys = x_sample.reshape(nb, d)
    kt_cache = jnp.transpose(cache_k, (0, 1, 3, 4, 2))
    v_cache = cache_v.reshape(cache_v.shape[:2] + (-1, DA_DV))
    outs = [[] for _ in range(6)]
    for l in range(depth):
        lam_init = 0.8 - 0.6 * math.exp(-0.3 * l)
        bf = lambda w: w[l].astype(BF16)
        w1 = (bf(ffn1_w_gate), bf(ffn1_w_up), bf(ffn1_w_down))
        w2 = (bf(ffn2_w_gate), bf(ffn2_w_up), bf(ffn2_w_down))
        win, wo = bf(w_in), bf(w_out)
        gq = jnp.tile(da_q_norm[l:l + 1], (1, nmap))
        gk = jnp.tile(da_k_norm[l:l + 1], (1, nmap))
        lams = (da_lambda_q1[l:l + 1], da_lambda_k1[l:l + 1], da_lambda_q2[l:l + 1],
                da_lambda_k2[l:l + 1])
        gout, gsub = hg_out_norm[l:l + 1], da_subln[l:l + 1]

        def inproj(x, tm, rows_per_seq, pages=None):
            return _ffn_inproj(x, ffn1_norm[l:l + 1], *w1, mix_norm[l:l + 1], win, gq, gk, tm,
                               rows_per_seq, pages)

        def outproj(h, ohg, oda, tm, pages=None):
            return _outproj_ffn(h, ohg, oda, wo, ffn2_norm[l:l + 1], *w2, tm, pages)

        def keys(kt):
            return jnp.transpose(kt.reshape(kt.shape[0], nmap, DA_DH, kt.shape[2]), (0, 3, 1, 2))

        hs, zhg_s, qs, kt_s, _, vs = inproj(ys, nb, nb)
        ks = keys(kt_s).reshape(nb, DA_WIDTH)
        ohg_s, s_new = _hgrn_sample(zhg_s, state_hgrn[l], hg_lb_logits, gout, l)
        kt_pages, v_pages = kt_cache[l:l + 1], v_cache[l:l + 1]
        first = page_table.shape[1] // PAGE_SHARE_FIRST
        h, zhg, qb, kt, kb, v, m_s, l_s, acc_s = inproj(
            yp, PROMPT_ROWS, seq, (page_table, qs, ks, vs, kt_pages, v_pages, first))
        ohg, s_fin = _hgrn_prompt(zhg, hg_lb_logits, gout, batch, l)
        oda = _attn_prompt(qb, kb, v, lams, gsub, batch, lam_init)
        yp, oda_s = outproj(h, ohg, oda, PROMPT_ROWS, (page_table, qs, m_s, l_s, acc_s, lams, gsub,
                                                       kt_pages, v_pages, first, lam_init))
        outs[0].append(keys(kt))
        outs[1].append(v.reshape(batch, seq, DA_HEADS, DA_DV))
        outs[2].append(s_fin)
        ys = outproj(hs, ohg_s.reshape(nb, HG_WIDTH), oda_s, nb)
        outs[3].append(ks.reshape(nb, 1, nmap, DA_DH))
        outs[4].append(vs.reshape(nb, 1, DA_HEADS, DA_DV))
        outs[5].append(s_new)
    stacked = [jnp.stack(o, axis=0) for o in outs]
    return (yp.reshape(batch, seq, d), ys.reshape(nb, 1, d), *stacked)
```

```python
import functools
import math

import jax
import jax.numpy as jnp
from jax import lax
from jax.experimental import pallas as pl
from jax.experimental.pallas import tpu as pltpu

F32 = jnp.float32
BF16 = jnp.bfloat16

NORM_EPS = 1e-6
NEG_INF = -1e30
LANES = 128
SUBLANES = 8
HG_HEADS = 4
HG_DK = 128
HG_DV = 128
HG_WIDTH = HG_HEADS * HG_DK
DA_HEADS = 4
DA_DH = 64
DA_DV = 128
DA_WIDTH = 2 * DA_HEADS * DA_DH
DA_SCALE = DA_DH ** -0.5
HG_COLS = 4 * HG_WIDTH
VMEM_LIMIT = 60 * 1024 * 1024

_NT = (((1,), (1,)), ((), ()))
_TN = (((0,), (0,)), ((), ()))


def _rmsnorm(x, g):
    ms = jnp.mean(x * x, axis=-1, keepdims=True)
    return x * lax.rsqrt(ms + NORM_EPS) * g


def _sigmoid(x):
    return 0.5 * jnp.tanh(0.5 * x) + 0.5


def _silu(x):
    return x * _sigmoid(x)


def _split3(x):
    hi = x.astype(BF16)
    r1 = x - hi.astype(F32)
    mid = r1.astype(BF16)
    lo = (r1 - mid.astype(F32)).astype(BF16)
    return hi, mid, lo


def _dot_exact_rhs(sel, x):
    hi, mid, lo = _split3(x)
    return (jnp.dot(sel, hi, preferred_element_type=F32)
            + jnp.dot(sel, mid, preferred_element_type=F32)
            + jnp.dot(sel, lo, preferred_element_type=F32))


def _dot_exact_lhs(x, sel):
    hi, mid, lo = _split3(x)
    return (jnp.dot(hi, sel, preferred_element_type=F32)
            + jnp.dot(mid, sel, preferred_element_type=F32)
            + jnp.dot(lo, sel, preferred_element_type=F32))


def _col_bcast(row, width=LANES):
    n = row.shape[1]
    k = 2 * SUBLANES
    first = lax.broadcasted_iota(jnp.int32, (k, n), 0) == 0
    rows = jnp.where(first, jnp.broadcast_to(row, (k, n)), 0.0)
    ones = jnp.ones((k, width), BF16)
    hi, mid, lo = _split3(rows)
    return (lax.dot_general(hi, ones, _TN, preferred_element_type=F32)
            + lax.dot_general(mid, ones, _TN, preferred_element_type=F32)
            + lax.dot_general(lo, ones, _TN, preferred_element_type=F32))


def _group64_rmsnorm(x, g):
    m, w = x.shape
    r = lax.broadcasted_iota(jnp.int32, (LANES, LANES), 0) // DA_DH
    c = lax.broadcasted_iota(jnp.int32, (LANES, LANES), 1) // DA_DH
    same = jnp.where(r == c, 1.0, 0.0).astype(BF16)
    x2 = x * x
    parts = [_dot_exact_lhs(x2[:, i * LANES:(i + 1) * LANES], same) for i in range(w // LANES)]
    ss = jnp.concatenate(parts, axis=-1)
    return x * lax.rsqrt(ss * (1.0 / DA_DH) + NORM_EPS) * g


def _lower_bound(logits, layer):
    m = jnp.max(logits, axis=0, keepdims=True)
    e = jnp.exp(logits - m)
    return jnp.sum(e[:layer + 1], axis=0, keepdims=True) / jnp.sum(e, axis=0, keepdims=True)


def _lambda(lq1_ref, lk1_ref, lq2_ref, lk2_ref, lam_init):
    a = jnp.sum(lq1_ref[...] * lk1_ref[...], axis=-1, keepdims=True)
    b = jnp.sum(lq2_ref[...] * lk2_ref[...], axis=-1, keepdims=True)
    return jnp.exp(a) - jnp.exp(b) + lam_init


FFN_PART = 1024


def _ffn_parts(dff):
    return [(lo, min(lo + FFN_PART, dff)) for lo in range(0, dff, FFN_PART)]


class _NoSide:
    def pre(self, j):
        pass

    def post(self, j):
        pass


def _swiglu_parts(xn, wg_ref, wu_ref, wd_ref, side):
    y = None
    for j, (lo, hi) in enumerate(_ffn_parts(wg_ref.shape[1])):
        side.pre(j)
        g = jnp.dot(xn, wg_ref[:, lo:hi], preferred_element_type=F32)
        u = jnp.dot(xn, wu_ref[:, lo:hi], preferred_element_type=F32)
        a = (_silu(g) * u).astype(BF16)
        part = jnp.dot(a, wd_ref[lo:hi, :], preferred_element_type=F32)
        y = part if y is None else y + part
        side.post(j)
    return y


def _ffn_inproj_body(x_ref, g1_ref, wg_ref, wu_ref, wd_ref, gm_ref, win_ref, gq_ref, gk_ref,
                     h_ref, zhg_ref, q_ref, kt_ref, kb_ref, v_ref, side):
    tm = x_ref.shape[0]
    last = len(_ffn_parts(wg_ref.shape[1]))
    x = x_ref[...]
    y = _swiglu_parts(_rmsnorm(x, g1_ref[...]).astype(BF16), wg_ref, wu_ref, wd_ref, side)
    side.pre(last)
    h = x + 0.5 * y
    h_ref[...] = h
    z = jnp.dot(_rmsnorm(h, gm_ref[...]).astype(BF16), win_ref[...], preferred_element_type=F32)
    zhg_ref[...] = z[:, :HG_COLS]
    dq = z[:, HG_COLS:HG_COLS + DA_WIDTH]
    dk = z[:, HG_COLS + DA_WIDTH:HG_COLS + 2 * DA_WIDTH]
    dv = z[:, HG_COLS + 2 * DA_WIDTH:]
    q_ref[...] = (_group64_rmsnorm(dq, gq_ref[...]) * DA_SCALE).astype(BF16)
    k = _group64_rmsnorm(dk, gk_ref[...])
    kt_ref[0] = k.T
    kb_ref[...] = k.astype(BF16)
    for hd in range(DA_HEADS):
        v_ref[pl.ds(hd, tm, stride=DA_HEADS), :] = dv[:, hd * DA_DV:(hd + 1) * DA_DV]
    side.post(last)


def _outproj_ffn_body(h_ref, ohg_ref, oda_ref, wo_ref, g2_ref, wg_ref, wu_ref, wd_ref, y_ref, side):
    last = len(_ffn_parts(wg_ref.shape[1]))
    h = (h_ref[...]
         + jnp.dot(ohg_ref[...], wo_ref[:HG_WIDTH, :], preferred_element_type=F32)
         + jnp.dot(oda_ref[...], wo_ref[HG_WIDTH:, :], preferred_element_type=F32))
    y = _swiglu_parts(_rmsnorm(h, g2_ref[...]).astype(BF16), wg_ref, wu_ref, wd_ref, side)
    side.pre(last)
    y_ref[...] = h + 0.5 * y
    side.post(last)


def _ffn_inproj_kernel(*refs):
    _ffn_inproj_body(*refs, _NoSide())


def _outproj_ffn_kernel(*refs):
    _outproj_ffn_body(*refs, _NoSide())


PAGE_SHARE_FIRST = 2


class _PageWalk:
    def __init__(self, s, n_steps, pt_ref, kt_hbm, v_hbm, kbuf, vbuf, sem, first_page, n_pages):
        nb = pt_ref.shape[0]
        assert n_steps % nb == 0
        self.steps_per_seq = n_steps // nb
        self.group = kbuf.shape[1]
        assert n_pages % (self.group * self.steps_per_seq) == 0
        self.groups = n_pages // (self.group * self.steps_per_seq)
        assert self.groups % 2 == 0
        self.s, self.n_steps, self.first_page = s, n_steps, first_page
        self.pt_ref, self.kt_hbm, self.v_hbm = pt_ref, kt_hbm, v_hbm
        self.kbuf, self.vbuf, self.sem = kbuf, vbuf, sem
        self.seq_first = (s % self.steps_per_seq) == 0
        self.seq_last = (s % self.steps_per_seq) == self.steps_per_seq - 1

    def first_page_of(self, step, g):
        return self.first_page + ((step % self.steps_per_seq) * self.groups + g) * self.group

    def _copies(self, step, g):
        seq = step // self.steps_per_seq
        page0 = self.first_page_of(step, g)
        slot = g % 2
        copies = []
        for i in range(self.group):
            pg = self.pt_ref[seq, page0 + i]
            copies.append(pltpu.make_async_copy(self.kt_hbm.at[0, pg], self.kbuf.at[slot, i],
                                                self.sem.at[slot, 0]))
            copies.append(pltpu.make_async_copy(self.v_hbm.at[0, pg], self.vbuf.at[slot, i],
                                                self.sem.at[slot, 1]))
        return copies

    def prime(self):
        for g in range(2):
            for c in self._copies(0, g):
                c.start()

    def wait(self, g):
        for c in self._copies(self.s, g):
            c.wait()

    def start_ahead(self, g):
        step = self.s + (g + 2) // self.groups

        @pl.when(step < self.n_steps)
        def _():
            for c in self._copies(step, (g + 2) % self.groups):
                c.start()


def _fold_pages(kbuf, vbuf, slot, first_pos, past_len, qcol_ref, expand_ref, m_ref, l_ref, acc_ref):
    nmap = 2 * DA_HEADS
    page = kbuf.shape[-1]
    vrows = vbuf.shape[-2]
    mrow = lax.broadcasted_iota(jnp.int32, (nmap, 1), 0)
    slope = jnp.exp2((-8.0 / DA_HEADS) * ((mrow // 2) + 1).astype(F32))
    own = (lax.broadcasted_iota(jnp.int32, (1, vrows), 1) % DA_HEADS) == (mrow // 2)
    lane = lax.broadcasted_iota(jnp.int32, (1, page), 1)
    qc = qcol_ref[...].reshape(nmap, DA_DH, page)
    scores = []
    for i in range(kbuf.shape[1]):
        s = jnp.sum(kbuf[slot, i] * qc, axis=1)
        pos = first_pos + i * page + lane
        scores.append(s - slope * (past_len - pos).astype(F32))
    m_old = m_ref[...]
    m_new = m_old
    for s in scores:
        m_new = jnp.maximum(m_new, jnp.max(s, axis=-1, keepdims=True))
    alpha = jnp.exp(m_old - m_new)
    l_new = alpha * l_ref[...]
    acc = alpha * acc_ref[...]
    probs = []
    for s in scores:
        p = jnp.exp(s - m_new)
        l_new = l_new + jnp.sum(p, axis=-1, keepdims=True)
        probs.append(p.astype(BF16))
    spread = jnp.dot(jnp.concatenate(probs, axis=0), expand_ref[...],
                     preferred_element_type=F32)
    for i in range(kbuf.shape[1]):
        w = jnp.where(own, spread[i * nmap:(i + 1) * nmap], 0.0).astype(BF16)
        acc = acc + jnp.dot(w, vbuf[slot, i].astype(BF16), preferred_element_type=F32)
    m_ref[...] = m_new
    l_ref[...] = l_new
    acc_ref[...] = acc


class _PageSide:
    def __init__(self, walk, n_stages, past_len, qcol_ref, expand_ref, m_ref, l_ref, acc_ref,
                 begin, finish):
        self.walk, self.n_stages, self.past_len = walk, n_stages, past_len
        self.state = (qcol_ref, expand_ref, m_ref, l_ref, acc_ref)
        self.begin, self.finish = begin, finish

    def _groups(self, j):
        return range(j, self.walk.groups, self.n_stages)

    def pre(self, j):
        walk = self.walk
        if j == 0:
            page, vrows = walk.kbuf.shape[-1], walk.vbuf.shape[-2]
            expand_ref = self.state[1]

            @pl.when(walk.s == 0)
            def _():
                walk.prime()
                tok = lax.broadcasted_iota(jnp.int32, (page, vrows), 0)
                vrow = lax.broadcasted_iota(jnp.int32, (page, vrows), 1)
                expand_ref[...] = jnp.where(vrow // DA_HEADS == tok, 1.0, 0.0).astype(BF16)

            pl.when(walk.seq_first)(self.begin)
        for g in self._groups(j):
            walk.wait(g)

    def post(self, j):
        walk = self.walk
        page = walk.kbuf.shape[-1]
        for g in self._groups(j):
            _fold_pages(walk.kbuf, walk.vbuf, g % 2, walk.first_page_of(walk.s, g) * page,
                        self.past_len, *self.state)
            walk.start_ahead(g)
        if j == self.n_stages - 1:
            pl.when(walk.seq_last)(self.finish)


def _ffn_inproj_pages_kernel(pt_ref, x_ref, g1_ref, wg_ref, wu_ref, wd_ref, gm_ref, win_ref, gq_ref,
                             gk_ref, qs_ref, kn_ref, vn_ref, kt_hbm, v_hbm,
                             h_ref, zhg_ref, q_ref, kt_ref, kb_ref, v_ref, mo_ref, lo_ref, ao_ref,
                             kbuf, vbuf, sem, qcol_ref, expand_ref, m_ref, l_ref, acc_ref,
                             *, n_steps, n_pages):
    nmap = 2 * DA_HEADS
    page = kbuf.shape[-1]
    walk = _PageWalk(pl.program_id(0), n_steps, pt_ref, kt_hbm, v_hbm, kbuf, vbuf, sem,
                     0, n_pages)

    def begin():
        qcol = _col_bcast(qs_ref[0].astype(F32), page)
        qcol_ref[...] = qcol
        kcol = _col_bcast(kn_ref[0], page)
        m_ref[...] = jnp.sum((qcol * kcol).reshape(nmap, DA_DH, page), axis=1)
        l_ref[...] = jnp.ones_like(l_ref)
        vn = vn_ref[0]
        acc_ref[...] = jnp.concatenate(
            [vn[:, (r // 2) * DA_DV:(r // 2 + 1) * DA_DV] for r in range(nmap)], axis=0)

    def finish():
        mo_ref[0] = m_ref[...]
        lo_ref[0] = l_ref[...]
        ao_ref[0] = acc_ref[...]

    n_stages = len(_ffn_parts(wg_ref.shape[1])) + 1
    side = _PageSide(walk, n_stages, pt_ref.shape[1] * page, qcol_ref, expand_ref, m_ref, l_ref,
                     acc_ref, begin, finish)
    _ffn_inproj_body(x_ref, g1_ref, wg_ref, wu_ref, wd_ref, gm_ref, win_ref, gq_ref, gk_ref,
                     h_ref, zhg_ref, q_ref, kt_ref, kb_ref, v_ref, side)


def _outproj_ffn_pages_kernel(pt_ref, h_ref, ohg_ref, oda_ref, wo_ref, g2_ref, wg_ref, wu_ref, wd_ref,
                              qs_ref, mi_ref, li_ref, ai_ref, lq1_ref, lk1_ref, lq2_ref, lk2_ref,
                              gsub_ref, kt_hbm, v_hbm, y_ref, os_ref,
                              kbuf, vbuf, sem, qcol_ref, expand_ref, m_ref, l_ref, acc_ref,
                              *, n_steps, first_page, lam_init):
    page = kbuf.shape[-1]
    n_pages = pt_ref.shape[1]
    walk = _PageWalk(pl.program_id(0), n_steps, pt_ref, kt_hbm, v_hbm, kbuf, vbuf, sem,
                     first_page, n_pages - first_page)

    def begin():
        qcol_ref[...] = _col_bcast(qs_ref[0].astype(F32), page)
        m_ref[...] = mi_ref[0]
        l_ref[...] = li_ref[0]
        acc_ref[...] = ai_ref[0]

    def finish():
        lam = _lambda(lq1_ref, lk1_ref, lq2_ref, lk2_ref, lam_init)
        acc, l_fin = acc_ref[...], l_ref[...]
        outs = []
        for hd in range(DA_HEADS):
            r1, r2 = 2 * hd, 2 * hd + 1
            outs.append(_diff_combine(acc[r1:r1 + 1], l_fin[r1:r1 + 1], acc[r2:r2 + 1],
                                      l_fin[r2:r2 + 1], lam, gsub_ref[...], lam_init))
        os_ref[0] = jnp.concatenate(outs, axis=-1).astype(BF16)

    n_stages = len(_ffn_parts(wg_ref.shape[1])) + 1
    side = _PageSide(walk, n_stages, n_pages * page, qcol_ref, expand_ref, m_ref, l_ref, acc_ref,
                     begin, finish)
    _outproj_ffn_body(h_ref, ohg_ref, oda_ref, wo_ref, g2_ref, wg_ref, wu_ref, wd_ref, y_ref, side)


def _resident(shape):
    return pl.BlockSpec(shape, lambda *_: (0,) * len(shape), pipeline_mode=pl.Buffered(1))


def _page_scratch(page, vrows, pages_per_step, groups_per_step):
    nmap = 2 * DA_HEADS
    assert pages_per_step % groups_per_step == 0 and groups_per_step % 2 == 0
    group = pages_per_step // groups_per_step
    return [pltpu.VMEM((2, group, nmap, DA_DH, page), F32),
            pltpu.VMEM((2, group, vrows, DA_DV), F32),
            pltpu.SemaphoreType.DMA((2, 2)),
            pltpu.VMEM((DA_WIDTH, page), F32), pltpu.VMEM((page, vrows), BF16),
            pltpu.VMEM((nmap, page), F32), pltpu.VMEM((nmap, page), F32),
            pltpu.VMEM((nmap, DA_DV), F32)]


def _ffn_inproj(x, g1, wg, wu, wd, gm, win, gq, gk, tm, seq, pages=None):
    n, d = x.shape
    dff = wg.shape[1]
    tiles = seq // tm
    rows = lambda w: pl.BlockSpec((tm, w), lambda i, *_: (i, 0))
    in_specs = [rows(d), _resident((1, d)), _resident((d, dff)), _resident((d, dff)),
                _resident((dff, d)), _resident((1, d)), _resident(win.shape),
                _resident((1, DA_WIDTH)), _resident((1, DA_WIDTH))]
    out_specs = [rows(d), rows(HG_COLS), rows(DA_WIDTH),
                 pl.BlockSpec((1, DA_WIDTH, tm), lambda i, *_: (i // tiles, 0, i % tiles)),
                 rows(DA_WIDTH),
                 pl.BlockSpec((tm * DA_HEADS, DA_DV), lambda i, *_: (i, 0))]
    out_shape = [jax.ShapeDtypeStruct((n, d), F32), jax.ShapeDtypeStruct((n, HG_COLS), F32),
                 jax.ShapeDtypeStruct((n, DA_WIDTH), BF16),
                 jax.ShapeDtypeStruct((n // seq, DA_WIDTH, seq), F32),
                 jax.ShapeDtypeStruct((n, DA_WIDTH), BF16),
                 jax.ShapeDtypeStruct((n * DA_HEADS, DA_DV), F32)]
    params = pltpu.CompilerParams(dimension_semantics=("arbitrary",), vmem_limit_bytes=VMEM_LIMIT)
    args = (x, g1, wg, wu, wd, gm, win, gq, gk)
    if pages is None:
        return pl.pallas_call(_ffn_inproj_kernel, grid=(n // tm,), in_specs=in_specs,
                              out_specs=out_specs, out_shape=out_shape, compiler_params=params,
                              name="ffn_inproj")(*args)
    page_table, qs, kn, vn, kt_pages, v_pages, n_pages = pages
    nb = page_table.shape[0]
    nmap = 2 * DA_HEADS
    page, vrows = kt_pages.shape[-1], v_pages.shape[-2]
    per_seq = (n // tm) // nb
    seq_row = lambda w: pl.BlockSpec((1, 1, w), lambda i, *_: (i // per_seq, 0, 0))
    state = lambda w: pl.BlockSpec((1, nmap, w), lambda i, *_: (i // per_seq, 0, 0))
    grid_spec = pltpu.PrefetchScalarGridSpec(
        num_scalar_prefetch=1, grid=(n // tm,),
        in_specs=in_specs + [seq_row(DA_WIDTH), seq_row(DA_WIDTH), seq_row(DA_HEADS * DA_DV),
                             pl.BlockSpec(memory_space=pl.ANY), pl.BlockSpec(memory_space=pl.ANY)],
        out_specs=out_specs + [state(page), state(page), state(DA_DV)],
        scratch_shapes=_page_scratch(page, vrows, n_pages // per_seq, 2))
    out_shape = out_shape + [jax.ShapeDtypeStruct((nb, nmap, page), F32),
                             jax.ShapeDtypeStruct((nb, nmap, page), F32),
                             jax.ShapeDtypeStruct((nb, nmap, DA_DV), F32)]
    return pl.pallas_call(
        functools.partial(_ffn_inproj_pages_kernel, n_steps=n // tm, n_pages=n_pages),
        grid_spec=grid_spec, out_shape=out_shape, compiler_params=params, name="ffn_inproj_pages",
    )(page_table, *args, qs.reshape(nb, 1, DA_WIDTH), kn.reshape(nb, 1, DA_WIDTH),
      vn.reshape(nb, 1, DA_HEADS * DA_DV), kt_pages, v_pages)


def _outproj_ffn(h, ohg, oda, wo, g2, wg, wu, wd, tm, pages=None):
    n, d = h.shape
    dff = wg.shape[1]
    rows = lambda w: pl.BlockSpec((tm, w), lambda i, *_: (i, 0))
    in_specs = [rows(d), rows(HG_WIDTH), rows(DA_WIDTH), _resident(wo.shape), _resident((1, d)),
                _resident((d, dff)), _resident((d, dff)), _resident((dff, d))]
    params = pltpu.CompilerParams(dimension_semantics=("arbitrary",), vmem_limit_bytes=VMEM_LIMIT)
    args = (h, ohg, oda, wo, g2, wg, wu, wd)
    if pages is None:
        return pl.pallas_call(_outproj_ffn_kernel, grid=(n // tm,), in_specs=in_specs,
                              out_specs=rows(d), out_shape=jax.ShapeDtypeStruct((n, d), F32),
                              compiler_params=params, name="outproj_ffn")(*args)
    page_table, qs, m, l, acc, lams, gsub, kt_pages, v_pages, first_page, lam_init = pages
    nb = page_table.shape[0]
    nmap = 2 * DA_HEADS
    page, vrows = kt_pages.shape[-1], v_pages.shape[-2]
    per_seq = (n // tm) // nb
    seq_row = lambda w: pl.BlockSpec((1, 1, w), lambda i, *_: (i // per_seq, 0, 0))
    state = lambda w: pl.BlockSpec((1, nmap, w), lambda i, *_: (i // per_seq, 0, 0))
    small = lambda a: pl.BlockSpec(a.shape, lambda i, *_: (0, 0))
    grid_spec = pltpu.PrefetchScalarGridSpec(
        num_scalar_prefetch=1, grid=(n // tm,),
        in_specs=in_specs + [seq_row(DA_WIDTH), state(page), state(page), state(DA_DV)]
                 + [small(a) for a in lams] + [small(gsub)]
                 + [pl.BlockSpec(memory_space=pl.ANY), pl.BlockSpec(memory_space=pl.ANY)],
        out_specs=[rows(d), seq_row(DA_HEADS * DA_DV)],
        scratch_shapes=_page_scratch(page, vrows, (page_table.shape[1] - first_page) // per_seq, 2))
    y, o_s = pl.pallas_call(
        functools.partial(_outproj_ffn_pages_kernel, n_steps=n // tm, first_page=first_page,
                          lam_init=lam_init),
        grid_spec=grid_spec,
        out_shape=[jax.ShapeDtypeStruct((n, d), F32),
                   jax.ShapeDtypeStruct((nb, 1, DA_HEADS * DA_DV), BF16)],
        compiler_params=params, name="outproj_ffn_pages",
    )(page_table, *args, qs.reshape(nb, 1, DA_WIDTH), m, l, acc, *lams, gsub, kt_pages, v_pages)
    return y, o_s.reshape(nb, DA_HEADS * DA_DV)


HG_CHUNK = 128
HG_BLOCK = 2
HG_GROUP = 4


def _hgrn_gates(z, lb):
    xq = z[:, :HG_WIDTH]
    xf = z[:, HG_WIDTH:2 * HG_WIDTH]
    xi = z[:, 2 * HG_WIDTH:3 * HG_WIDTH]
    xg = z[:, 3 * HG_WIDTH:]
    q = _silu(xq) * (HG_DK ** -0.5)
    f = lb + (1.0 - lb) * _sigmoid(xf)
    return q, 1.0 - f, f, xi, _silu(xg)


def _hgrn_chunk_kernel(z_ref, lbl_ref, gout_ref, o_ref, sfin_ref, st_ref, *, layer):
    c = pl.program_id(1)
    group, n = z_ref.shape[0], z_ref.shape[1]

    @pl.when(c == 0)
    def _():
        st_ref[...] = jnp.zeros_like(st_ref)

    lb = _lower_bound(lbl_ref[...], layer)
    row = lax.broadcasted_iota(jnp.int32, (n, n), 0)
    col = lax.broadcasted_iota(jnp.int32, (n, n), 1)
    tri = jnp.where(row >= col, 1.0, 0.0).astype(BF16)
    trow = lax.broadcasted_iota(jnp.int32, (n, 1), 0)
    spans = []
    w = HG_BLOCK
    while 2 * w <= n:
        span = 2 * w
        spans.append((w, span, (trow % span) >= w,
                      ((row // span) == (col // span)) & ((row % span) >= w) & ((col % span) < w)))
        w = span
    nears = [((row - col) == d) & ((row % HG_BLOCK) >= d) for d in range(HG_BLOCK)]

    for r in range(group):
        q, k, f, v, gate = _hgrn_gates(z_ref[r], lb)
        b = _dot_exact_rhs(tri, jnp.log2(f))
        a = [jnp.zeros((n, n), F32) for _ in range(HG_HEADS)]
        for w, span, right, lvl in spans:
            ref = jnp.concatenate(
                [jnp.broadcast_to(b[p * span + w - 1:p * span + w, :], (span, HG_WIDTH))
                 for p in range(n // span)], axis=0)
            e = jnp.exp2(-jnp.abs(b - ref))
            qw = jnp.where(right, q * e, 0.0).astype(BF16)
            kw = jnp.where(right, 0.0, k * e).astype(BF16)
            for h in range(HG_HEADS):
                hs = slice(h * HG_DK, (h + 1) * HG_DK)
                p = lax.dot_general(qw[:, hs], kw[:, hs], _NT, preferred_element_type=F32)
                a[h] = jnp.where(lvl, p, a[h])
        for d in range(HG_BLOCK):
            kd = pltpu.roll(k, d, axis=0) if d else k
            bd = pltpu.roll(b, d, axis=0) if d else b
            p = q * kd * jnp.exp2(jnp.minimum(b - bd, 0.0))
            for h in range(HG_HEADS):
                hs = slice(h * HG_DK, (h + 1) * HG_DK)
                a[h] = jnp.where(nears[d], jnp.sum(p[:, hs], axis=-1, keepdims=True), a[h])

        b_last = b[n - 1:n, :]
        q_in = (q * jnp.exp2(b)).astype(BF16)
        k_out = (k * jnp.exp2(b_last - b)).astype(BF16)
        carry = jnp.exp2(b_last)
        vb = v.astype(BF16)
        for h in range(HG_HEADS):
            hs = slice(h * HG_DK, (h + 1) * HG_DK)
            st = st_ref[r, h]
            o = (jnp.dot(a[h].astype(BF16), vb[:, hs], preferred_element_type=F32)
                 + lax.dot_general(q_in[:, hs], st.astype(BF16), _NT, preferred_element_type=F32))
            st_new = st * carry[:, hs] + lax.dot_general(vb[:, hs], k_out[:, hs], _TN,
                                                         preferred_element_type=F32)
            st_ref[r, h] = st_new
            o_ref[r, :, hs] = (_rmsnorm(o, gout_ref[...]) * gate[:, hs]).astype(BF16)

    @pl.when(c == pl.num_programs(1) - 1)
    def _():
        for r in range(group):
            for h in range(HG_HEADS):
                sfin_ref[r, h] = st_ref[r, h].T


def _hgrn_prompt(zhg, lb_logits, gout, batch, layer):
    n = zhg.shape[0]
    seq = n // batch
    nc = seq // HG_CHUNK
    group = HG_GROUP if batch % HG_GROUP == 0 else 1
    o, s_fin = pl.pallas_call(
        functools.partial(_hgrn_chunk_kernel, layer=layer),
        grid=(batch // group, nc),
        in_specs=[pl.BlockSpec((group, HG_CHUNK, HG_COLS), lambda g, c: (g, c, 0)),
                  pl.BlockSpec(lb_logits.shape, lambda g, c: (0, 0)),
                  pl.BlockSpec((1, HG_DV), lambda g, c: (0, 0))],
        out_specs=[pl.BlockSpec((group, HG_CHUNK, HG_WIDTH), lambda g, c: (g, c, 0)),
                   pl.BlockSpec((group, HG_HEADS, HG_DK, HG_DV), lambda g, c: (g, 0, 0, 0))],
        out_shape=[jax.ShapeDtypeStruct((batch, seq, HG_WIDTH), BF16),
                   jax.ShapeDtypeStruct((batch, HG_HEADS, HG_DK, HG_DV), F32)],
        scratch_shapes=[pltpu.VMEM((group, HG_HEADS, HG_DV, HG_DK), F32)],
        compiler_params=pltpu.CompilerParams(dimension_semantics=("arbitrary", "arbitrary")),
        name="hgrn_prompt",
    )(zhg.reshape(batch, seq, HG_COLS), lb_logits, gout)
    return o.reshape(n, HG_WIDTH), s_fin


HG_STEP_GROUP = 8


def _hgrn_step_kernel(z_ref, s_ref, lbl_ref, gout_ref, o_ref, snew_ref, *, layer):
    lb = _lower_bound(lbl_ref[...], layer)
    for r in range(z_ref.shape[0]):
        q, k, f, v, gate = _hgrn_gates(z_ref[r], lb)
        outs = []
        for h in range(HG_HEADS):
            hs = slice(h * HG_DK, (h + 1) * HG_DK)
            s_new = _col_bcast(f[:, hs]) * s_ref[r, h] + _col_bcast(k[:, hs]) * v[:, hs]
            snew_ref[r, h] = s_new
            o = jnp.sum(_col_bcast(q[:, hs]) * s_new, axis=0, keepdims=True)
            outs.append(_rmsnorm(o, gout_ref[...]) * gate[:, hs])
        o_ref[r] = jnp.concatenate(outs, axis=-1).astype(BF16)


def _hgrn_sample(zhg, state, lb_logits, gout, layer):
    nb = zhg.shape[0]
    grp = HG_STEP_GROUP if nb % HG_STEP_GROUP == 0 else 1
    return pl.pallas_call(
        functools.partial(_hgrn_step_kernel, layer=layer),
        grid=(nb // grp,),
        in_specs=[pl.BlockSpec((grp, 1, HG_COLS), lambda b: (b, 0, 0)),
                  pl.BlockSpec((grp, HG_HEADS, HG_DK, HG_DV), lambda b: (b, 0, 0, 0)),
                  pl.BlockSpec(lb_logits.shape, lambda b: (0, 0)),
                  pl.BlockSpec((1, HG_DV), lambda b: (0, 0))],
        out_specs=[pl.BlockSpec((grp, 1, HG_WIDTH), lambda b: (b, 0, 0)),
                   pl.BlockSpec((grp, HG_HEADS, HG_DK, HG_DV), lambda b: (b, 0, 0, 0))],
        out_shape=[jax.ShapeDtypeStruct((nb, 1, HG_WIDTH), BF16),
                   jax.ShapeDtypeStruct(state.shape, F32)],
        compiler_params=pltpu.CompilerParams(dimension_semantics=("arbitrary",)),
        name="hgrn_sample",
    )(zhg.reshape(nb, 1, HG_COLS), state, lb_logits, gout)


ATT_TILE = 512


def _head_slope(h):
    return jnp.exp2(jnp.full((1, 1), -8.0 / DA_HEADS, F32) * (h + 1).astype(F32))


def _diff_combine(acc1, l1, acc2, l2, lam, gsub, lam_init):
    out = acc1 / l1 - lam * (acc2 / l2)
    return _rmsnorm(out, gsub) * (1.0 - lam_init)


ATT_VROWS = DA_DV + 16


def _attn_prompt_kernel(q_ref, k_ref, v_ref, lq1_ref, lk1_ref, lq2_ref, lk2_ref, gsub_ref, o_ref,
                        kaug_ref, vt_ref, m_ref, acc_ref, p_ref, alpha_ref, gcol_ref, *, lam_init):
    qi = pl.program_id(2)
    t = q_ref.shape[0]
    nt = kaug_ref.shape[1]
    heads = vt_ref.shape[0]
    units = [(hh, i) for hh in range(heads) for i in range(2)]
    slopes = [_head_slope(pl.program_id(1) * heads + hh) for hh in range(heads)]
    lane = lax.broadcasted_iota(jnp.int32, (t, 2 * DA_DH), 1)
    loc = lax.broadcasted_iota(jnp.int32, (t, 2 * DA_DH), 0)
    loc_lo = (loc % 256).astype(F32)
    loc_hi = (loc - loc % 256).astype(F32)
    slot = [(1 - i) * DA_DH for i in range(2)]

    def augment(x, i, extras):
        out = jnp.zeros_like(x)
        for n, e in enumerate(extras):
            out = jnp.where(lane == slot[i] + n, e, out)
        return jnp.where((lane // DA_DH) == i, x, out)

    @pl.when(qi == 0)
    def _():
        for j in range(nt):
            for u, (hh, i) in enumerate(units):
                k = k_ref[j * t:(j + 1) * t, hh * 2 * DA_DH:(hh + 1) * 2 * DA_DH].astype(F32)
                kaug_ref[u, j] = augment(k, i, (slopes[hh] * loc_lo, slopes[hh] * loc_hi, 1.0, 1.0)
                                         ).astype(BF16)
            for hh in range(heads):
                head = pl.program_id(1) * heads + hh
                vj = v_ref[pl.ds(j * t * DA_HEADS + head, t, stride=DA_HEADS), :]
                vt_ref[hh, j, :DA_DV, :] = vj.T.astype(BF16)
                vt_ref[hh, j, DA_DV:, :] = jnp.ones((ATT_VROWS - DA_DV, t), BF16)
        gcol_ref[...] = _col_bcast(gsub_ref[...], t)

    qts = []
    for hh, i in units:
        q = q_ref[:, hh * 2 * DA_DH:(hh + 1) * 2 * DA_DH].astype(F32)
        qts.append(augment(q, i, (1.0, 1.0, -slopes[hh] * loc_lo, -slopes[hh] * loc_hi)
                           ).T.astype(BF16))
    m_ref[...] = jnp.full_like(m_ref, NEG_INF)
    acc_ref[...] = jnp.zeros_like(acc_ref)
    causal = (lax.broadcasted_iota(jnp.int32, (t, t), 0)
              <= lax.broadcasted_iota(jnp.int32, (t, t), 1))

    def values(ki, slot):
        for u, (hh, _) in enumerate(units):
            acc_ref[u] = (alpha_ref[slot, u] * acc_ref[u]
                          + jnp.dot(vt_ref[hh, ki], p_ref[slot, u], preferred_element_type=F32))

    def step(ki, slot, keep=None, first=False):
        ss = [jnp.dot(kaug_ref[u, ki], qts[u], preferred_element_type=F32)
              for u in range(len(units))]
        if not first:
            values(ki - 1, 1 - slot)
        for u, (hh, _) in enumerate(units):
            shift = -slopes[hh] * ((qi - ki) * t).astype(F32)
            s = ss[u] if keep is None else jnp.where(keep, ss[u], NEG_INF)
            m_old = m_ref[u]
            m_new = jnp.maximum(m_old, jnp.max(s, axis=0, keepdims=True) + shift)
            alpha_ref[slot, u] = jnp.exp(m_old - m_new)
            p_ref[slot, u] = jnp.exp(s - (m_new - shift)).astype(BF16)
            m_ref[u] = m_new

    pl.when(qi == 0)(lambda: step(0, 0, keep=causal, first=True))
    pl.when(qi > 0)(lambda: step(0, 0, first=True))
    inner = qi - 1

    def pair(j, carry):
        step(1 + 2 * j, 1)
        step(2 + 2 * j, 0)
        return carry

    lax.fori_loop(0, inner // 2, pair, 0)
    for parity in range(2):
        @pl.when(jnp.logical_and(qi > 0, qi % 2 == parity))
        def _():
            if parity == 0:
                step(qi - 1, 1)
            step(qi, parity, keep=causal)

        @pl.when(qi % 2 == parity)
        def _():
            values(qi, parity)

    lam = _lambda(lq1_ref, lk1_ref, lq2_ref, lk2_ref, lam_init)
    for hh in range(heads):
        a1, a2 = acc_ref[2 * hh], acc_ref[2 * hh + 1]
        inv1 = 1.0 / a1[DA_DV:DA_DV + 1]
        inv2 = lam / a2[DA_DV:DA_DV + 1]
        out = a1[:DA_DV] * inv1 - a2[:DA_DV] * inv2
        ms = jnp.mean(out * out, axis=0, keepdims=True)
        out = out * (lax.rsqrt(ms + NORM_EPS) * (1.0 - lam_init)) * gcol_ref[...]
        o_ref[:, hh * DA_DV:(hh + 1) * DA_DV] = out.T.astype(BF16)


ATT_HEADS = 1


def _attn_prompt(qb, kb, v, lams, gsub, batch, lam_init):
    n = qb.shape[0]
    seq = n // batch
    nq = seq // ATT_TILE
    units = 2 * ATT_HEADS
    small = lambda a: pl.BlockSpec(a.shape, lambda b, h, i: (0, 0))
    return pl.pallas_call(
        functools.partial(_attn_prompt_kernel, lam_init=lam_init),
        grid=(batch, DA_HEADS // ATT_HEADS, nq),
        in_specs=[pl.BlockSpec((ATT_TILE, ATT_HEADS * 2 * DA_DH), lambda b, h, i: (b * nq + i, h)),
                  pl.BlockSpec((seq, ATT_HEADS * 2 * DA_DH), lambda b, h, i: (b, h)),
                  pl.BlockSpec((seq * DA_HEADS, DA_DV), lambda b, h, i: (b, 0))]
                 + [small(a) for a in lams] + [small(gsub)],
        out_specs=pl.BlockSpec((ATT_TILE, ATT_HEADS * DA_DV), lambda b, h, i: (b * nq + i, h)),
        out_shape=jax.ShapeDtypeStruct((n, DA_HEADS * DA_DV), BF16),
        scratch_shapes=[pltpu.VMEM((units, nq, ATT_TILE, 2 * DA_DH), BF16),
                        pltpu.VMEM((ATT_HEADS, nq, ATT_VROWS, ATT_TILE), BF16),
                        pltpu.VMEM((units, 1, ATT_TILE), F32),
                        pltpu.VMEM((units, ATT_VROWS, ATT_TILE), F32),
                        pltpu.VMEM((2, units, ATT_TILE, ATT_TILE), BF16),
                        pltpu.VMEM((2, units, 1, ATT_TILE), F32),
                        pltpu.VMEM((DA_DV, ATT_TILE), F32)],
        compiler_params=pltpu.CompilerParams(
            dimension_semantics=("arbitrary", "arbitrary", "arbitrary"),
            vmem_limit_bytes=VMEM_LIMIT),
        name="attn_prompt",
    )(qb, kb, v, *lams, gsub)


PROMPT_ROWS = 256


def kernel(x_prompt, x_sample, cache_k, cache_v, state_hgrn, page_table, ffn1_norm, ffn1_w_gate, ffn1_w_up, ffn1_w_down, mix_norm, w_in, hg_lb_logits, hg_out_norm, da_q_norm, da_k_norm, da_lambda_q1, da_lambda_k1, da_lambda_q2, da_lambda_k2, da_subln, w_out, ffn2_norm, ffn2_w_gate, ffn2_w_up, ffn2_w_down):
    batch, seq, d = x_prompt.shape
    nb = x_sample.shape[0]
    depth = ffn1_norm.shape[0]
    nmap = 2 * DA_HEADS
    yp = x_prompt.reshape(batch * seq, d)
    ys = x_sample.reshape(nb, d)
    kt_cache = jnp.transpose(cache_k, (0, 1, 3, 4, 2))
    v_cache = cache_v.reshape(cache_v.shape[:2] + (-1, DA_DV))
    outs = [[] for _ in range(6)]
    for l in range(depth):
        lam_init = 0.8 - 0.6 * math.exp(-0.3 * l)
        bf = lambda w: w[l].astype(BF16)
        w1 = (bf(ffn1_w_gate), bf(ffn1_w_up), bf(ffn1_w_down))
        w2 = (bf(ffn2_w_gate), bf(ffn2_w_up), bf(ffn2_w_down))
        win, wo = bf(w_in), bf(w_out)
        gq = jnp.tile(da_q_norm[l:l + 1], (1, nmap))
        gk = jnp.tile(da_k_norm[l:l + 1], (1, nmap))
        lams = (da_lambda_q1[l:l + 1], da_lambda_k1[l:l + 1], da_lambda_q2[l:l + 1],
                da_lambda_k2[l:l + 1])
        gout, gsub = hg_out_norm[l:l + 1], da_subln[l:l + 1]

        def inproj(x, tm, rows_per_seq, pages=None):
            return _ffn_inproj(x, ffn1_norm[l:l + 1], *w1, mix_norm[l:l + 1], win, gq, gk, tm,
                               rows_per_seq, pages)

        def outproj(h, ohg, oda, tm, pages=None):
            return _outproj_ffn(h, ohg, oda, wo, ffn2_norm[l:l + 1], *w2, tm, pages)

        def keys(kt):
            return jnp.transpose(kt.reshape(kt.shape[0], nmap, DA_DH, kt.shape[2]), (0, 3, 1, 2))

        hs, zhg_s, qs, kt_s, _, vs = inproj(ys, nb, nb)
        ks = keys(kt_s).reshape(nb, DA_WIDTH)
        ohg_s, s_new = _hgrn_sample(zhg_s, state_hgrn[l], hg_lb_logits, gout, l)
        kt_pages, v_pages = kt_cache[l:l + 1], v_cache[l:l + 1]
        first = page_table.shape[1] // PAGE_SHARE_FIRST
        h, zhg, qb, kt, kb, v, m_s, l_s, acc_s = inproj(
            yp, PROMPT_ROWS, seq, (page_table, qs, ks, vs, kt_pages, v_pages, first))
        ohg, s_fin = _hgrn_prompt(zhg, hg_lb_logits, gout, batch, l)
        oda = _attn_prompt(qb, kb, v, lams, gsub, batch, lam_init)
        yp, oda_s = outproj(h, ohg, oda, PROMPT_ROWS, (page_table, qs, m_s, l_s, acc_s, lams, gsub,
                                                       kt_pages, v_pages, first, lam_init))
        outs[0].append(keys(kt))
        outs[1].append(v.reshape(batch, seq, DA_HEADS, DA_DV))
        outs[2].append(s_fin)
        ys = outproj(hs, ohg_s.reshape(nb, HG_WIDTH), oda_s, nb)
        outs[3].append(ks.reshape(nb, 1, nmap, DA_DH))
        outs[4].append(vs.reshape(nb, 1, DA_HEADS, DA_DV))
        outs[5].append(s_new)
    stacked = [jnp.stack(o, axis=0) for o in outs]
    return (yp.reshape(batch, seq, d), ys.reshape(nb, 1, d), *stacked)
```

```python
import functools
import math

import jax
import jax.numpy as jnp
from jax import lax
from jax.experimental import pallas as pl
from jax.experimental.pallas import tpu as pltpu

F32 = jnp.float32
BF16 = jnp.bfloat16

NORM_EPS = 1e-6
NEG_INF = -1e30
LANES = 128
SUBLANES = 8
HG_HEADS = 4
HG_DK = 128
HG_DV = 128
HG_WIDTH = HG_HEADS * HG_DK
DA_HEADS = 4
DA_DH = 64
DA_DV = 128
DA_WIDTH = 2 * DA_HEADS * DA_DH
DA_SCALE = DA_DH ** -0.5
HG_COLS = 4 * HG_WIDTH
VMEM_LIMIT = 60 * 1024 * 1024

_NT = (((1,), (1,)), ((), ()))
_TN = (((0,), (0,)), ((), ()))


def _rmsnorm(x, g):
    ms = jnp.mean(x * x, axis=-1, keepdims=True)
    return x * lax.rsqrt(ms + NORM_EPS) * g


def _sigmoid(x):
    return 0.5 * jnp.tanh(0.5 * x) + 0.5


def _silu(x):
    return x * _sigmoid(x)


def _split3(x):
    hi = x.astype(BF16)
    r1 = x - hi.astype(F32)
    mid = r1.astype(BF16)
    lo = (r1 - mid.astype(F32)).astype(BF16)
    return hi, mid, lo


def _dot_exact_rhs(sel, x):
    hi, mid, lo = _split3(x)
    return (jnp.dot(sel, hi, preferred_element_type=F32)
            + jnp.dot(sel, mid, preferred_element_type=F32)
            + jnp.dot(sel, lo, preferred_element_type=F32))


def _dot_exact_lhs(x, sel):
    hi, mid, lo = _split3(x)
    return (jnp.dot(hi, sel, preferred_element_type=F32)
            + jnp.dot(mid, sel, preferred_element_type=F32)
            + jnp.dot(lo, sel, preferred_element_type=F32))


def _col_bcast(row, width=LANES):
    n = row.shape[1]
    k = 2 * SUBLANES
    first = lax.broadcasted_iota(jnp.int32, (k, n), 0) == 0
    rows = jnp.where(first, jnp.broadcast_to(row, (k, n)), 0.0)
    ones = jnp.ones((k, width), BF16)
    hi, mid, lo = _split3(rows)
    return (lax.dot_general(hi, ones, _TN, preferred_element_type=F32)
            + lax.dot_general(mid, ones, _TN, preferred_element_type=F32)
            + lax.dot_general(lo, ones, _TN, preferred_element_type=F32))


def _group64_rmsnorm(x, g):
    m, w = x.shape
    r = lax.broadcasted_iota(jnp.int32, (LANES, LANES), 0) // DA_DH
    c = lax.broadcasted_iota(jnp.int32, (LANES, LANES), 1) // DA_DH
    same = jnp.where(r == c, 1.0, 0.0).astype(BF16)
    x2 = x * x
    parts = [_dot_exact_lhs(x2[:, i * LANES:(i + 1) * LANES], same) for i in range(w // LANES)]
    ss = jnp.concatenate(parts, axis=-1)
    return x * lax.rsqrt(ss * (1.0 / DA_DH) + NORM_EPS) * g


def _lower_bound(logits, layer):
    m = jnp.max(logits, axis=0, keepdims=True)
    e = jnp.exp(logits - m)
    return jnp.sum(e[:layer + 1], axis=0, keepdims=True) / jnp.sum(e, axis=0, keepdims=True)


def _lambda(lq1_ref, lk1_ref, lq2_ref, lk2_ref, lam_init):
    a = jnp.sum(lq1_ref[...] * lk1_ref[...], axis=-1, keepdims=True)
    b = jnp.sum(lq2_ref[...] * lk2_ref[...], axis=-1, keepdims=True)
    return jnp.exp(a) - jnp.exp(b) + lam_init


FFN_PART = 1024


def _ffn_parts(dff):
    return [(lo, min(lo + FFN_PART, dff)) for lo in range(0, dff, FFN_PART)]


class _NoSide:
    def pre(self, j):
        pass

    def post(self, j):
        pass


def _swiglu_parts(xn, wg_ref, wu_ref, wd_ref, side):
    y = None
    for j, (lo, hi) in enumerate(_ffn_parts(wg_ref.shape[1])):
        side.pre(j)
        g = jnp.dot(xn, wg_ref[:, lo:hi], preferred_element_type=F32)
        u = jnp.dot(xn, wu_ref[:, lo:hi], preferred_element_type=F32)
        a = (_silu(g) * u).astype(BF16)
        part = jnp.dot(a, wd_ref[lo:hi, :], preferred_element_type=F32)
        y = part if y is None else y + part
        side.post(j)
    return y


def _ffn_inproj_body(x_ref, g1_ref, wg_ref, wu_ref, wd_ref, gm_ref, win_ref, gq_ref, gk_ref,
                     h_ref, zhg_ref, q_ref, kt_ref, kb_ref, v_ref, side):
    tm = x_ref.shape[0]
    last = len(_ffn_parts(wg_ref.shape[1]))
    x = x_ref[...]
    y = _swiglu_parts(_rmsnorm(x, g1_ref[...]).astype(BF16), wg_ref, wu_ref, wd_ref, side)
    side.pre(last)
    h = x + 0.5 * y
    h_ref[...] = h
    z = jnp.dot(_rmsnorm(h, gm_ref[...]).astype(BF16), win_ref[...], preferred_element_type=F32)
    zhg_ref[...] = z[:, :HG_COLS]
    dq = z[:, HG_COLS:HG_COLS + DA_WIDTH]
    dk = z[:, HG_COLS + DA_WIDTH:HG_COLS + 2 * DA_WIDTH]
    dv = z[:, HG_COLS + 2 * DA_WIDTH:]
    q_ref[...] = (_group64_rmsnorm(dq, gq_ref[...]) * DA_SCALE).astype(BF16)
    k = _group64_rmsnorm(dk, gk_ref[...])
    kt_ref[0] = k.T
    kb_ref[...] = k.astype(BF16)
    for hd in range(DA_HEADS):
        v_ref[pl.ds(hd, tm, stride=DA_HEADS), :] = dv[:, hd * DA_DV:(hd + 1) * DA_DV]
    side.post(last)


def _outproj_ffn_body(h_ref, ohg_ref, oda_ref, wo_ref, g2_ref, wg_ref, wu_ref, wd_ref, y_ref, side):
    last = len(_ffn_parts(wg_ref.shape[1]))
    h = (h_ref[...]
         + jnp.dot(ohg_ref[...], wo_ref[:HG_WIDTH, :], preferred_element_type=F32)
         + jnp.dot(oda_ref[...], wo_ref[HG_WIDTH:, :], preferred_element_type=F32))
    y = _swiglu_parts(_rmsnorm(h, g2_ref[...]).astype(BF16), wg_ref, wu_ref, wd_ref, side)
    side.pre(last)
    y_ref[...] = h + 0.5 * y
    side.post(last)


def _ffn_inproj_kernel(*refs):
    _ffn_inproj_body(*refs, _NoSide())


def _outproj_ffn_kernel(*refs):
    _outproj_ffn_body(*refs, _NoSide())


PAGE_SHARE_FIRST = 2


class _PageWalk:
    def __init__(self, s, n_steps, pt_ref, kt_hbm, v_hbm, kbuf, vbuf, sem, first_page, n_pages):
        nb = pt_ref.shape[0]
        assert n_steps % nb == 0
        self.steps_per_seq = n_steps // nb
        self.group = kbuf.shape[1]
        assert n_pages % (self.group * self.steps_per_seq) == 0
        self.groups = n_pages // (self.group * self.steps_per_seq)
        assert self.groups % 2 == 0
        self.s, self.n_steps, self.first_page = s, n_steps, first_page
        self.pt_ref, self.kt_hbm, self.v_hbm = pt_ref, kt_hbm, v_hbm
        self.kbuf, self.vbuf, self.sem = kbuf, vbuf, sem
        self.seq_first = (s % self.steps_per_seq) == 0
        self.seq_last = (s % self.steps_per_seq) == self.steps_per_seq - 1

    def first_page_of(self, step, g):
        return self.first_page + ((step % self.steps_per_seq) * self.groups + g) * self.group

    def _copies(self, step, g):
        seq = step // self.steps_per_seq
        page0 = self.first_page_of(step, g)
        slot = g % 2
        copies = []
        for i in range(self.group):
            pg = self.pt_ref[seq, page0 + i]
            copies.append(pltpu.make_async_copy(self.kt_hbm.at[0, pg], self.kbuf.at[slot, i],
                                                self.sem.at[slot, 0]))
            copies.append(pltpu.make_async_copy(self.v_hbm.at[0, pg], self.vbuf.at[slot, i],
                                                self.sem.at[slot, 1]))
        return copies

    def prime(self):
        for g in range(2):
            for c in self._copies(0, g):
                c.start()

    def wait(self, g):
        for c in self._copies(self.s, g):
            c.wait()

    def start_ahead(self, g):
        step = self.s + (g + 2) // self.groups

        @pl.when(step < self.n_steps)
        def _():
            for c in self._copies(step, (g + 2) % self.groups):
                c.start()


def _fold_pages(kbuf, vbuf, slot, first_pos, past_len, qcol_ref, expand_ref, m_ref, l_ref, acc_ref):
    nmap = 2 * DA_HEADS
    page = kbuf.shape[-1]
    vrows = vbuf.shape[-2]
    mrow = lax.broadcasted_iota(jnp.int32, (nmap, 1), 0)
    slope = jnp.exp2((-8.0 / DA_HEADS) * ((mrow // 2) + 1).astype(F32))
    own = (lax.broadcasted_iota(jnp.int32, (1, vrows), 1) % DA_HEADS) == (mrow // 2)
    lane = lax.broadcasted_iota(jnp.int32, (1, page), 1)
    qc = qcol_ref[...].reshape(nmap, DA_DH, page)
    scores = []
    for i in range(kbuf.shape[1]):
        s = jnp.sum(kbuf[slot, i] * qc, axis=1)
        pos = first_pos + i * page + lane
        scores.append(s - slope * (past_len - pos).astype(F32))
    m_old = m_ref[...]
    m_new = m_old
    for s in scores:
        m_new = jnp.maximum(m_new, jnp.max(s, axis=-1, keepdims=True))
    alpha = jnp.exp(m_old - m_new)
    l_new = alpha * l_ref[...]
    acc = alpha * acc_ref[...]
    probs = []
    for s in scores:
        p = jnp.exp(s - m_new)
        l_new = l_new + jnp.sum(p, axis=-1, keepdims=True)
        probs.append(p.astype(BF16))
    spread = jnp.dot(jnp.concatenate(probs, axis=0), expand_ref[...],
                     preferred_element_type=F32)
    for i in range(kbuf.shape[1]):
        w = jnp.where(own, spread[i * nmap:(i + 1) * nmap], 0.0).astype(BF16)
        acc = acc + jnp.dot(w, vbuf[slot, i].astype(BF16), preferred_element_type=F32)
    m_ref[...] = m_new
    l_ref[...] = l_new
    acc_ref[...] = acc


class _PageSide:
    def __init__(self, walk, n_stages, past_len, qcol_ref, expand_ref, m_ref, l_ref, acc_ref,
                 begin, finish):
        self.walk, self.n_stages, self.past_len = walk, n_stages, past_len
        self.state = (qcol_ref, expand_ref, m_ref, l_ref, acc_ref)
        self.begin, self.finish = begin, finish

    def _groups(self, j):
        return range(j, self.walk.groups, self.n_stages)

    def pre(self, j):
        walk = self.walk
        if j == 0:
            page, vrows = walk.kbuf.shape[-1], walk.vbuf.shape[-2]
            expand_ref = self.state[1]

            @pl.when(walk.s == 0)
            def _():
                walk.prime()
                tok = lax.broadcasted_iota(jnp.int32, (page, vrows), 0)
                vrow = lax.broadcasted_iota(jnp.int32, (page, vrows), 1)
                expand_ref[...] = jnp.where(vrow // DA_HEADS == tok, 1.0, 0.0).astype(BF16)

            pl.when(walk.seq_first)(self.begin)
        for g in self._groups(j):
            walk.wait(g)

    def post(self, j):
        walk = self.walk
        page = walk.kbuf.shape[-1]
        for g in self._groups(j):
            _fold_pages(walk.kbuf, walk.vbuf, g % 2, walk.first_page_of(walk.s, g) * page,
                        self.past_len, *self.state)
            if walk.groups > 2:
                walk.start_ahead(g)
        if walk.groups == 2 and j == 1:
            for g in range(2):
                walk.start_ahead(g)
        if j == self.n_stages - 1:
            pl.when(walk.seq_last)(self.finish)


def _ffn_inproj_pages_kernel(pt_ref, x_ref, g1_ref, wg_ref, wu_ref, wd_ref, gm_ref, win_ref, gq_ref,
                             gk_ref, qs_ref, kn_ref, vn_ref, kt_hbm, v_hbm,
                             h_ref, zhg_ref, q_ref, kt_ref, kb_ref, v_ref, mo_ref, lo_ref, ao_ref,
                             kbuf, vbuf, sem, qcol_ref, expand_ref, m_ref, l_ref, acc_ref,
                             *, n_steps, n_pages):
    nmap = 2 * DA_HEADS
    page = kbuf.shape[-1]
    walk = _PageWalk(pl.program_id(0), n_steps, pt_ref, kt_hbm, v_hbm, kbuf, vbuf, sem,
                     0, n_pages)

    def begin():
        qcol = _col_bcast(qs_ref[0].astype(F32), page)
        qcol_ref[...] = qcol
        kcol = _col_bcast(kn_ref[0], page)
        m_ref[...] = jnp.sum((qcol * kcol).reshape(nmap, DA_DH, page), axis=1)
        l_ref[...] = jnp.ones_like(l_ref)
        vn = vn_ref[0]
        acc_ref[...] = jnp.concatenate(
            [vn[:, (r // 2) * DA_DV:(r // 2 + 1) * DA_DV] for r in range(nmap)], axis=0)

    def finish():
        mo_ref[0] = m_ref[...]
        lo_ref[0] = l_ref[...]
        ao_ref[0] = acc_ref[...]

    n_stages = len(_ffn_parts(wg_ref.shape[1])) + 1
    side = _PageSide(walk, n_stages, pt_ref.shape[1] * page, qcol_ref, expand_ref, m_ref, l_ref,
                     acc_ref, begin, finish)
    _ffn_inproj_body(x_ref, g1_ref, wg_ref, wu_ref, wd_ref, gm_ref, win_ref, gq_ref, gk_ref,
                     h_ref, zhg_ref, q_ref, kt_ref, kb_ref, v_ref, side)


def _outproj_ffn_pages_kernel(pt_ref, h_ref, ohg_ref, oda_ref, wo_ref, g2_ref, wg_ref, wu_ref, wd_ref,
                              qs_ref, mi_ref, li_ref, ai_ref, lq1_ref, lk1_ref, lq2_ref, lk2_ref,
                              gsub_ref, kt_hbm, v_hbm, y_ref, os_ref,
                              kbuf, vbuf, sem, qcol_ref, expand_ref, m_ref, l_ref, acc_ref,
                              *, n_steps, first_page, lam_init):
    page = kbuf.shape[-1]
    n_pages = pt_ref.shape[1]
    walk = _PageWalk(pl.program_id(0), n_steps, pt_ref, kt_hbm, v_hbm, kbuf, vbuf, sem,
                     first_page, n_pages - first_page)

    def begin():
        qcol_ref[...] = _col_bcast(qs_ref[0].astype(F32), page)
        m_ref[...] = mi_ref[0]
        l_ref[...] = li_ref[0]
        acc_ref[...] = ai_ref[0]

    def finish():
        lam = _lambda(lq1_ref, lk1_ref, lq2_ref, lk2_ref, lam_init)
        acc, l_fin = acc_ref[...], l_ref[...]
        outs = []
        for hd in range(DA_HEADS):
            r1, r2 = 2 * hd, 2 * hd + 1
            outs.append(_diff_combine(acc[r1:r1 + 1], l_fin[r1:r1 + 1], acc[r2:r2 + 1],
                                      l_fin[r2:r2 + 1], lam, gsub_ref[...], lam_init))
        os_ref[0] = jnp.concatenate(outs, axis=-1).astype(BF16)

    n_stages = len(_ffn_parts(wg_ref.shape[1])) + 1
    side = _PageSide(walk, n_stages, n_pages * page, qcol_ref, expand_ref, m_ref, l_ref, acc_ref,
                     begin, finish)
    _outproj_ffn_body(h_ref, ohg_ref, oda_ref, wo_ref, g2_ref, wg_ref, wu_ref, wd_ref, y_ref, side)


def _resident(shape):
    return pl.BlockSpec(shape, lambda *_: (0,) * len(shape), pipeline_mode=pl.Buffered(1))


def _page_scratch(page, vrows, pages_per_step, groups_per_step):
    nmap = 2 * DA_HEADS
    assert pages_per_step % groups_per_step == 0 and groups_per_step % 2 == 0
    group = pages_per_step // groups_per_step
    return [pltpu.VMEM((2, group, nmap, DA_DH, page), F32),
            pltpu.VMEM((2, group, vrows, DA_DV), F32),
            pltpu.SemaphoreType.DMA((2, 2)),
            pltpu.VMEM((DA_WIDTH, page), F32), pltpu.VMEM((page, vrows), BF16),
            pltpu.VMEM((nmap, page), F32), pltpu.VMEM((nmap, page), F32),
            pltpu.VMEM((nmap, DA_DV), F32)]


def _ffn_inproj(x, g1, wg, wu, wd, gm, win, gq, gk, tm, seq, pages=None):
    n, d = x.shape
    dff = wg.shape[1]
    tiles = seq // tm
    rows = lambda w: pl.BlockSpec((tm, w), lambda i, *_: (i, 0))
    in_specs = [rows(d), _resident((1, d)), _resident((d, dff)), _resident((d, dff)),
                _resident((dff, d)), _resident((1, d)), _resident(win.shape),
                _resident((1, DA_WIDTH)), _resident((1, DA_WIDTH))]
    out_specs = [rows(d), rows(HG_COLS), rows(DA_WIDTH),
                 pl.BlockSpec((1, DA_WIDTH, tm), lambda i, *_: (i // tiles, 0, i % tiles)),
                 rows(DA_WIDTH),
                 pl.BlockSpec((tm * DA_HEADS, DA_DV), lambda i, *_: (i, 0))]
    out_shape = [jax.ShapeDtypeStruct((n, d), F32), jax.ShapeDtypeStruct((n, HG_COLS), F32),
                 jax.ShapeDtypeStruct((n, DA_WIDTH), BF16),
                 jax.ShapeDtypeStruct((n // seq, DA_WIDTH, seq), F32),
                 jax.ShapeDtypeStruct((n, DA_WIDTH), BF16),
                 jax.ShapeDtypeStruct((n * DA_HEADS, DA_DV), F32)]
    params = pltpu.CompilerParams(dimension_semantics=("arbitrary",), vmem_limit_bytes=VMEM_LIMIT)
    args = (x, g1, wg, wu, wd, gm, win, gq, gk)
    if pages is None:
        return pl.pallas_call(_ffn_inproj_kernel, grid=(n // tm,), in_specs=in_specs,
                              out_specs=out_specs, out_shape=out_shape, compiler_params=params,
                              name="ffn_inproj")(*args)
    page_table, qs, kn, vn, kt_pages, v_pages, n_pages = pages
    nb = page_table.shape[0]
    nmap = 2 * DA_HEADS
    page, vrows = kt_pages.shape[-1], v_pages.shape[-2]
    per_seq = (n // tm) // nb
    seq_row = lambda w: pl.BlockSpec((1, 1, w), lambda i, *_: (i // per_seq, 0, 0))
    state = lambda w: pl.BlockSpec((1, nmap, w), lambda i, *_: (i // per_seq, 0, 0))
    grid_spec = pltpu.PrefetchScalarGridSpec(
        num_scalar_prefetch=1, grid=(n // tm,),
        in_specs=in_specs + [seq_row(DA_WIDTH), seq_row(DA_WIDTH), seq_row(DA_HEADS * DA_DV),
                             pl.BlockSpec(memory_space=pl.ANY), pl.BlockSpec(memory_space=pl.ANY)],
        out_specs=out_specs + [state(page), state(page), state(DA_DV)],
        scratch_shapes=_page_scratch(page, vrows, n_pages // per_seq, 2))
    out_shape = out_shape + [jax.ShapeDtypeStruct((nb, nmap, page), F32),
                             jax.ShapeDtypeStruct((nb, nmap, page), F32),
                             jax.ShapeDtypeStruct((nb, nmap, DA_DV), F32)]
    return pl.pallas_call(
        functools.partial(_ffn_inproj_pages_kernel, n_steps=n // tm, n_pages=n_pages),
        grid_spec=grid_spec, out_shape=out_shape, compiler_params=params, name="ffn_inproj_pages",
    )(page_table, *args, qs.reshape(nb, 1, DA_WIDTH), kn.reshape(nb, 1, DA_WIDTH),
      vn.reshape(nb, 1, DA_HEADS * DA_DV), kt_pages, v_pages)


def _outproj_ffn(h, ohg, oda, wo, g2, wg, wu, wd, tm, pages=None):
    n, d = h.shape
    dff = wg.shape[1]
    rows = lambda w: pl.BlockSpec((tm, w), lambda i, *_: (i, 0))
    in_specs = [rows(d), rows(HG_WIDTH), rows(DA_WIDTH), _resident(wo.shape), _resident((1, d)),
                _resident((d, dff)), _resident((d, dff)), _resident((dff, d))]
    params = pltpu.CompilerParams(dimension_semantics=("arbitrary",), vmem_limit_bytes=VMEM_LIMIT)
    args = (h, ohg, oda, wo, g2, wg, wu, wd)
    if pages is None:
        return pl.pallas_call(_outproj_ffn_kernel, grid=(n // tm,), in_specs=in_specs,
                              out_specs=rows(d), out_shape=jax.ShapeDtypeStruct((n, d), F32),
                              compiler_params=params, name="outproj_ffn")(*args)
    page_table, qs, m, l, acc, lams, gsub, kt_pages, v_pages, first_page, lam_init = pages
    nb = page_table.shape[0]
    nmap = 2 * DA_HEADS
    page, vrows = kt_pages.shape[-1], v_pages.shape[-2]
    per_seq = (n // tm) // nb
    seq_row = lambda w: pl.BlockSpec((1, 1, w), lambda i, *_: (i // per_seq, 0, 0))
    state = lambda w: pl.BlockSpec((1, nmap, w), lambda i, *_: (i // per_seq, 0, 0))
    small = lambda a: pl.BlockSpec(a.shape, lambda i, *_: (0, 0))
    grid_spec = pltpu.PrefetchScalarGridSpec(
        num_scalar_prefetch=1, grid=(n // tm,),
        in_specs=in_specs + [seq_row(DA_WIDTH), state(page), state(page), state(DA_DV)]
                 + [small(a) for a in lams] + [small(gsub)]
                 + [pl.BlockSpec(memory_space=pl.ANY), pl.BlockSpec(memory_space=pl.ANY)],
        out_specs=[rows(d), seq_row(DA_HEADS * DA_DV)],
        scratch_shapes=_page_scratch(page, vrows, (page_table.shape[1] - first_page) // per_seq, 2))
    y, o_s = pl.pallas_call(
        functools.partial(_outproj_ffn_pages_kernel, n_steps=n // tm, first_page=first_page,
                          lam_init=lam_init),
        grid_spec=grid_spec,
        out_shape=[jax.ShapeDtypeStruct((n, d), F32),
                   jax.ShapeDtypeStruct((nb, 1, DA_HEADS * DA_DV), BF16)],
        compiler_params=params, name="outproj_ffn_pages",
    )(page_table, *args, qs.reshape(nb, 1, DA_WIDTH), m, l, acc, *lams, gsub, kt_pages, v_pages)
    return y, o_s.reshape(nb, DA_HEADS * DA_DV)


HG_CHUNK = 128
HG_BLOCK = 2
HG_GROUP = 4


def _hgrn_gates(z, lb):
    xq = z[:, :HG_WIDTH]
    xf = z[:, HG_WIDTH:2 * HG_WIDTH]
    xi = z[:, 2 * HG_WIDTH:3 * HG_WIDTH]
    xg = z[:, 3 * HG_WIDTH:]
    q = _silu(xq) * (HG_DK ** -0.5)
    f = lb + (1.0 - lb) * _sigmoid(xf)
    return q, 1.0 - f, f, xi, _silu(xg)


def _hgrn_chunk_kernel(z_ref, lbl_ref, gout_ref, o_ref, sfin_ref, st_ref, *, layer):
    c = pl.program_id(1)
    group, n = z_ref.shape[0], z_ref.shape[1]

    @pl.when(c == 0)
    def _():
        st_ref[...] = jnp.zeros_like(st_ref)

    lb = _lower_bound(lbl_ref[...], layer)
    row = lax.broadcasted_iota(jnp.int32, (n, n), 0)
    col = lax.broadcasted_iota(jnp.int32, (n, n), 1)
    tri = jnp.where(row >= col, 1.0, 0.0).astype(BF16)
    trow = lax.broadcasted_iota(jnp.int32, (n, 1), 0)
    spans = []
    w = HG_BLOCK
    while 2 * w <= n:
        span = 2 * w
        spans.append((w, span, (trow % span) >= w,
                      ((row // span) == (col // span)) & ((row % span) >= w) & ((col % span) < w)))
        w = span
    nears = [((row - col) == d) & ((row % HG_BLOCK) >= d) for d in range(HG_BLOCK)]

    for r in range(group):
        q, k, f, v, gate = _hgrn_gates(z_ref[r], lb)
        b = _dot_exact_rhs(tri, jnp.log2(f))
        a = [jnp.zeros((n, n), F32) for _ in range(HG_HEADS)]
        for w, span, right, lvl in spans:
            ref = jnp.concatenate(
                [jnp.broadcast_to(b[p * span + w - 1:p * span + w, :], (span, HG_WIDTH))
                 for p in range(n // span)], axis=0)
            e = jnp.exp2(-jnp.abs(b - ref))
            qw = jnp.where(right, q * e, 0.0).astype(BF16)
            kw = jnp.where(right, 0.0, k * e).astype(BF16)
            for h in range(HG_HEADS):
                hs = slice(h * HG_DK, (h + 1) * HG_DK)
                p = lax.dot_general(qw[:, hs], kw[:, hs], _NT, preferred_element_type=F32)
                a[h] = jnp.where(lvl, p, a[h])
        for d in range(HG_BLOCK):
            kd = pltpu.roll(k, d, axis=0) if d else k
            bd = pltpu.roll(b, d, axis=0) if d else b
            p = q * kd * jnp.exp2(jnp.minimum(b - bd, 0.0))
            for h in range(HG_HEADS):
                hs = slice(h * HG_DK, (h + 1) * HG_DK)
                a[h] = jnp.where(nears[d], jnp.sum(p[:, hs], axis=-1, keepdims=True), a[h])

        b_last = b[n - 1:n, :]
        q_in = (q * jnp.exp2(b)).astype(BF16)
        k_out = (k * jnp.exp2(b_last - b)).astype(BF16)
        carry = jnp.exp2(b_last)
        vb = v.astype(BF16)
        for h in range(HG_HEADS):
            hs = slice(h * HG_DK, (h + 1) * HG_DK)
            st = st_ref[r, h]
            o = (jnp.dot(a[h].astype(BF16), vb[:, hs], preferred_element_type=F32)
                 + lax.dot_general(q_in[:, hs], st.astype(BF16), _NT, preferred_element_type=F32))
            st_new = st * carry[:, hs] + lax.dot_general(vb[:, hs], k_out[:, hs], _TN,
                                                         preferred_element_type=F32)
            st_ref[r, h] = st_new
            o_ref[r, :, hs] = (_rmsnorm(o, gout_ref[...]) * gate[:, hs]).astype(BF16)

    @pl.when(c == pl.num_programs(1) - 1)
    def _():
        for r in range(group):
            for h in range(HG_HEADS):
                sfin_ref[r, h] = st_ref[r, h].T


def _hgrn_prompt(zhg, lb_logits, gout, batch, layer):
    n = zhg.shape[0]
    seq = n // batch
    nc = seq // HG_CHUNK
    group = HG_GROUP if batch % HG_GROUP == 0 else 1
    o, s_fin = pl.pallas_call(
        functools.partial(_hgrn_chunk_kernel, layer=layer),
        grid=(batch // group, nc),
        in_specs=[pl.BlockSpec((group, HG_CHUNK, HG_COLS), lambda g, c: (g, c, 0)),
                  pl.BlockSpec(lb_logits.shape, lambda g, c: (0, 0)),
                  pl.BlockSpec((1, HG_DV), lambda g, c: (0, 0))],
        out_specs=[pl.BlockSpec((group, HG_CHUNK, HG_WIDTH), lambda g, c: (g, c, 0)),
                   pl.BlockSpec((group, HG_HEADS, HG_DK, HG_DV), lambda g, c: (g, 0, 0, 0))],
        out_shape=[jax.ShapeDtypeStruct((batch, seq, HG_WIDTH), BF16),
                   jax.ShapeDtypeStruct((batch, HG_HEADS, HG_DK, HG_DV), F32)],
        scratch_shapes=[pltpu.VMEM((group, HG_HEADS, HG_DV, HG_DK), F32)],
        compiler_params=pltpu.CompilerParams(dimension_semantics=("arbitrary", "arbitrary")),
        name="hgrn_prompt",
    )(zhg.reshape(batch, seq, HG_COLS), lb_logits, gout)
    return o.reshape(n, HG_WIDTH), s_fin


HG_STEP_GROUP = 8


def _hgrn_step_kernel(z_ref, s_ref, lbl_ref, gout_ref, o_ref, snew_ref, *, layer):
    lb = _lower_bound(lbl_ref[...], layer)
    for r in range(z_ref.shape[0]):
        q, k, f, v, gate = _hgrn_gates(z_ref[r], lb)
        outs = []
        for h in range(HG_HEADS):
            hs = slice(h * HG_DK, (h + 1) * HG_DK)
            s_new = _col_bcast(f[:, hs]) * s_ref[r, h] + _col_bcast(k[:, hs]) * v[:, hs]
            snew_ref[r, h] = s_new
            o = jnp.sum(_col_bcast(q[:, hs]) * s_new, axis=0, keepdims=True)
            outs.append(_rmsnorm(o, gout_ref[...]) * gate[:, hs])
        o_ref[r] = jnp.concatenate(outs, axis=-1).astype(BF16)


def _hgrn_sample(zhg, state, lb_logits, gout, layer):
    nb = zhg.shape[0]
    grp = HG_STEP_GROUP if nb % HG_STEP_GROUP == 0 else 1
    return pl.pallas_call(
        functools.partial(_hgrn_step_kernel, layer=layer),
        grid=(nb // grp,),
        in_specs=[pl.BlockSpec((grp, 1, HG_COLS), lambda b: (b, 0, 0)),
                  pl.BlockSpec((grp, HG_HEADS, HG_DK, HG_DV), lambda b: (b, 0, 0, 0)),
                  pl.BlockSpec(lb_logits.shape, lambda b: (0, 0)),
                  pl.BlockSpec((1, HG_DV), lambda b: (0, 0))],
        out_specs=[pl.BlockSpec((grp, 1, HG_WIDTH), lambda b: (b, 0, 0)),
                   pl.BlockSpec((grp, HG_HEADS, HG_DK, HG_DV), lambda b: (b, 0, 0, 0))],
        out_shape=[jax.ShapeDtypeStruct((nb, 1, HG_WIDTH), BF16),
                   jax.ShapeDtypeStruct(state.shape, F32)],
        compiler_params=pltpu.CompilerParams(dimension_semantics=("arbitrary",)),
        name="hgrn_sample",
    )(zhg.reshape(nb, 1, HG_COLS), state, lb_logits, gout)


ATT_TILE = 512


def _head_slope(h):
    return jnp.exp2(jnp.full((1, 1), -8.0 / DA_HEADS, F32) * (h + 1).astype(F32))


def _diff_combine(acc1, l1, acc2, l2, lam, gsub, lam_init):
    out = acc1 / l1 - lam * (acc2 / l2)
    return _rmsnorm(out, gsub) * (1.0 - lam_init)


ATT_VROWS = DA_DV + 16


def _attn_prompt_kernel(q_ref, k_ref, v_ref, lq1_ref, lk1_ref, lq2_ref, lk2_ref, gsub_ref, o_ref,
                        kaug_ref, vt_ref, m_ref, acc_ref, p_ref, alpha_ref, gcol_ref, *, lam_init):
    qi = pl.program_id(2)
    t = q_ref.shape[0]
    nt = kaug_ref.shape[1]
    heads = vt_ref.shape[0]
    units = [(hh, i) for hh in range(heads) for i in range(2)]
    slopes = [_head_slope(pl.program_id(1) * heads + hh) for hh in range(heads)]
    lane = lax.broadcasted_iota(jnp.int32, (t, 2 * DA_DH), 1)
    loc = lax.broadcasted_iota(jnp.int32, (t, 2 * DA_DH), 0)
    loc_lo = (loc % 256).astype(F32)
    loc_hi = (loc - loc % 256).astype(F32)
    slot = [(1 - i) * DA_DH for i in range(2)]

    def augment(x, i, extras):
        out = jnp.zeros_like(x)
        for n, e in enumerate(extras):
            out = jnp.where(lane == slot[i] + n, e, out)
        return jnp.where((lane // DA_DH) == i, x, out)

    @pl.when(qi == 0)
    def _():
        for j in range(nt):
            for u, (hh, i) in enumerate(units):
                k = k_ref[j * t:(j + 1) * t, hh * 2 * DA_DH:(hh + 1) * 2 * DA_DH].astype(F32)
                kaug_ref[u, j] = augment(k, i, (slopes[hh] * loc_lo, slopes[hh] * loc_hi, 1.0, 1.0)
                                         ).astype(BF16)
            for hh in range(heads):
                head = pl.program_id(1) * heads + hh
                vj = v_ref[pl.ds(j * t * DA_HEADS + head, t, stride=DA_HEADS), :]
                vt_ref[hh, j, :DA_DV, :] = vj.T.astype(BF16)
                vt_ref[hh, j, DA_DV:, :] = jnp.ones((ATT_VROWS - DA_DV, t), BF16)
        gcol_ref[...] = _col_bcast(gsub_ref[...], t)

    qts = []
    for hh, i in units:
        q = q_ref[:, hh * 2 * DA_DH:(hh + 1) * 2 * DA_DH].astype(F32)
        qts.append(augment(q, i, (1.0, 1.0, -slopes[hh] * loc_lo, -slopes[hh] * loc_hi)
                           ).T.astype(BF16))
    m_ref[...] = jnp.full_like(m_ref, NEG_INF)
    acc_ref[...] = jnp.zeros_like(acc_ref)
    causal = (lax.broadcasted_iota(jnp.int32, (t, t), 0)
              <= lax.broadcasted_iota(jnp.int32, (t, t), 1))

    def values(ki, slot):
        for u, (hh, _) in enumerate(units):
            acc_ref[u] = (alpha_ref[slot, u] * acc_ref[u]
                          + jnp.dot(vt_ref[hh, ki], p_ref[slot, u], preferred_element_type=F32))

    def step(ki, slot, keep=None, first=False):
        ss = [jnp.dot(kaug_ref[u, ki], qts[u], preferred_element_type=F32)
              for u in range(len(units))]
        if not first:
            values(ki - 1, 1 - slot)
        for u, (hh, _) in enumerate(units):
            shift = -slopes[hh] * ((qi - ki) * t).astype(F32)
            s = ss[u] if keep is None else jnp.where(keep, ss[u], NEG_INF)
            m_old = m_ref[u]
            m_new = jnp.maximum(m_old, jnp.max(s, axis=0, keepdims=True) + shift)
            alpha_ref[slot, u] = jnp.exp(m_old - m_new)
            p_ref[slot, u] = jnp.exp(s - (m_new - shift)).astype(BF16)
            m_ref[u] = m_new

    pl.when(qi == 0)(lambda: step(0, 0, keep=causal, first=True))
    pl.when(qi > 0)(lambda: step(0, 0, first=True))
    inner = qi - 1

    def pair(j, carry):
        step(1 + 2 * j, 1)
        step(2 + 2 * j, 0)
        return carry

    lax.fori_loop(0, inner // 2, pair, 0)
    for parity in range(2):
        @pl.when(jnp.logical_and(qi > 0, qi % 2 == parity))
        def _():
            if parity == 0:
                step(qi - 1, 1)
            step(qi, parity, keep=causal)

        @pl.when(qi % 2 == parity)
        def _():
            values(qi, parity)

    lam = _lambda(lq1_ref, lk1_ref, lq2_ref, lk2_ref, lam_init)
    for hh in range(heads):
        a1, a2 = acc_ref[2 * hh], acc_ref[2 * hh + 1]
        inv1 = 1.0 / a1[DA_DV:DA_DV + 1]
        inv2 = lam / a2[DA_DV:DA_DV + 1]
        out = a1[:DA_DV] * inv1 - a2[:DA_DV] * inv2
        ms = jnp.mean(out * out, axis=0, keepdims=True)
        out = out * (lax.rsqrt(ms + NORM_EPS) * (1.0 - lam_init)) * gcol_ref[...]
        o_ref[:, hh * DA_DV:(hh + 1) * DA_DV] = out.T.astype(BF16)


ATT_HEADS = 1


def _attn_prompt(qb, kb, v, lams, gsub, batch, lam_init):
    n = qb.shape[0]
    seq = n // batch
    nq = seq // ATT_TILE
    units = 2 * ATT_HEADS
    small = lambda a: pl.BlockSpec(a.shape, lambda b, h, i: (0, 0))
    return pl.pallas_call(
        functools.partial(_attn_prompt_kernel, lam_init=lam_init),
        grid=(batch, DA_HEADS // ATT_HEADS, nq),
        in_specs=[pl.BlockSpec((ATT_TILE, ATT_HEADS * 2 * DA_DH), lambda b, h, i: (b * nq + i, h)),
                  pl.BlockSpec((seq, ATT_HEADS * 2 * DA_DH), lambda b, h, i: (b, h)),
                  pl.BlockSpec((seq * DA_HEADS, DA_DV), lambda b, h, i: (b, 0))]
                 + [small(a) for a in lams] + [small(gsub)],
        out_specs=pl.BlockSpec((ATT_TILE, ATT_HEADS * DA_DV), lambda b, h, i: (b * nq + i, h)),
        out_shape=jax.ShapeDtypeStruct((n, DA_HEADS * DA_DV), BF16),
        scratch_shapes=[pltpu.VMEM((units, nq, ATT_TILE, 2 * DA_DH), BF16),
                        pltpu.VMEM((ATT_HEADS, nq, ATT_VROWS, ATT_TILE), BF16),
                        pltpu.VMEM((units, 1, ATT_TILE), F32),
                        pltpu.VMEM((units, ATT_VROWS, ATT_TILE), F32),
                        pltpu.VMEM((2, units, ATT_TILE, ATT_TILE), BF16),
                        pltpu.VMEM((2, units, 1, ATT_TILE), F32),
                        pltpu.VMEM((DA_DV, ATT_TILE), F32)],
        compiler_params=pltpu.CompilerParams(
            dimension_semantics=("arbitrary", "arbitrary", "arbitrary"),
            vmem_limit_bytes=VMEM_LIMIT),
        name="attn_prompt",
    )(qb, kb, v, *lams, gsub)


PROMPT_ROWS = 256


def kernel(x_prompt, x_sample, cache_k, cache_v, state_hgrn, page_table, ffn1_norm, ffn1_w_gate, ffn1_w_up, ffn1_w_down, mix_norm, w_in, hg_lb_logits, hg_out_norm, da_q_norm, da_k_norm, da_lambda_q1, da_lambda_k1, da_lambda_q2, da_lambda_k2, da_subln, w_out, ffn2_norm, ffn2_w_gate, ffn2_w_up, ffn2_w_down):
    batch, seq, d = x_prompt.shape
    nb = x_sample.shape[0]
    depth = ffn1_norm.shape[0]
    nmap = 2 * DA_HEADS
    yp = x_prompt.reshape(batch * seq, d)
    ys = x_sample.reshape(nb, d)
    kt_cache = jnp.transpose(cache_k, (0, 1, 3, 4, 2))
    v_cache = cache_v.reshape(cache_v.shape[:2] + (-1, DA_DV))
    outs = [[] for _ in range(6)]
    for l in range(depth):
        lam_init = 0.8 - 0.6 * math.exp(-0.3 * l)
        bf = lambda w: w[l].astype(BF16)
        w1 = (bf(ffn1_w_gate), bf(ffn1_w_up), bf(ffn1_w_down))
        w2 = (bf(ffn2_w_gate), bf(ffn2_w_up), bf(ffn2_w_down))
        win, wo = bf(w_in), bf(w_out)
        gq = jnp.tile(da_q_norm[l:l + 1], (1, nmap))
        gk = jnp.tile(da_k_norm[l:l + 1], (1, nmap))
        lams = (da_lambda_q1[l:l + 1], da_lambda_k1[l:l + 1], da_lambda_q2[l:l + 1],
                da_lambda_k2[l:l + 1])
        gout, gsub = hg_out_norm[l:l + 1], da_subln[l:l + 1]

        def inproj(x, tm, rows_per_seq, pages=None):
            return _ffn_inproj(x, ffn1_norm[l:l + 1], *w1, mix_norm[l:l + 1], win, gq, gk, tm,
                               rows_per_seq, pages)

        def outproj(h, ohg, oda, tm, pages=None):
            return _outproj_ffn(h, ohg, oda, wo, ffn2_norm[l:l + 1], *w2, tm, pages)

        def keys(kt):
            return jnp.transpose(kt.reshape(kt.shape[0], nmap, DA_DH, kt.shape[2]), (0, 3, 1, 2))

        hs, zhg_s, qs, kt_s, _, vs = inproj(ys, nb, nb)
        ks = keys(kt_s).reshape(nb, DA_WIDTH)
        ohg_s, s_new = _hgrn_sample(zhg_s, state_hgrn[l], hg_lb_logits, gout, l)
        kt_pages, v_pages = kt_cache[l:l + 1], v_cache[l:l + 1]
        first = page_table.shape[1] // PAGE_SHARE_FIRST
        h, zhg, qb, kt, kb, v, m_s, l_s, acc_s = inproj(
            yp, PROMPT_ROWS, seq, (page_table, qs, ks, vs, kt_pages, v_pages, first))
        ohg, s_fin = _hgrn_prompt(zhg, hg_lb_logits, gout, batch, l)
        oda = _attn_prompt(qb, kb, v, lams, gsub, batch, lam_init)
        yp, oda_s = outproj(h, ohg, oda, PROMPT_ROWS, (page_table, qs, m_s, l_s, acc_s, lams, gsub,
                                                       kt_pages, v_pages, first, lam_init))
        outs[0].append(keys(kt))
        outs[1].append(v.reshape(batch, seq, DA_HEADS, DA_DV))
        outs[2].append(s_fin)
        ys = outproj(hs, ohg_s.reshape(nb, HG_WIDTH), oda_s, nb)
        outs[3].append(ks.reshape(nb, 1, nmap, DA_DH))
        outs[4].append(vs.reshape(nb, 1, DA_HEADS, DA_DV))
        outs[5].append(s_new)
    stacked = [jnp.stack(o, axis=0) for o in outs]
    return (yp.reshape(batch, seq, d), ys.reshape(nb, 1, d), *stacked)
```

```python
import functools
import math

import jax
import jax.numpy as jnp
from jax import lax
from jax.experimental import pallas as pl
from jax.experimental.pallas import tpu as pltpu

F32 = jnp.float32
BF16 = jnp.bfloat16

NORM_EPS = 1e-6
NEG_INF = -1e30
LANES = 128
SUBLANES = 8
HG_HEADS = 4
HG_DK = 128
HG_DV = 128
HG_WIDTH = HG_HEADS * HG_DK
DA_HEADS = 4
DA_DH = 64
DA_DV = 128
DA_WIDTH = 2 * DA_HEADS * DA_DH
DA_SCALE = DA_DH ** -0.5
HG_COLS = 4 * HG_WIDTH
VMEM_LIMIT = 60 * 1024 * 1024

_NT = (((1,), (1,)), ((), ()))
_TN = (((0,), (0,)), ((), ()))


def _rmsnorm(x, g):
    ms = jnp.mean(x * x, axis=-1, keepdims=True)
    return x * lax.rsqrt(ms + NORM_EPS) * g


def _sigmoid(x):
    return 0.5 * jnp.tanh(0.5 * x) + 0.5


def _silu(x):
    return x * _sigmoid(x)


def _split3(x):
    hi = x.astype(BF16)
    r1 = x - hi.astype(F32)
    mid = r1.astype(BF16)
    lo = (r1 - mid.astype(F32)).astype(BF16)
    return hi, mid, lo


def _dot_exact_rhs(sel, x):
    hi, mid, lo = _split3(x)
    return (jnp.dot(sel, hi, preferred_element_type=F32)
            + jnp.dot(sel, mid, preferred_element_type=F32)
            + jnp.dot(sel, lo, preferred_element_type=F32))


def _dot_exact_lhs(x, sel):
    hi, mid, lo = _split3(x)
    return (jnp.dot(hi, sel, preferred_element_type=F32)
            + jnp.dot(mid, sel, preferred_element_type=F32)
            + jnp.dot(lo, sel, preferred_element_type=F32))


def _col_bcast(row, width=LANES):
    n = row.shape[1]
    k = 2 * SUBLANES
    first = lax.broadcasted_iota(jnp.int32, (k, n), 0) == 0
    rows = jnp.where(first, jnp.broadcast_to(row, (k, n)), 0.0)
    ones = jnp.ones((k, width), BF16)
    hi, mid, lo = _split3(rows)
    return (lax.dot_general(hi, ones, _TN, preferred_element_type=F32)
            + lax.dot_general(mid, ones, _TN, preferred_element_type=F32)
            + lax.dot_general(lo, ones, _TN, preferred_element_type=F32))


def _group64_rmsnorm(x, g):
    m, w = x.shape
    r = lax.broadcasted_iota(jnp.int32, (LANES, LANES), 0) // DA_DH
    c = lax.broadcasted_iota(jnp.int32, (LANES, LANES), 1) // DA_DH
    same = jnp.where(r == c, 1.0, 0.0).astype(BF16)
    x2 = x * x
    parts = [_dot_exact_lhs(x2[:, i * LANES:(i + 1) * LANES], same) for i in range(w // LANES)]
    ss = jnp.concatenate(parts, axis=-1)
    return x * lax.rsqrt(ss * (1.0 / DA_DH) + NORM_EPS) * g


def _lower_bound(logits, layer):
    m = jnp.max(logits, axis=0, keepdims=True)
    e = jnp.exp(logits - m)
    return jnp.sum(e[:layer + 1], axis=0, keepdims=True) / jnp.sum(e, axis=0, keepdims=True)


def _lambda(lq1_ref, lk1_ref, lq2_ref, lk2_ref, lam_init):
    a = jnp.sum(lq1_ref[...] * lk1_ref[...], axis=-1, keepdims=True)
    b = jnp.sum(lq2_ref[...] * lk2_ref[...], axis=-1, keepdims=True)
    return jnp.exp(a) - jnp.exp(b) + lam_init


FFN_PART = 1024


def _ffn_parts(dff):
    return [(lo, min(lo + FFN_PART, dff)) for lo in range(0, dff, FFN_PART)]


class _NoSide:
    def pre(self, j):
        pass

    def post(self, j):
        pass


def _swiglu_parts(xn, wg_ref, wu_ref, wd_ref, side):
    y = None
    for j, (lo, hi) in enumerate(_ffn_parts(wg_ref.shape[1])):
        side.pre(j)
        g = jnp.dot(xn, wg_ref[:, lo:hi], preferred_element_type=F32)
        u = jnp.dot(xn, wu_ref[:, lo:hi], preferred_element_type=F32)
        a = (_silu(g) * u).astype(BF16)
        part = jnp.dot(a, wd_ref[lo:hi, :], preferred_element_type=F32)
        y = part if y is None else y + part
        side.post(j)
    return y


def _ffn_inproj_body(x_ref, g1_ref, wg_ref, wu_ref, wd_ref, gm_ref, win_ref, gq_ref, gk_ref,
                     h_ref, zhg_ref, q_ref, kt_ref, kb_ref, v_ref, side):
    tm = x_ref.shape[0]
    last = len(_ffn_parts(wg_ref.shape[1]))
    x = x_ref[...]
    y = _swiglu_parts(_rmsnorm(x, g1_ref[...]).astype(BF16), wg_ref, wu_ref, wd_ref, side)
    side.pre(last)
    h = x + 0.5 * y
    h_ref[...] = h
    z = jnp.dot(_rmsnorm(h, gm_ref[...]).astype(BF16), win_ref[...], preferred_element_type=F32)
    zhg_ref[...] = z[:, :HG_COLS]
    dq = z[:, HG_COLS:HG_COLS + DA_WIDTH]
    dk = z[:, HG_COLS + DA_WIDTH:HG_COLS + 2 * DA_WIDTH]
    dv = z[:, HG_COLS + 2 * DA_WIDTH:]
    q_ref[...] = (_group64_rmsnorm(dq, gq_ref[...]) * DA_SCALE).astype(BF16)
    k = _group64_rmsnorm(dk, gk_ref[...])
    kt_ref[0] = k.T
    kb_ref[...] = k.astype(BF16)
    for hd in range(DA_HEADS):
        v_ref[pl.ds(hd, tm, stride=DA_HEADS), :] = dv[:, hd * DA_DV:(hd + 1) * DA_DV]
    side.post(last)


def _outproj_ffn_body(h_ref, ohg_ref, oda_ref, wo_ref, g2_ref, wg_ref, wu_ref, wd_ref, y_ref, side):
    last = len(_ffn_parts(wg_ref.shape[1]))
    h = (h_ref[...]
         + jnp.dot(ohg_ref[...], wo_ref[:HG_WIDTH, :], preferred_element_type=F32)
         + jnp.dot(oda_ref[...], wo_ref[HG_WIDTH:, :], preferred_element_type=F32))
    y = _swiglu_parts(_rmsnorm(h, g2_ref[...]).astype(BF16), wg_ref, wu_ref, wd_ref, side)
    side.pre(last)
    y_ref[...] = h + 0.5 * y
    side.post(last)


def _ffn_inproj_kernel(*refs):
    _ffn_inproj_body(*refs, _NoSide())


def _outproj_ffn_kernel(*refs):
    _outproj_ffn_body(*refs, _NoSide())


PAGE_SHARE_FIRST = 2


class _PageWalk:
    def __init__(self, s, n_steps, pt_ref, kt_hbm, v_hbm, kbuf, vbuf, sem, first_page, n_pages):
        nb = pt_ref.shape[0]
        assert n_steps % nb == 0
        self.steps_per_seq = n_steps // nb
        self.group = kbuf.shape[1]
        assert n_pages % (self.group * self.steps_per_seq) == 0
        self.groups = n_pages // (self.group * self.steps_per_seq)
        assert self.groups % 2 == 0
        self.s, self.n_steps, self.first_page = s, n_steps, first_page
        self.pt_ref, self.kt_hbm, self.v_hbm = pt_ref, kt_hbm, v_hbm
        self.kbuf, self.vbuf, self.sem = kbuf, vbuf, sem
        self.seq_first = (s % self.steps_per_seq) == 0
        self.seq_last = (s % self.steps_per_seq) == self.steps_per_seq - 1

    def first_page_of(self, step, g):
        return self.first_page + ((step % self.steps_per_seq) * self.groups + g) * self.group

    def _copies(self, step, g):
        seq = step // self.steps_per_seq
        page0 = self.first_page_of(step, g)
        slot = g % 2
        copies = []
        for i in range(self.group):
            pg = self.pt_ref[seq, page0 + i]
            copies.append(pltpu.make_async_copy(self.kt_hbm.at[0, pg], self.kbuf.at[slot, i],
                                                self.sem.at[slot, 0]))
            copies.append(pltpu.make_async_copy(self.v_hbm.at[0, pg], self.vbuf.at[slot, i],
                                                self.sem.at[slot, 1]))
        return copies

    def _start(self, step, g):
        for c in self._copies(step, g):
            c.start(priority=1)

    def prime(self):
        for g in range(2):
            self._start(0, g)

    def wait(self, g):
        for c in self._copies(self.s, g):
            c.wait()

    def start_ahead(self, g):
        step = self.s + (g + 2) // self.groups

        @pl.when(step < self.n_steps)
        def _():
            self._start(step, (g + 2) % self.groups)


def _fold_pages(kbuf, vbuf, slot, first_pos, past_len, qcol_ref, expand_ref, m_ref, l_ref, acc_ref):
    nmap = 2 * DA_HEADS
    page = kbuf.shape[-1]
    vrows = vbuf.shape[-2]
    mrow = lax.broadcasted_iota(jnp.int32, (nmap, 1), 0)
    slope = jnp.exp2((-8.0 / DA_HEADS) * ((mrow // 2) + 1).astype(F32))
    own = (lax.broadcasted_iota(jnp.int32, (1, vrows), 1) % DA_HEADS) == (mrow // 2)
    lane = lax.broadcasted_iota(jnp.int32, (1, page), 1)
    qc = qcol_ref[...].reshape(nmap, DA_DH, page)
    scores = []
    for i in range(kbuf.shape[1]):
        s = jnp.sum(kbuf[slot, i] * qc, axis=1)
        pos = first_pos + i * page + lane
        scores.append(s - slope * (past_len - pos).astype(F32))
    m_old = m_ref[...]
    m_new = m_old
    for s in scores:
        m_new = jnp.maximum(m_new, jnp.max(s, axis=-1, keepdims=True))
    alpha = jnp.exp(m_old - m_new)
    l_new = alpha * l_ref[...]
    acc = alpha * acc_ref[...]
    probs = []
    for s in scores:
        p = jnp.exp(s - m_new)
        l_new = l_new + jnp.sum(p, axis=-1, keepdims=True)
        probs.append(p.astype(BF16))
    spread = jnp.dot(jnp.concatenate(probs, axis=0), expand_ref[...],
                     preferred_element_type=F32)
    for i in range(kbuf.shape[1]):
        w = jnp.where(own, spread[i * nmap:(i + 1) * nmap], 0.0).astype(BF16)
        acc = acc + jnp.dot(w, vbuf[slot, i].astype(BF16), preferred_element_type=F32)
    m_ref[...] = m_new
    l_ref[...] = l_new
    acc_ref[...] = acc


class _PageSide:
    def __init__(self, walk, n_stages, past_len, qcol_ref, expand_ref, m_ref, l_ref, acc_ref,
                 begin, finish):
        self.walk, self.n_stages, self.past_len = walk, n_stages, past_len
        self.state = (qcol_ref, expand_ref, m_ref, l_ref, acc_ref)
        self.begin, self.finish = begin, finish

    def _groups(self, j):
        return range(j, self.walk.groups, self.n_stages)

    def pre(self, j):
        walk = self.walk
        if j == 0:
            page, vrows = walk.kbuf.shape[-1], walk.vbuf.shape[-2]
            expand_ref = self.state[1]

            @pl.when(walk.s == 0)
            def _():
                walk.prime()
                tok = lax.broadcasted_iota(jnp.int32, (page, vrows), 0)
                vrow = lax.broadcasted_iota(jnp.int32, (page, vrows), 1)
                expand_ref[...] = jnp.where(vrow // DA_HEADS == tok, 1.0, 0.0).astype(BF16)

            pl.when(walk.seq_first)(self.begin)
        for g in self._groups(j):
            walk.wait(g)

    def post(self, j):
        walk = self.walk
        page = walk.kbuf.shape[-1]
        for g in self._groups(j):
            _fold_pages(walk.kbuf, walk.vbuf, g % 2, walk.first_page_of(walk.s, g) * page,
                        self.past_len, *self.state)
            walk.start_ahead(g)
        if j == self.n_stages - 1:
            pl.when(walk.seq_last)(self.finish)


def _ffn_inproj_pages_kernel(pt_ref, x_ref, g1_ref, wg_ref, wu_ref, wd_ref, gm_ref, win_ref, gq_ref,
                             gk_ref, qs_ref, kn_ref, vn_ref, kt_hbm, v_hbm,
                             h_ref, zhg_ref, q_ref, kt_ref, kb_ref, v_ref, mo_ref, lo_ref, ao_ref,
                             kbuf, vbuf, sem, qcol_ref, expand_ref, m_ref, l_ref, acc_ref,
                             *, n_steps, n_pages):
    nmap = 2 * DA_HEADS
    page = kbuf.shape[-1]
    walk = _PageWalk(pl.program_id(0), n_steps, pt_ref, kt_hbm, v_hbm, kbuf, vbuf, sem,
                     0, n_pages)

    def begin():
        qcol = _col_bcast(qs_ref[0].astype(F32), page)
        qcol_ref[...] = qcol
        kcol = _col_bcast(kn_ref[0], page)
        m_ref[...] = jnp.sum((qcol * kcol).reshape(nmap, DA_DH, page), axis=1)
        l_ref[...] = jnp.ones_like(l_ref)
        vn = vn_ref[0]
        acc_ref[...] = jnp.concatenate(
            [vn[:, (r // 2) * DA_DV:(r // 2 + 1) * DA_DV] for r in range(nmap)], axis=0)

    def finish():
        mo_ref[0] = m_ref[...]
        lo_ref[0] = l_ref[...]
        ao_ref[0] = acc_ref[...]

    n_stages = len(_ffn_parts(wg_ref.shape[1])) + 1
    side = _PageSide(walk, n_stages, pt_ref.shape[1] * page, qcol_ref, expand_ref, m_ref, l_ref,
                     acc_ref, begin, finish)
    _ffn_inproj_body(x_ref, g1_ref, wg_ref, wu_ref, wd_ref, gm_ref, win_ref, gq_ref, gk_ref,
                     h_ref, zhg_ref, q_ref, kt_ref, kb_ref, v_ref, side)


def _outproj_ffn_pages_kernel(pt_ref, h_ref, ohg_ref, oda_ref, wo_ref, g2_ref, wg_ref, wu_ref, wd_ref,
                              qs_ref, mi_ref, li_ref, ai_ref, lq1_ref, lk1_ref, lq2_ref, lk2_ref,
                              gsub_ref, kt_hbm, v_hbm, y_ref, os_ref,
                              kbuf, vbuf, sem, qcol_ref, expand_ref, m_ref, l_ref, acc_ref,
                              *, n_steps, first_page, lam_init):
    page = kbuf.shape[-1]
    n_pages = pt_ref.shape[1]
    walk = _PageWalk(pl.program_id(0), n_steps, pt_ref, kt_hbm, v_hbm, kbuf, vbuf, sem,
                     first_page, n_pages - first_page)

    def begin():
        qcol_ref[...] = _col_bcast(qs_ref[0].astype(F32), page)
        m_ref[...] = mi_ref[0]
        l_ref[...] = li_ref[0]
        acc_ref[...] = ai_ref[0]

    def finish():
        lam = _lambda(lq1_ref, lk1_ref, lq2_ref, lk2_ref, lam_init)
        acc, l_fin = acc_ref[...], l_ref[...]
        outs = []
        for hd in range(DA_HEADS):
            r1, r2 = 2 * hd, 2 * hd + 1
            outs.append(_diff_combine(acc[r1:r1 + 1], l_fin[r1:r1 + 1], acc[r2:r2 + 1],
                                      l_fin[r2:r2 + 1], lam, gsub_ref[...], lam_init))
        os_ref[0] = jnp.concatenate(outs, axis=-1).astype(BF16)

    n_stages = len(_ffn_parts(wg_ref.shape[1])) + 1
    side = _PageSide(walk, n_stages, n_pages * page, qcol_ref, expand_ref, m_ref, l_ref, acc_ref,
                     begin, finish)
    _outproj_ffn_body(h_ref, ohg_ref, oda_ref, wo_ref, g2_ref, wg_ref, wu_ref, wd_ref, y_ref, side)


def _resident(shape):
    return pl.BlockSpec(shape, lambda *_: (0,) * len(shape), pipeline_mode=pl.Buffered(1))


def _page_scratch(page, vrows, pages_per_step, groups_per_step):
    nmap = 2 * DA_HEADS
    assert pages_per_step % groups_per_step == 0 and groups_per_step % 2 == 0
    group = pages_per_step // groups_per_step
    return [pltpu.VMEM((2, group, nmap, DA_DH, page), F32),
            pltpu.VMEM((2, group, vrows, DA_DV), F32),
            pltpu.SemaphoreType.DMA((2, 2)),
            pltpu.VMEM((DA_WIDTH, page), F32), pltpu.VMEM((page, vrows), BF16),
            pltpu.VMEM((nmap, page), F32), pltpu.VMEM((nmap, page), F32),
            pltpu.VMEM((nmap, DA_DV), F32)]


def _ffn_inproj(x, g1, wg, wu, wd, gm, win, gq, gk, tm, seq, pages=None):
    n, d = x.shape
    dff = wg.shape[1]
    tiles = seq // tm
    rows = lambda w: pl.BlockSpec((tm, w), lambda i, *_: (i, 0))
    in_specs = [rows(d), _resident((1, d)), _resident((d, dff)), _resident((d, dff)),
                _resident((dff, d)), _resident((1, d)), _resident(win.shape),
                _resident((1, DA_WIDTH)), _resident((1, DA_WIDTH))]
    out_specs = [rows(d), rows(HG_COLS), rows(DA_WIDTH),
                 pl.BlockSpec((1, DA_WIDTH, tm), lambda i, *_: (i // tiles, 0, i % tiles)),
                 rows(DA_WIDTH),
                 pl.BlockSpec((tm * DA_HEADS, DA_DV), lambda i, *_: (i, 0))]
    out_shape = [jax.ShapeDtypeStruct((n, d), F32), jax.ShapeDtypeStruct((n, HG_COLS), F32),
                 jax.ShapeDtypeStruct((n, DA_WIDTH), BF16),
                 jax.ShapeDtypeStruct((n // seq, DA_WIDTH, seq), F32),
                 jax.ShapeDtypeStruct((n, DA_WIDTH), BF16),
                 jax.ShapeDtypeStruct((n * DA_HEADS, DA_DV), F32)]
    params = pltpu.CompilerParams(dimension_semantics=("arbitrary",), vmem_limit_bytes=VMEM_LIMIT)
    args = (x, g1, wg, wu, wd, gm, win, gq, gk)
    if pages is None:
        return pl.pallas_call(_ffn_inproj_kernel, grid=(n // tm,), in_specs=in_specs,
                              out_specs=out_specs, out_shape=out_shape, compiler_params=params,
                              name="ffn_inproj")(*args)
    page_table, qs, kn, vn, kt_pages, v_pages, n_pages = pages
    nb = page_table.shape[0]
    nmap = 2 * DA_HEADS
    page, vrows = kt_pages.shape[-1], v_pages.shape[-2]
    per_seq = (n // tm) // nb
    seq_row = lambda w: pl.BlockSpec((1, 1, w), lambda i, *_: (i // per_seq, 0, 0))
    state = lambda w: pl.BlockSpec((1, nmap, w), lambda i, *_: (i // per_seq, 0, 0))
    grid_spec = pltpu.PrefetchScalarGridSpec(
        num_scalar_prefetch=1, grid=(n // tm,),
        in_specs=in_specs + [seq_row(DA_WIDTH), seq_row(DA_WIDTH), seq_row(DA_HEADS * DA_DV),
                             pl.BlockSpec(memory_space=pl.ANY), pl.BlockSpec(memory_space=pl.ANY)],
        out_specs=out_specs + [state(page), state(page), state(DA_DV)],
        scratch_shapes=_page_scratch(page, vrows, n_pages // per_seq, 2))
    out_shape = out_shape + [jax.ShapeDtypeStruct((nb, nmap, page), F32),
                             jax.ShapeDtypeStruct((nb, nmap, page), F32),
                             jax.ShapeDtypeStruct((nb, nmap, DA_DV), F32)]
    return pl.pallas_call(
        functools.partial(_ffn_inproj_pages_kernel, n_steps=n // tm, n_pages=n_pages),
        grid_spec=grid_spec, out_shape=out_shape, compiler_params=params, name="ffn_inproj_pages",
    )(page_table, *args, qs.reshape(nb, 1, DA_WIDTH), kn.reshape(nb, 1, DA_WIDTH),
      vn.reshape(nb, 1, DA_HEADS * DA_DV), kt_pages, v_pages)


def _outproj_ffn(h, ohg, oda, wo, g2, wg, wu, wd, tm, pages=None):
    n, d = h.shape
    dff = wg.shape[1]
    rows = lambda w: pl.BlockSpec((tm, w), lambda i, *_: (i, 0))
    in_specs = [rows(d), rows(HG_WIDTH), rows(DA_WIDTH), _resident(wo.shape), _resident((1, d)),
                _resident((d, dff)), _resident((d, dff)), _resident((dff, d))]
    params = pltpu.CompilerParams(dimension_semantics=("arbitrary",), vmem_limit_bytes=VMEM_LIMIT)
    args = (h, ohg, oda, wo, g2, wg, wu, wd)
    if pages is None:
        return pl.pallas_call(_outproj_ffn_kernel, grid=(n // tm,), in_specs=in_specs,
                              out_specs=rows(d), out_shape=jax.ShapeDtypeStruct((n, d), F32),
                              compiler_params=params, name="outproj_ffn")(*args)
    page_table, qs, m, l, acc, lams, gsub, kt_pages, v_pages, first_page, lam_init = pages
    nb = page_table.shape[0]
    nmap = 2 * DA_HEADS
    page, vrows = kt_pages.shape[-1], v_pages.shape[-2]
    per_seq = (n // tm) // nb
    seq_row = lambda w: pl.BlockSpec((1, 1, w), lambda i, *_: (i // per_seq, 0, 0))
    state = lambda w: pl.BlockSpec((1, nmap, w), lambda i, *_: (i // per_seq, 0, 0))
    small = lambda a: pl.BlockSpec(a.shape, lambda i, *_: (0, 0))
    grid_spec = pltpu.PrefetchScalarGridSpec(
        num_scalar_prefetch=1, grid=(n // tm,),
        in_specs=in_specs + [seq_row(DA_WIDTH), state(page), state(page), state(DA_DV)]
                 + [small(a) for a in lams] + [small(gsub)]
                 + [pl.BlockSpec(memory_space=pl.ANY), pl.BlockSpec(memory_space=pl.ANY)],
        out_specs=[rows(d), seq_row(DA_HEADS * DA_DV)],
        scratch_shapes=_page_scratch(page, vrows, (page_table.shape[1] - first_page) // per_seq, 2))
    y, o_s = pl.pallas_call(
        functools.partial(_outproj_ffn_pages_kernel, n_steps=n // tm, first_page=first_page,
                          lam_init=lam_init),
        grid_spec=grid_spec,
        out_shape=[jax.ShapeDtypeStruct((n, d), F32),
                   jax.ShapeDtypeStruct((nb, 1, DA_HEADS * DA_DV), BF16)],
        compiler_params=params, name="outproj_ffn_pages",
    )(page_table, *args, qs.reshape(nb, 1, DA_WIDTH), m, l, acc, *lams, gsub, kt_pages, v_pages)
    return y, o_s.reshape(nb, DA_HEADS * DA_DV)


HG_CHUNK = 128
HG_BLOCK = 2
HG_GROUP = 4


def _hgrn_gates(z, lb):
    xq = z[:, :HG_WIDTH]
    xf = z[:, HG_WIDTH:2 * HG_WIDTH]
    xi = z[:, 2 * HG_WIDTH:3 * HG_WIDTH]
    xg = z[:, 3 * HG_WIDTH:]
    q = _silu(xq) * (HG_DK ** -0.5)
    f = lb + (1.0 - lb) * _sigmoid(xf)
    return q, 1.0 - f, f, xi, _silu(xg)


def _hgrn_chunk_kernel(z_ref, lbl_ref, gout_ref, o_ref, sfin_ref, st_ref, *, layer):
    c = pl.program_id(1)
    group, n = z_ref.shape[0], z_ref.shape[1]

    @pl.when(c == 0)
    def _():
        st_ref[...] = jnp.zeros_like(st_ref)

    lb = _lower_bound(lbl_ref[...], layer)
    row = lax.broadcasted_iota(jnp.int32, (n, n), 0)
    col = lax.broadcasted_iota(jnp.int32, (n, n), 1)
    tri = jnp.where(row >= col, 1.0, 0.0).astype(BF16)
    trow = lax.broadcasted_iota(jnp.int32, (n, 1), 0)
    spans = []
    w = HG_BLOCK
    while 2 * w <= n:
        span = 2 * w
        spans.append((w, span, (trow % span) >= w,
                      ((row // span) == (col // span)) & ((row % span) >= w) & ((col % span) < w)))
        w = span
    nears = [((row - col) == d) & ((row % HG_BLOCK) >= d) for d in range(HG_BLOCK)]

    for r in range(group):
        q, k, f, v, gate = _hgrn_gates(z_ref[r], lb)
        b = _dot_exact_rhs(tri, jnp.log2(f))
        a = [jnp.zeros((n, n), F32) for _ in range(HG_HEADS)]
        for w, span, right, lvl in spans:
            ref = jnp.concatenate(
                [jnp.broadcast_to(b[p * span + w - 1:p * span + w, :], (span, HG_WIDTH))
                 for p in range(n // span)], axis=0)
            e = jnp.exp2(-jnp.abs(b - ref))
            qw = jnp.where(right, q * e, 0.0).astype(BF16)
            kw = jnp.where(right, 0.0, k * e).astype(BF16)
            for h in range(HG_HEADS):
                hs = slice(h * HG_DK, (h + 1) * HG_DK)
                p = lax.dot_general(qw[:, hs], kw[:, hs], _NT, preferred_element_type=F32)
                a[h] = jnp.where(lvl, p, a[h])
        for d in range(HG_BLOCK):
            kd = pltpu.roll(k, d, axis=0) if d else k
            bd = pltpu.roll(b, d, axis=0) if d else b
            p = q * kd * jnp.exp2(jnp.minimum(b - bd, 0.0))
            for h in range(HG_HEADS):
                hs = slice(h * HG_DK, (h + 1) * HG_DK)
                a[h] = jnp.where(nears[d], jnp.sum(p[:, hs], axis=-1, keepdims=True), a[h])

        b_last = b[n - 1:n, :]
        q_in = (q * jnp.exp2(b)).astype(BF16)
        k_out = (k * jnp.exp2(b_last - b)).astype(BF16)
        carry = jnp.exp2(b_last)
        vb = v.astype(BF16)
        for h in range(HG_HEADS):
            hs = slice(h * HG_DK, (h + 1) * HG_DK)
            st = st_ref[r, h]
            o = (jnp.dot(a[h].astype(BF16), vb[:, hs], preferred_element_type=F32)
                 + lax.dot_general(q_in[:, hs], st.astype(BF16), _NT, preferred_element_type=F32))
            st_new = st * carry[:, hs] + lax.dot_general(vb[:, hs], k_out[:, hs], _TN,
                                                         preferred_element_type=F32)
            st_ref[r, h] = st_new
            o_ref[r, :, hs] = (_rmsnorm(o, gout_ref[...]) * gate[:, hs]).astype(BF16)

    @pl.when(c == pl.num_programs(1) - 1)
    def _():
        for r in range(group):
            for h in range(HG_HEADS):
                sfin_ref[r, h] = st_ref[r, h].T


def _hgrn_prompt(zhg, lb_logits, gout, batch, layer):
    n = zhg.shape[0]
    seq = n // batch
    nc = seq // HG_CHUNK
    group = HG_GROUP if batch % HG_GROUP == 0 else 1
    o, s_fin = pl.pallas_call(
        functools.partial(_hgrn_chunk_kernel, layer=layer),
        grid=(batch // group, nc),
        in_specs=[pl.BlockSpec((group, HG_CHUNK, HG_COLS), lambda g, c: (g, c, 0)),
                  pl.BlockSpec(lb_logits.shape, lambda g, c: (0, 0)),
                  pl.BlockSpec((1, HG_DV), lambda g, c: (0, 0))],
        out_specs=[pl.BlockSpec((group, HG_CHUNK, HG_WIDTH), lambda g, c: (g, c, 0)),
                   pl.BlockSpec((group, HG_HEADS, HG_DK, HG_DV), lambda g, c: (g, 0, 0, 0))],
        out_shape=[jax.ShapeDtypeStruct((batch, seq, HG_WIDTH), BF16),
                   jax.ShapeDtypeStruct((batch, HG_HEADS, HG_DK, HG_DV), F32)],
        scratch_shapes=[pltpu.VMEM((group, HG_HEADS, HG_DV, HG_DK), F32)],
        compiler_params=pltpu.CompilerParams(dimension_semantics=("arbitrary", "arbitrary")),
        name="hgrn_prompt",
    )(zhg.reshape(batch, seq, HG_COLS), lb_logits, gout)
    return o.reshape(n, HG_WIDTH), s_fin


HG_STEP_GROUP = 8


def _hgrn_step_kernel(z_ref, s_ref, lbl_ref, gout_ref, o_ref, snew_ref, *, layer):
    lb = _lower_bound(lbl_ref[...], layer)
    for r in range(z_ref.shape[0]):
        q, k, f, v, gate = _hgrn_gates(z_ref[r], lb)
        outs = []
        for h in range(HG_HEADS):
            hs = slice(h * HG_DK, (h + 1) * HG_DK)
            s_new = _col_bcast(f[:, hs]) * s_ref[r, h] + _col_bcast(k[:, hs]) * v[:, hs]
            snew_ref[r, h] = s_new
            o = jnp.sum(_col_bcast(q[:, hs]) * s_new, axis=0, keepdims=True)
            outs.append(_rmsnorm(o, gout_ref[...]) * gate[:, hs])
        o_ref[r] = jnp.concatenate(outs, axis=-1).astype(BF16)


def _hgrn_sample(zhg, state, lb_logits, gout, layer):
    nb = zhg.shape[0]
    grp = HG_STEP_GROUP if nb % HG_STEP_GROUP == 0 else 1
    return pl.pallas_call(
        functools.partial(_hgrn_step_kernel, layer=layer),
        grid=(nb // grp,),
        in_specs=[pl.BlockSpec((grp, 1, HG_COLS), lambda b: (b, 0, 0)),
                  pl.BlockSpec((grp, HG_HEADS, HG_DK, HG_DV), lambda b: (b, 0, 0, 0)),
                  pl.BlockSpec(lb_logits.shape, lambda b: (0, 0)),
                  pl.BlockSpec((1, HG_DV), lambda b: (0, 0))],
        out_specs=[pl.BlockSpec((grp, 1, HG_WIDTH), lambda b: (b, 0, 0)),
                   pl.BlockSpec((grp, HG_HEADS, HG_DK, HG_DV), lambda b: (b, 0, 0, 0))],
        out_shape=[jax.ShapeDtypeStruct((nb, 1, HG_WIDTH), BF16),
                   jax.ShapeDtypeStruct(state.shape, F32)],
        compiler_params=pltpu.CompilerParams(dimension_semantics=("arbitrary",)),
        name="hgrn_sample",
    )(zhg.reshape(nb, 1, HG_COLS), state, lb_logits, gout)


ATT_TILE = 512


def _head_slope(h):
    return jnp.exp2(jnp.full((1, 1), -8.0 / DA_HEADS, F32) * (h + 1).astype(F32))


def _diff_combine(acc1, l1, acc2, l2, lam, gsub, lam_init):
    out = acc1 / l1 - lam * (acc2 / l2)
    return _rmsnorm(out, gsub) * (1.0 - lam_init)


ATT_VROWS = DA_DV + 16


def _attn_prompt_kernel(q_ref, k_ref, v_ref, lq1_ref, lk1_ref, lq2_ref, lk2_ref, gsub_ref, o_ref,
                        kaug_ref, vt_ref, m_ref, acc_ref, p_ref, alpha_ref, gcol_ref, *, lam_init):
    qi = pl.program_id(2)
    t = q_ref.shape[0]
    nt = kaug_ref.shape[1]
    heads = vt_ref.shape[0]
    units = [(hh, i) for hh in range(heads) for i in range(2)]
    slopes = [_head_slope(pl.program_id(1) * heads + hh) for hh in range(heads)]
    lane = lax.broadcasted_iota(jnp.int32, (t, 2 * DA_DH), 1)
    loc = lax.broadcasted_iota(jnp.int32, (t, 2 * DA_DH), 0)
    loc_lo = (loc % 256).astype(F32)
    loc_hi = (loc - loc % 256).astype(F32)
    slot = [(1 - i) * DA_DH for i in range(2)]

    def augment(x, i, extras):
        out = jnp.zeros_like(x)
        for n, e in enumerate(extras):
            out = jnp.where(lane == slot[i] + n, e, out)
        return jnp.where((lane // DA_DH) == i, x, out)

    @pl.when(qi == 0)
    def _():
        for j in range(nt):
            for u, (hh, i) in enumerate(units):
                k = k_ref[j * t:(j + 1) * t, hh * 2 * DA_DH:(hh + 1) * 2 * DA_DH].astype(F32)
                kaug_ref[u, j] = augment(k, i, (slopes[hh] * loc_lo, slopes[hh] * loc_hi, 1.0, 1.0)
                                         ).astype(BF16)
            for hh in range(heads):
                head = pl.program_id(1) * heads + hh
                vj = v_ref[pl.ds(j * t * DA_HEADS + head, t, stride=DA_HEADS), :]
                vt_ref[hh, j, :DA_DV, :] = vj.T.astype(BF16)
                vt_ref[hh, j, DA_DV:, :] = jnp.ones((ATT_VROWS - DA_DV, t), BF16)
        gcol_ref[...] = _col_bcast(gsub_ref[...], t)

    qts = []
    for hh, i in units:
        q = q_ref[:, hh * 2 * DA_DH:(hh + 1) * 2 * DA_DH].astype(F32)
        qts.append(augment(q, i, (1.0, 1.0, -slopes[hh] * loc_lo, -slopes[hh] * loc_hi)
                           ).T.astype(BF16))
    m_ref[...] = jnp.full_like(m_ref, NEG_INF)
    acc_ref[...] = jnp.zeros_like(acc_ref)
    causal = (lax.broadcasted_iota(jnp.int32, (t, t), 0)
              <= lax.broadcasted_iota(jnp.int32, (t, t), 1))

    def values(ki, slot):
        for u, (hh, _) in enumerate(units):
            acc_ref[u] = (alpha_ref[slot, u] * acc_ref[u]
                          + jnp.dot(vt_ref[hh, ki], p_ref[slot, u], preferred_element_type=F32))

    def step(ki, slot, keep=None, first=False):
        ss = [jnp.dot(kaug_ref[u, ki], qts[u], preferred_element_type=F32)
              for u in range(len(units))]
        if not first:
            values(ki - 1, 1 - slot)
        for u, (hh, _) in enumerate(units):
            shift = -slopes[hh] * ((qi - ki) * t).astype(F32)
            s = ss[u] if keep is None else jnp.where(keep, ss[u], NEG_INF)
            m_old = m_ref[u]
            m_new = jnp.maximum(m_old, jnp.max(s, axis=0, keepdims=True) + shift)
            alpha_ref[slot, u] = jnp.exp(m_old - m_new)
            p_ref[slot, u] = jnp.exp(s - (m_new - shift)).astype(BF16)
            m_ref[u] = m_new

    pl.when(qi == 0)(lambda: step(0, 0, keep=causal, first=True))
    pl.when(qi > 0)(lambda: step(0, 0, first=True))
    inner = qi - 1

    def pair(j, carry):
        step(1 + 2 * j, 1)
        step(2 + 2 * j, 0)
        return carry

    lax.fori_loop(0, inner // 2, pair, 0)
    for parity in range(2):
        @pl.when(jnp.logical_and(qi > 0, qi % 2 == parity))
        def _():
            if parity == 0:
                step(qi - 1, 1)
            step(qi, parity, keep=causal)

        @pl.when(qi % 2 == parity)
        def _():
            values(qi, parity)

    lam = _lambda(lq1_ref, lk1_ref, lq2_ref, lk2_ref, lam_init)
    for hh in range(heads):
        a1, a2 = acc_ref[2 * hh], acc_ref[2 * hh + 1]
        inv1 = 1.0 / a1[DA_DV:DA_DV + 1]
        inv2 = lam / a2[DA_DV:DA_DV + 1]
        out = a1[:DA_DV] * inv1 - a2[:DA_DV] * inv2
        ms = jnp.mean(out * out, axis=0, keepdims=True)
        out = out * (lax.rsqrt(ms + NORM_EPS) * (1.0 - lam_init)) * gcol_ref[...]
        o_ref[:, hh * DA_DV:(hh + 1) * DA_DV] = out.T.astype(BF16)


ATT_HEADS = 1


def _attn_prompt(qb, kb, v, lams, gsub, batch, lam_init):
    n = qb.shape[0]
    seq = n // batch
    nq = seq // ATT_TILE
    units = 2 * ATT_HEADS
    small = lambda a: pl.BlockSpec(a.shape, lambda b, h, i: (0, 0))
    return pl.pallas_call(
        functools.partial(_attn_prompt_kernel, lam_init=lam_init),
        grid=(batch, DA_HEADS // ATT_HEADS, nq),
        in_specs=[pl.BlockSpec((ATT_TILE, ATT_HEADS * 2 * DA_DH), lambda b, h, i: (b * nq + i, h)),
                  pl.BlockSpec((seq, ATT_HEADS * 2 * DA_DH), lambda b, h, i: (b, h)),
                  pl.BlockSpec((seq * DA_HEADS, DA_DV), lambda b, h, i: (b, 0))]
                 + [small(a) for a in lams] + [small(gsub)],
        out_specs=pl.BlockSpec((ATT_TILE, ATT_HEADS * DA_DV), lambda b, h, i: (b * nq + i, h)),
        out_shape=jax.ShapeDtypeStruct((n, DA_HEADS * DA_DV), BF16),
        scratch_shapes=[pltpu.VMEM((units, nq, ATT_TILE, 2 * DA_DH), BF16),
                        pltpu.VMEM((ATT_HEADS, nq, ATT_VROWS, ATT_TILE), BF16),
                        pltpu.VMEM((units, 1, ATT_TILE), F32),
                        pltpu.VMEM((units, ATT_VROWS, ATT_TILE), F32),
                        pltpu.VMEM((2, units, ATT_TILE, ATT_TILE), BF16),
                        pltpu.VMEM((2, units, 1, ATT_TILE), F32),
                        pltpu.VMEM((DA_DV, ATT_TILE), F32)],
        compiler_params=pltpu.CompilerParams(
            dimension_semantics=("arbitrary", "arbitrary", "arbitrary"),
            vmem_limit_bytes=VMEM_LIMIT),
        name="attn_prompt",
    )(qb, kb, v, *lams, gsub)


PROMPT_ROWS = 256


def kernel(x_prompt, x_sample, cache_k, cache_v, state_hgrn, page_table, ffn1_norm, ffn1_w_gate, ffn1_w_up, ffn1_w_down, mix_norm, w_in, hg_lb_logits, hg_out_norm, da_q_norm, da_k_norm, da_lambda_q1, da_lambda_k1, da_lambda_q2, da_lambda_k2, da_subln, w_out, ffn2_norm, ffn2_w_gate, ffn2_w_up, ffn2_w_down):
    batch, seq, d = x_prompt.shape
    nb = x_sample.shape[0]
    depth = ffn1_norm.shape[0]
    nmap = 2 * DA_HEADS
    yp = x_prompt.reshape(batch * seq, d)
    ys = x_sample.reshape(nb, d)
    kt_cache = jnp.transpose(cache_k, (0, 1, 3, 4, 2))
    v_cache = cache_v.reshape(cache_v.shape[:2] + (-1, DA_DV))
    outs = [[] for _ in range(6)]
    for l in range(depth):
        lam_init = 0.8 - 0.6 * math.exp(-0.3 * l)
        bf = lambda w: w[l].astype(BF16)
        w1 = (bf(ffn1_w_gate), bf(ffn1_w_up), bf(ffn1_w_down))
        w2 = (bf(ffn2_w_gate), bf(ffn2_w_up), bf(ffn2_w_down))
        win, wo = bf(w_in), bf(w_out)
        gq = jnp.tile(da_q_norm[l:l + 1], (1, nmap))
        gk = jnp.tile(da_k_norm[l:l + 1], (1, nmap))
        lams = (da_lambda_q1[l:l + 1], da_lambda_k1[l:l + 1], da_lambda_q2[l:l + 1],
                da_lambda_k2[l:l + 1])
        gout, gsub = hg_out_norm[l:l + 1], da_subln[l:l + 1]

        def inproj(x, tm, rows_per_seq, pages=None):
            return _ffn_inproj(x, ffn1_norm[l:l + 1], *w1, mix_norm[l:l + 1], win, gq, gk, tm,
                               rows_per_seq, pages)

        def outproj(h, ohg, oda, tm, pages=None):
            return _outproj_ffn(h, ohg, oda, wo, ffn2_norm[l:l + 1], *w2, tm, pages)

        def keys(kt):
            return jnp.transpose(kt.reshape(kt.shape[0], nmap, DA_DH, kt.shape[2]), (0, 3, 1, 2))

        hs, zhg_s, qs, kt_s, _, vs = inproj(ys, nb, nb)
        ks = keys(kt_s).reshape(nb, DA_WIDTH)
        ohg_s, s_new = _hgrn_sample(zhg_s, state_hgrn[l], hg_lb_logits, gout, l)
        kt_pages, v_pages = kt_cache[l:l + 1], v_cache[l:l + 1]
        first = page_table.shape[1] // PAGE_SHARE_FIRST
        h, zhg, qb, kt, kb, v, m_s, l_s, acc_s = inproj(
            yp, PROMPT_ROWS, seq, (page_table, qs, ks, vs, kt_pages, v_pages, first))
        ohg, s_fin = _hgrn_prompt(zhg, hg_lb_logits, gout, batch, l)
        oda = _attn_prompt(qb, kb, v, lams, gsub, batch, lam_init)
        yp, oda_s = outproj(h, ohg, oda, PROMPT_ROWS, (page_table, qs, m_s, l_s, acc_s, lams, gsub,
                                                       kt_pages, v_pages, first, lam_init))
        outs[0].append(keys(kt))
        outs[1].append(v.reshape(batch, seq, DA_HEADS, DA_DV))
        outs[2].append(s_fin)
        ys = outproj(hs, ohg_s.reshape(nb, HG_WIDTH), oda_s, nb)
        outs[3].append(ks.reshape(nb, 1, nmap, DA_DH))
        outs[4].append(vs.reshape(nb, 1, DA_HEADS, DA_DV))
        outs[5].append(s_new)
    stacked = [jnp.stack(o, axis=0) for o in outs]
    return (yp.reshape(batch, seq, d), ys.reshape(nb, 1, d), *stacked)
```

```python
import functools
import math

import jax
import jax.numpy as jnp
from jax import lax
from jax.experimental import pallas as pl
from jax.experimental.pallas import tpu as pltpu

F32 = jnp.float32
BF16 = jnp.bfloat16

NORM_EPS = 1e-6
NEG_INF = -1e30
LANES = 128
SUBLANES = 8
HG_HEADS = 4
HG_DK = 128
HG_DV = 128
HG_WIDTH = HG_HEADS * HG_DK
DA_HEADS = 4
DA_DH = 64
DA_DV = 128
DA_WIDTH = 2 * DA_HEADS * DA_DH
DA_SCALE = DA_DH ** -0.5
HG_COLS = 4 * HG_WIDTH
VMEM_LIMIT = 60 * 1024 * 1024

_NT = (((1,), (1,)), ((), ()))
_TN = (((0,), (0,)), ((), ()))


def _rmsnorm(x, g):
    ms = jnp.mean(x * x, axis=-1, keepdims=True)
    return x * lax.rsqrt(ms + NORM_EPS) * g


def _sigmoid(x):
    return 0.5 * jnp.tanh(0.5 * x) + 0.5


def _silu(x):
    return x * _sigmoid(x)


def _split3(x):
    hi = x.astype(BF16)
    r1 = x - hi.astype(F32)
    mid = r1.astype(BF16)
    lo = (r1 - mid.astype(F32)).astype(BF16)
    return hi, mid, lo


def _dot_exact_rhs(sel, x):
    hi, mid, lo = _split3(x)
    return (jnp.dot(sel, hi, preferred_element_type=F32)
            + jnp.dot(sel, mid, preferred_element_type=F32)
            + jnp.dot(sel, lo, preferred_element_type=F32))


def _dot_exact_lhs(x, sel):
    hi, mid, lo = _split3(x)
    return (jnp.dot(hi, sel, preferred_element_type=F32)
            + jnp.dot(mid, sel, preferred_element_type=F32)
            + jnp.dot(lo, sel, preferred_element_type=F32))


def _col_bcast(row, width=LANES):
    n = row.shape[1]
    k = 2 * SUBLANES
    first = lax.broadcasted_iota(jnp.int32, (k, n), 0) == 0
    rows = jnp.where(first, jnp.broadcast_to(row, (k, n)), 0.0)
    ones = jnp.ones((k, width), BF16)
    hi, mid, lo = _split3(rows)
    return (lax.dot_general(hi, ones, _TN, preferred_element_type=F32)
            + lax.dot_general(mid, ones, _TN, preferred_element_type=F32)
            + lax.dot_general(lo, ones, _TN, preferred_element_type=F32))


def _group64_rmsnorm(x, g):
    m, w = x.shape
    r = lax.broadcasted_iota(jnp.int32, (LANES, LANES), 0) // DA_DH
    c = lax.broadcasted_iota(jnp.int32, (LANES, LANES), 1) // DA_DH
    same = jnp.where(r == c, 1.0, 0.0).astype(BF16)
    x2 = x * x
    parts = [_dot_exact_lhs(x2[:, i * LANES:(i + 1) * LANES], same) for i in range(w // LANES)]
    ss = jnp.concatenate(parts, axis=-1)
    return x * lax.rsqrt(ss * (1.0 / DA_DH) + NORM_EPS) * g


def _lower_bound(logits, layer):
    m = jnp.max(logits, axis=0, keepdims=True)
    e = jnp.exp(logits - m)
    return jnp.sum(e[:layer + 1], axis=0, keepdims=True) / jnp.sum(e, axis=0, keepdims=True)


def _lambda(lq1_ref, lk1_ref, lq2_ref, lk2_ref, lam_init):
    a = jnp.sum(lq1_ref[...] * lk1_ref[...], axis=-1, keepdims=True)
    b = jnp.sum(lq2_ref[...] * lk2_ref[...], axis=-1, keepdims=True)
    return jnp.exp(a) - jnp.exp(b) + lam_init


FFN_PART = 1024


def _ffn_parts(dff):
    return [(lo, min(lo + FFN_PART, dff)) for lo in range(0, dff, FFN_PART)]


class _NoSide:
    def pre(self, j):
        pass

    def post(self, j):
        pass


def _swiglu_parts(xn, wg_ref, wu_ref, wd_ref, side):
    y = None
    for j, (lo, hi) in enumerate(_ffn_parts(wg_ref.shape[1])):
        side.pre(j)
        g = jnp.dot(xn, wg_ref[:, lo:hi], preferred_element_type=F32)
        u = jnp.dot(xn, wu_ref[:, lo:hi], preferred_element_type=F32)
        a = (_silu(g) * u).astype(BF16)
        part = jnp.dot(a, wd_ref[lo:hi, :], preferred_element_type=F32)
        y = part if y is None else y + part
        side.post(j)
    return y


def _ffn_inproj_body(x_ref, g1_ref, wg_ref, wu_ref, wd_ref, gm_ref, win_ref, gq_ref, gk_ref,
                     h_ref, zhg_ref, q_ref, kt_ref, kb_ref, v_ref, side):
    tm = x_ref.shape[0]
    last = len(_ffn_parts(wg_ref.shape[1]))
    x = x_ref[...]
    y = _swiglu_parts(_rmsnorm(x, g1_ref[...]).astype(BF16), wg_ref, wu_ref, wd_ref, side)
    side.pre(last)
    h = x + 0.5 * y
    h_ref[...] = h
    z = jnp.dot(_rmsnorm(h, gm_ref[...]).astype(BF16), win_ref[...], preferred_element_type=F32)
    zhg_ref[...] = z[:, :HG_COLS]
    dq = z[:, HG_COLS:HG_COLS + DA_WIDTH]
    dk = z[:, HG_COLS + DA_WIDTH:HG_COLS + 2 * DA_WIDTH]
    dv = z[:, HG_COLS + 2 * DA_WIDTH:]
    q_ref[...] = (_group64_rmsnorm(dq, gq_ref[...]) * DA_SCALE).astype(BF16)
    k = _group64_rmsnorm(dk, gk_ref[...])
    kt_ref[0] = k.T
    kb_ref[...] = k.astype(BF16)
    for hd in range(DA_HEADS):
        v_ref[pl.ds(hd, tm, stride=DA_HEADS), :] = dv[:, hd * DA_DV:(hd + 1) * DA_DV]
    side.post(last)


def _outproj_ffn_body(h_ref, ohg_ref, oda_ref, wo_ref, g2_ref, wg_ref, wu_ref, wd_ref, y_ref, side):
    last = len(_ffn_parts(wg_ref.shape[1]))
    h = (h_ref[...]
         + jnp.dot(ohg_ref[...], wo_ref[:HG_WIDTH, :], preferred_element_type=F32)
         + jnp.dot(oda_ref[...], wo_ref[HG_WIDTH:, :], preferred_element_type=F32))
    y = _swiglu_parts(_rmsnorm(h, g2_ref[...]).astype(BF16), wg_ref, wu_ref, wd_ref, side)
    side.pre(last)
    y_ref[...] = h + 0.5 * y
    side.post(last)


def _ffn_inproj_kernel(*refs):
    _ffn_inproj_body(*refs, _NoSide())


def _outproj_ffn_kernel(*refs):
    _outproj_ffn_body(*refs, _NoSide())


PAGE_SHARE_FIRST = 2


class _PageWalk:
    def __init__(self, s, n_steps, pt_ref, kt_hbm, v_hbm, kbuf, vbuf, sem, first_page, n_pages):
        nb = pt_ref.shape[0]
        assert n_steps % nb == 0
        self.steps_per_seq = n_steps // nb
        self.group = kbuf.shape[1]
        assert n_pages % (self.group * self.steps_per_seq) == 0
        self.groups = n_pages // (self.group * self.steps_per_seq)
        assert self.groups % 2 == 0
        self.s, self.n_steps, self.first_page = s, n_steps, first_page
        self.pt_ref, self.kt_hbm, self.v_hbm = pt_ref, kt_hbm, v_hbm
        self.kbuf, self.vbuf, self.sem = kbuf, vbuf, sem
        self.seq_first = (s % self.steps_per_seq) == 0
        self.seq_last = (s % self.steps_per_seq) == self.steps_per_seq - 1

    def first_page_of(self, step, g):
        return self.first_page + ((step % self.steps_per_seq) * self.groups + g) * self.group

    def _copies(self, step, g):
        seq = step // self.steps_per_seq
        page0 = self.first_page_of(step, g)
        slot = g % 2
        copies = []
        for i in range(self.group):
            pg = self.pt_ref[seq, page0 + i]
            copies.append(pltpu.make_async_copy(self.kt_hbm.at[0, pg], self.kbuf.at[slot, i],
                                                self.sem.at[slot, 0]))
            copies.append(pltpu.make_async_copy(self.v_hbm.at[0, pg], self.vbuf.at[slot, i],
                                                self.sem.at[slot, 1]))
        return copies

    def prime(self):
        for g in range(2):
            for c in self._copies(0, g):
                c.start()

    def wait(self, g):
        for c in self._copies(self.s, g):
            c.wait()

    def start_ahead(self, g):
        step = self.s + (g + 2) // self.groups

        @pl.when(step < self.n_steps)
        def _():
            for c in self._copies(step, (g + 2) % self.groups):
                c.start()


def _fold_pages(kbuf, vbuf, slot, first_pos, past_len, qcol_ref, expand_ref, m_ref, l_ref, acc_ref):
    nmap = 2 * DA_HEADS
    page = kbuf.shape[-1]
    vrows = vbuf.shape[-2]
    mrow = lax.broadcasted_iota(jnp.int32, (nmap, 1), 0)
    slope = jnp.exp2((-8.0 / DA_HEADS) * ((mrow // 2) + 1).astype(F32))
    own = (lax.broadcasted_iota(jnp.int32, (1, vrows), 1) % DA_HEADS) == (mrow // 2)
    lane = lax.broadcasted_iota(jnp.int32, (1, page), 1)
    qc = qcol_ref[...].reshape(nmap, DA_DH, page)
    scores = []
    for i in range(kbuf.shape[1]):
        s = jnp.sum(kbuf[slot, i] * qc, axis=1)
        pos = first_pos + i * page + lane
        scores.append(s - slope * (past_len - pos).astype(F32))
    m_old = m_ref[...]
    m_new = m_old
    for s in scores:
        m_new = jnp.maximum(m_new, jnp.max(s, axis=-1, keepdims=True))
    alpha = jnp.exp(m_old - m_new)
    l_new = alpha * l_ref[...]
    acc = alpha * acc_ref[...]
    probs = []
    for s in scores:
        p = jnp.exp(s - m_new)
        l_new = l_new + jnp.sum(p, axis=-1, keepdims=True)
        probs.append(p.astype(BF16))
    spread = jnp.dot(jnp.concatenate(probs, axis=0), expand_ref[...],
                     preferred_element_type=F32)
    for i in range(kbuf.shape[1]):
        w = jnp.where(own, spread[i * nmap:(i + 1) * nmap], 0.0).astype(BF16)
        acc = acc + jnp.dot(w, vbuf[slot, i].astype(BF16), preferred_element_type=F32)
    m_ref[...] = m_new
    l_ref[...] = l_new
    acc_ref[...] = acc


class _PageSide:
    def __init__(self, walk, n_stages, past_len, qcol_ref, expand_ref, m_ref, l_ref, acc_ref,
                 begin, finish):
        self.walk, self.n_stages, self.past_len = walk, n_stages, past_len
        self.state = (qcol_ref, expand_ref, m_ref, l_ref, acc_ref)
        self.begin, self.finish = begin, finish

    def _groups(self, j):
        return range(j, self.walk.groups, self.n_stages)

    def pre(self, j):
        walk = self.walk
        if j == 0:
            page, vrows = walk.kbuf.shape[-1], walk.vbuf.shape[-2]
            expand_ref = self.state[1]

            @pl.when(walk.s == 0)
            def _():
                walk.prime()
                tok = lax.broadcasted_iota(jnp.int32, (page, vrows), 0)
                vrow = lax.broadcasted_iota(jnp.int32, (page, vrows), 1)
                expand_ref[...] = jnp.where(vrow // DA_HEADS == tok, 1.0, 0.0).astype(BF16)

            pl.when(walk.seq_first)(self.begin)
        for g in self._groups(j):
            walk.wait(g)

    def post(self, j):
        walk = self.walk
        page = walk.kbuf.shape[-1]
        for g in self._groups(j):
            _fold_pages(walk.kbuf, walk.vbuf, g % 2, walk.first_page_of(walk.s, g) * page,
                        self.past_len, *self.state)
            walk.start_ahead(g)
        if j == self.n_stages - 1:
            pl.when(walk.seq_last)(self.finish)


def _ffn_inproj_pages_kernel(pt_ref, x_ref, g1_ref, wg_ref, wu_ref, wd_ref, gm_ref, win_ref, gq_ref,
                             gk_ref, qs_ref, kn_ref, vn_ref, kt_hbm, v_hbm,
                             h_ref, zhg_ref, q_ref, kt_ref, kb_ref, v_ref, mo_ref, lo_ref, ao_ref,
                             kbuf, vbuf, sem, qcol_ref, expand_ref, m_ref, l_ref, acc_ref,
                             *, n_steps, n_pages):
    nmap = 2 * DA_HEADS
    page = kbuf.shape[-1]
    walk = _PageWalk(pl.program_id(0), n_steps, pt_ref, kt_hbm, v_hbm, kbuf, vbuf, sem,
                     0, n_pages)

    def begin():
        qcol = _col_bcast(qs_ref[0].astype(F32), page)
        qcol_ref[...] = qcol
        kcol = _col_bcast(kn_ref[0], page)
        m_ref[...] = jnp.sum((qcol * kcol).reshape(nmap, DA_DH, page), axis=1)
        l_ref[...] = jnp.ones_like(l_ref)
        vn = vn_ref[0]
        acc_ref[...] = jnp.concatenate(
            [vn[:, (r // 2) * DA_DV:(r // 2 + 1) * DA_DV] for r in range(nmap)], axis=0)

    def finish():
        mo_ref[0] = m_ref[...]
        lo_ref[0] = l_ref[...]
        ao_ref[0] = acc_ref[...]

    n_stages = len(_ffn_parts(wg_ref.shape[1])) + 1
    side = _PageSide(walk, n_stages, pt_ref.shape[1] * page, qcol_ref, expand_ref, m_ref, l_ref,
                     acc_ref, begin, finish)
    _ffn_inproj_body(x_ref, g1_ref, wg_ref, wu_ref, wd_ref, gm_ref, win_ref, gq_ref, gk_ref,
                     h_ref, zhg_ref, q_ref, kt_ref, kb_ref, v_ref, side)


def _outproj_ffn_pages_kernel(pt_ref, h_ref, ohg_ref, oda_ref, wo_ref, g2_ref, wg_ref, wu_ref, wd_ref,
                              qs_ref, mi_ref, li_ref, ai_ref, lq1_ref, lk1_ref, lq2_ref, lk2_ref,
                              gsub_ref, kt_hbm, v_hbm, y_ref, os_ref,
                              kbuf, vbuf, sem, qcol_ref, expand_ref, m_ref, l_ref, acc_ref,
                              *, n_steps, first_page, lam_init):
    page = kbuf.shape[-1]
    n_pages = pt_ref.shape[1]
    walk = _PageWalk(pl.program_id(0), n_steps, pt_ref, kt_hbm, v_hbm, kbuf, vbuf, sem,
                     first_page, n_pages - first_page)

    def begin():
        qcol_ref[...] = _col_bcast(qs_ref[0].astype(F32), page)
        m_ref[...] = mi_ref[0]
        l_ref[...] = li_ref[0]
        acc_ref[...] = ai_ref[0]

    def finish():
        lam = _lambda(lq1_ref, lk1_ref, lq2_ref, lk2_ref, lam_init)
        acc, l_fin = acc_ref[...], l_ref[...]
        outs = []
        for hd in range(DA_HEADS):
            r1, r2 = 2 * hd, 2 * hd + 1
            outs.append(_diff_combine(acc[r1:r1 + 1], l_fin[r1:r1 + 1], acc[r2:r2 + 1],
                                      l_fin[r2:r2 + 1], lam, gsub_ref[...], lam_init))
        os_ref[0] = jnp.concatenate(outs, axis=-1).astype(BF16)

    n_stages = len(_ffn_parts(wg_ref.shape[1])) + 1
    side = _PageSide(walk, n_stages, n_pages * page, qcol_ref, expand_ref, m_ref, l_ref, acc_ref,
                     begin, finish)
    _outproj_ffn_body(h_ref, ohg_ref, oda_ref, wo_ref, g2_ref, wg_ref, wu_ref, wd_ref, y_ref, side)


def _resident(shape):
    return pl.BlockSpec(shape, lambda *_: (0,) * len(shape), pipeline_mode=pl.Buffered(1))


def _page_scratch(page, vrows, pages_per_step, groups_per_step):
    nmap = 2 * DA_HEADS
    assert pages_per_step % groups_per_step == 0 and groups_per_step % 2 == 0
    group = pages_per_step // groups_per_step
    return [pltpu.VMEM((2, group, nmap, DA_DH, page), F32),
            pltpu.VMEM((2, group, vrows, DA_DV), F32),
            pltpu.SemaphoreType.DMA((2, 2)),
            pltpu.VMEM((DA_WIDTH, page), F32), pltpu.VMEM((page, vrows), BF16),
            pltpu.VMEM((nmap, page), F32), pltpu.VMEM((nmap, page), F32),
            pltpu.VMEM((nmap, DA_DV), F32)]


def _ffn_inproj(x, g1, wg, wu, wd, gm, win, gq, gk, tm, seq, pages=None):
    n, d = x.shape
    dff = wg.shape[1]
    tiles = seq // tm
    rows = lambda w: pl.BlockSpec((tm, w), lambda i, *_: (i, 0))
    in_specs = [rows(d), _resident((1, d)), _resident((d, dff)), _resident((d, dff)),
                _resident((dff, d)), _resident((1, d)), _resident(win.shape),
                _resident((1, DA_WIDTH)), _resident((1, DA_WIDTH))]
    out_specs = [rows(d), rows(HG_COLS), rows(DA_WIDTH),
                 pl.BlockSpec((1, DA_WIDTH, tm), lambda i, *_: (i // tiles, 0, i % tiles)),
                 rows(DA_WIDTH),
                 pl.BlockSpec((tm * DA_HEADS, DA_DV), lambda i, *_: (i, 0))]
    out_shape = [jax.ShapeDtypeStruct((n, d), F32), jax.ShapeDtypeStruct((n, HG_COLS), F32),
                 jax.ShapeDtypeStruct((n, DA_WIDTH), BF16),
                 jax.ShapeDtypeStruct((n // seq, DA_WIDTH, seq), F32),
                 jax.ShapeDtypeStruct((n, DA_WIDTH), BF16),
                 jax.ShapeDtypeStruct((n * DA_HEADS, DA_DV), F32)]
    params = pltpu.CompilerParams(dimension_semantics=("arbitrary",), vmem_limit_bytes=VMEM_LIMIT)
    args = (x, g1, wg, wu, wd, gm, win, gq, gk)
    if pages is None:
        return pl.pallas_call(_ffn_inproj_kernel, grid=(n // tm,), in_specs=in_specs,
                              out_specs=out_specs, out_shape=out_shape, compiler_params=params,
                              name="ffn_inproj")(*args)
    page_table, qs, kn, vn, kt_pages, v_pages, n_pages = pages
    nb = page_table.shape[0]
    nmap = 2 * DA_HEADS
    page, vrows = kt_pages.shape[-1], v_pages.shape[-2]
    per_seq = (n // tm) // nb
    seq_row = lambda w: pl.BlockSpec((1, 1, w), lambda i, *_: (i // per_seq, 0, 0))
    state = lambda w: pl.BlockSpec((1, nmap, w), lambda i, *_: (i // per_seq, 0, 0))
    grid_spec = pltpu.PrefetchScalarGridSpec(
        num_scalar_prefetch=1, grid=(n // tm,),
        in_specs=in_specs + [seq_row(DA_WIDTH), seq_row(DA_WIDTH), seq_row(DA_HEADS * DA_DV),
                             pl.BlockSpec(memory_space=pl.ANY), pl.BlockSpec(memory_space=pl.ANY)],
        out_specs=out_specs + [state(page), state(page), state(DA_DV)],
        scratch_shapes=_page_scratch(page, vrows, n_pages // per_seq, 2))
    out_shape = out_shape + [jax.ShapeDtypeStruct((nb, nmap, page), F32),
                             jax.ShapeDtypeStruct((nb, nmap, page), F32),
                             jax.ShapeDtypeStruct((nb, nmap, DA_DV), F32)]
    return pl.pallas_call(
        functools.partial(_ffn_inproj_pages_kernel, n_steps=n // tm, n_pages=n_pages),
        grid_spec=grid_spec, out_shape=out_shape, compiler_params=params, name="ffn_inproj_pages",
    )(page_table, *args, qs.reshape(nb, 1, DA_WIDTH), kn.reshape(nb, 1, DA_WIDTH),
      vn.reshape(nb, 1, DA_HEADS * DA_DV), kt_pages, v_pages)


def _outproj_ffn(h, ohg, oda, wo, g2, wg, wu, wd, tm, pages=None):
    n, d = h.shape
    dff = wg.shape[1]
    rows = lambda w: pl.BlockSpec((tm, w), lambda i, *_: (i, 0))
    in_specs = [rows(d), rows(HG_WIDTH), rows(DA_WIDTH), _resident(wo.shape), _resident((1, d)),
                _resident((d, dff)), _resident((d, dff)), _resident((dff, d))]
    params = pltpu.CompilerParams(dimension_semantics=("arbitrary",), vmem_limit_bytes=VMEM_LIMIT)
    args = (h, ohg, oda, wo, g2, wg, wu, wd)
    if pages is None:
        return pl.pallas_call(_outproj_ffn_kernel, grid=(n // tm,), in_specs=in_specs,
                              out_specs=rows(d), out_shape=jax.ShapeDtypeStruct((n, d), F32),
                              compiler_params=params, name="outproj_ffn")(*args)
    page_table, qs, m, l, acc, lams, gsub, kt_pages, v_pages, first_page, lam_init = pages
    nb = page_table.shape[0]
    nmap = 2 * DA_HEADS
    page, vrows = kt_pages.shape[-1], v_pages.shape[-2]
    per_seq = (n // tm) // nb
    seq_row = lambda w: pl.BlockSpec((1, 1, w), lambda i, *_: (i // per_seq, 0, 0))
    state = lambda w: pl.BlockSpec((1, nmap, w), lambda i, *_: (i // per_seq, 0, 0))
    small = lambda a: pl.BlockSpec(a.shape, lambda i, *_: (0, 0))
    grid_spec = pltpu.PrefetchScalarGridSpec(
        num_scalar_prefetch=1, grid=(n // tm,),
        in_specs=in_specs + [seq_row(DA_WIDTH), state(page), state(page), state(DA_DV)]
                 + [small(a) for a in lams] + [small(gsub)]
                 + [pl.BlockSpec(memory_space=pl.ANY), pl.BlockSpec(memory_space=pl.ANY)],
        out_specs=[rows(d), seq_row(DA_HEADS * DA_DV)],
        scratch_shapes=_page_scratch(page, vrows, (page_table.shape[1] - first_page) // per_seq, 2))
    y, o_s = pl.pallas_call(
        functools.partial(_outproj_ffn_pages_kernel, n_steps=n // tm, first_page=first_page,
                          lam_init=lam_init),
        grid_spec=grid_spec,
        out_shape=[jax.ShapeDtypeStruct((n, d), F32),
                   jax.ShapeDtypeStruct((nb, 1, DA_HEADS * DA_DV), BF16)],
        compiler_params=params, name="outproj_ffn_pages",
    )(page_table, *args, qs.reshape(nb, 1, DA_WIDTH), m, l, acc, *lams, gsub, kt_pages, v_pages)
    return y, o_s.reshape(nb, DA_HEADS * DA_DV)


HG_CHUNK = 128
HG_BLOCK = 2
HG_GROUP = 4


def _hgrn_gates(z, lb):
    xq = z[:, :HG_WIDTH]
    xf = z[:, HG_WIDTH:2 * HG_WIDTH]
    xi = z[:, 2 * HG_WIDTH:3 * HG_WIDTH]
    xg = z[:, 3 * HG_WIDTH:]
    q = _silu(xq) * (HG_DK ** -0.5)
    f = lb + (1.0 - lb) * _sigmoid(xf)
    return q, 1.0 - f, f, xi, _silu(xg)


def _hgrn_chunk_kernel(z_ref, lbl_ref, gout_ref, o_ref, sfin_ref, st_ref, *, layer):
    c = pl.program_id(1)
    group, n = z_ref.shape[0], z_ref.shape[1]

    @pl.when(c == 0)
    def _():
        st_ref[...] = jnp.zeros_like(st_ref)

    lb = _lower_bound(lbl_ref[...], layer)
    row = lax.broadcasted_iota(jnp.int32, (n, n), 0)
    col = lax.broadcasted_iota(jnp.int32, (n, n), 1)
    tri = jnp.where(row >= col, 1.0, 0.0).astype(BF16)
    trow = lax.broadcasted_iota(jnp.int32, (n, 1), 0)
    spans = []
    w = HG_BLOCK
    while 2 * w <= n:
        span = 2 * w
        spans.append((w, span, (trow % span) >= w,
                      ((row // span) == (col // span)) & ((row % span) >= w) & ((col % span) < w)))
        w = span
    nears = [((row - col) == d) & ((row % HG_BLOCK) >= d) for d in range(HG_BLOCK)]

    for r in range(group):
        q, k, f, v, gate = _hgrn_gates(z_ref[r], lb)
        b = _dot_exact_rhs(tri, jnp.log2(f))
        a = [jnp.zeros((n, n), F32) for _ in range(HG_HEADS)]
        for w, span, right, lvl in spans:
            ref = jnp.concatenate(
                [jnp.broadcast_to(b[p * span + w - 1:p * span + w, :], (span, HG_WIDTH))
                 for p in range(n // span)], axis=0)
            e = jnp.exp2(-jnp.abs(b - ref))
            qw = jnp.where(right, q * e, 0.0).astype(BF16)
            kw = jnp.where(right, 0.0, k * e).astype(BF16)
            for h in range(HG_HEADS):
                hs = slice(h * HG_DK, (h + 1) * HG_DK)
                p = lax.dot_general(qw[:, hs], kw[:, hs], _NT, preferred_element_type=F32)
                a[h] = jnp.where(lvl, p, a[h])
        for d in range(HG_BLOCK):
            kd = pltpu.roll(k, d, axis=0) if d else k
            bd = pltpu.roll(b, d, axis=0) if d else b
            p = q * kd * jnp.exp2(jnp.minimum(b - bd, 0.0))
            for h in range(HG_HEADS):
                hs = slice(h * HG_DK, (h + 1) * HG_DK)
                a[h] = jnp.where(nears[d], jnp.sum(p[:, hs], axis=-1, keepdims=True), a[h])

        b_last = b[n - 1:n, :]
        q_in = (q * jnp.exp2(b)).astype(BF16)
        k_out = (k * jnp.exp2(b_last - b)).astype(BF16)
        carry = jnp.exp2(b_last)
        vb = v.astype(BF16)
        for h in range(HG_HEADS):
            hs = slice(h * HG_DK, (h + 1) * HG_DK)
            st = st_ref[r, h]
            o = (jnp.dot(a[h].astype(BF16), vb[:, hs], preferred_element_type=F32)
                 + lax.dot_general(q_in[:, hs], st.astype(BF16), _NT, preferred_element_type=F32))
            st_new = st * carry[:, hs] + lax.dot_general(vb[:, hs], k_out[:, hs], _TN,
                                                         preferred_element_type=F32)
            st_ref[r, h] = st_new
            o_ref[r, :, hs] = (_rmsnorm(o, gout_ref[...]) * gate[:, hs]).astype(BF16)

    @pl.when(c == pl.num_programs(1) - 1)
    def _():
        for r in range(group):
            for h in range(HG_HEADS):
                sfin_ref[r, h] = st_ref[r, h].T


def _hgrn_prompt(zhg, lb_logits, gout, batch, layer):
    n = zhg.shape[0]
    seq = n // batch
    nc = seq // HG_CHUNK
    group = HG_GROUP if batch % HG_GROUP == 0 else 1
    o, s_fin = pl.pallas_call(
        functools.partial(_hgrn_chunk_kernel, layer=layer),
        grid=(batch // group, nc),
        in_specs=[pl.BlockSpec((group, HG_CHUNK, HG_COLS), lambda g, c: (g, c, 0)),
                  pl.BlockSpec(lb_logits.shape, lambda g, c: (0, 0)),
                  pl.BlockSpec((1, HG_DV), lambda g, c: (0, 0))],
        out_specs=[pl.BlockSpec((group, HG_CHUNK, HG_WIDTH), lambda g, c: (g, c, 0)),
                   pl.BlockSpec((group, HG_HEADS, HG_DK, HG_DV), lambda g, c: (g, 0, 0, 0))],
        out_shape=[jax.ShapeDtypeStruct((batch, seq, HG_WIDTH), BF16),
                   jax.ShapeDtypeStruct((batch, HG_HEADS, HG_DK, HG_DV), F32)],
        scratch_shapes=[pltpu.VMEM((group, HG_HEADS, HG_DV, HG_DK), F32)],
        compiler_params=pltpu.CompilerParams(dimension_semantics=("arbitrary", "arbitrary")),
        name="hgrn_prompt",
    )(zhg.reshape(batch, seq, HG_COLS), lb_logits, gout)
    return o.reshape(n, HG_WIDTH), s_fin


HG_STEP_GROUP = 8


def _hgrn_step_kernel(z_ref, s_ref, lbl_ref, gout_ref, o_ref, snew_ref, *, layer):
    lb = _lower_bound(lbl_ref[...], layer)
    for r in range(z_ref.shape[0]):
        q, k, f, v, gate = _hgrn_gates(z_ref[r], lb)
        outs = []
        for h in range(HG_HEADS):
            hs = slice(h * HG_DK, (h + 1) * HG_DK)
            s_new = _col_bcast(f[:, hs]) * s_ref[r, h] + _col_bcast(k[:, hs]) * v[:, hs]
            snew_ref[r, h] = s_new
            o = jnp.sum(_col_bcast(q[:, hs]) * s_new, axis=0, keepdims=True)
            outs.append(_rmsnorm(o, gout_ref[...]) * gate[:, hs])
        o_ref[r] = jnp.concatenate(outs, axis=-1).astype(BF16)


def _hgrn_sample(zhg, state, lb_logits, gout, layer):
    nb = zhg.shape[0]
    grp = HG_STEP_GROUP if nb % HG_STEP_GROUP == 0 else 1
    return pl.pallas_call(
        functools.partial(_hgrn_step_kernel, layer=layer),
        grid=(nb // grp,),
        in_specs=[pl.BlockSpec((grp, 1, HG_COLS), lambda b: (b, 0, 0)),
                  pl.BlockSpec((grp, HG_HEADS, HG_DK, HG_DV), lambda b: (b, 0, 0, 0)),
                  pl.BlockSpec(lb_logits.shape, lambda b: (0, 0)),
                  pl.BlockSpec((1, HG_DV), lambda b: (0, 0))],
        out_specs=[pl.BlockSpec((grp, 1, HG_WIDTH), lambda b: (b, 0, 0)),
                   pl.BlockSpec((grp, HG_HEADS, HG_DK, HG_DV), lambda b: (b, 0, 0, 0))],
        out_shape=[jax.ShapeDtypeStruct((nb, 1, HG_WIDTH), BF16),
                   jax.ShapeDtypeStruct(state.shape, F32)],
        compiler_params=pltpu.CompilerParams(dimension_semantics=("arbitrary",)),
        name="hgrn_sample",
    )(zhg.reshape(nb, 1, HG_COLS), state, lb_logits, gout)


ATT_TILE = 512


def _head_slope(h):
    return jnp.exp2(jnp.full((1, 1), -8.0 / DA_HEADS, F32) * (h + 1).astype(F32))


def _diff_combine(acc1, l1, acc2, l2, lam, gsub, lam_init):
    out = acc1 / l1 - lam * (acc2 / l2)
    return _rmsnorm(out, gsub) * (1.0 - lam_init)


ATT_VROWS = DA_DV + 16


def _attn_prompt_kernel(q_ref, k_ref, v_ref, lq1_ref, lk1_ref, lq2_ref, lk2_ref, gsub_ref, o_ref,
                        kaug_ref, vt_ref, m_ref, acc_ref, p_ref, alpha_ref, gcol_ref, *, lam_init):
    qi = pl.program_id(2)
    t = q_ref.shape[0]
    nt = kaug_ref.shape[1]
    heads = vt_ref.shape[0]
    units = [(hh, i) for hh in range(heads) for i in range(2)]
    slopes = [_head_slope(pl.program_id(1) * heads + hh) for hh in range(heads)]
    lane = lax.broadcasted_iota(jnp.int32, (t, 2 * DA_DH), 1)
    loc = lax.broadcasted_iota(jnp.int32, (t, 2 * DA_DH), 0)
    loc_lo = (loc % 256).astype(F32)
    loc_hi = (loc - loc % 256).astype(F32)
    slot = [(1 - i) * DA_DH for i in range(2)]

    def augment(x, i, extras):
        out = jnp.zeros_like(x)
        for n, e in enumerate(extras):
            out = jnp.where(lane == slot[i] + n, e, out)
        return jnp.where((lane // DA_DH) == i, x, out)

    @pl.when(qi == 0)
    def _():
        for j in range(nt):
            for u, (hh, i) in enumerate(units):
                k = k_ref[j * t:(j + 1) * t, hh * 2 * DA_DH:(hh + 1) * 2 * DA_DH].astype(F32)
                kaug_ref[u, j] = augment(k, i, (slopes[hh] * loc_lo, slopes[hh] * loc_hi, 1.0, 1.0)
                                         ).astype(BF16)
            for hh in range(heads):
                head = pl.program_id(1) * heads + hh
                vj = v_ref[pl.ds(j * t * DA_HEADS + head, t, stride=DA_HEADS), :]
                vt_ref[hh, j, :DA_DV, :] = vj.T.astype(BF16)
                vt_ref[hh, j, DA_DV:, :] = jnp.ones((ATT_VROWS - DA_DV, t), BF16)
        gcol_ref[...] = _col_bcast(gsub_ref[...], t)

    qts = []
    for hh, i in units:
        q = q_ref[:, hh * 2 * DA_DH:(hh + 1) * 2 * DA_DH].astype(F32)
        qts.append(augment(q, i, (1.0, 1.0, -slopes[hh] * loc_lo, -slopes[hh] * loc_hi)
                           ).T.astype(BF16))
    m_ref[...] = jnp.full_like(m_ref, NEG_INF)
    acc_ref[...] = jnp.zeros_like(acc_ref)
    causal = (lax.broadcasted_iota(jnp.int32, (t, t), 0)
              <= lax.broadcasted_iota(jnp.int32, (t, t), 1))

    def values(ki, slot):
        for u, (hh, _) in enumerate(units):
            acc_ref[u] = (alpha_ref[slot, u] * acc_ref[u]
                          + jnp.dot(vt_ref[hh, ki], p_ref[slot, u], preferred_element_type=F32))

    def step(ki, slot, keep=None, first=False):
        ss = [jnp.dot(kaug_ref[u, ki], qts[u], preferred_element_type=F32)
              for u in range(len(units))]
        if not first:
            values(ki - 1, 1 - slot)
        for u, (hh, _) in enumerate(units):
            shift = -slopes[hh] * ((qi - ki) * t).astype(F32)
            s = ss[u] if keep is None else jnp.where(keep, ss[u], NEG_INF)
            m_old = m_ref[u]
            m_new = jnp.maximum(m_old, jnp.max(s, axis=0, keepdims=True) + shift)
            alpha_ref[slot, u] = jnp.exp(m_old - m_new)
            p_ref[slot, u] = jnp.exp(s - (m_new - shift)).astype(BF16)
            m_ref[u] = m_new

    pl.when(qi == 0)(lambda: step(0, 0, keep=causal, first=True))
    pl.when(qi > 0)(lambda: step(0, 0, first=True))
    inner = qi - 1

    def pair(j, carry):
        step(1 + 2 * j, 1)
        step(2 + 2 * j, 0)
        return carry

    lax.fori_loop(0, inner // 2, pair, 0)
    for parity in range(2):
        @pl.when(jnp.logical_and(qi > 0, qi % 2 == parity))
        def _():
            if parity == 0:
                step(qi - 1, 1)
            step(qi, parity, keep=causal)

        @pl.when(qi % 2 == parity)
        def _():
            values(qi, parity)

    lam = _lambda(lq1_ref, lk1_ref, lq2_ref, lk2_ref, lam_init)
    for hh in range(heads):
        a1, a2 = acc_ref[2 * hh], acc_ref[2 * hh + 1]
        inv1 = 1.0 / a1[DA_DV:DA_DV + 1]
        inv2 = lam / a2[DA_DV:DA_DV + 1]
        out = a1[:DA_DV] * inv1 - a2[:DA_DV] * inv2
        ms = jnp.mean(out * out, axis=0, keepdims=True)
        out = out * (lax.rsqrt(ms + NORM_EPS) * (1.0 - lam_init)) * gcol_ref[...]
        o_ref[:, hh * DA_DV:(hh + 1) * DA_DV] = out.T.astype(BF16)


ATT_HEADS = 2


def _attn_prompt(qb, kb, v, lams, gsub, batch, lam_init):
    n = qb.shape[0]
    seq = n // batch
    nq = seq // ATT_TILE
    units = 2 * ATT_HEADS
    small = lambda a: pl.BlockSpec(a.shape, lambda b, h, i: (0, 0))
    return pl.pallas_call(
        functools.partial(_attn_prompt_kernel, lam_init=lam_init),
        grid=(batch, DA_HEADS // ATT_HEADS, nq),
        in_specs=[pl.BlockSpec((ATT_TILE, ATT_HEADS * 2 * DA_DH), lambda b, h, i: (b * nq + i, h)),
                  pl.BlockSpec((seq, ATT_HEADS * 2 * DA_DH), lambda b, h, i: (b, h)),
                  pl.BlockSpec((seq * DA_HEADS, DA_DV), lambda b, h, i: (b, 0))]
                 + [small(a) for a in lams] + [small(gsub)],
        out_specs=pl.BlockSpec((ATT_TILE, ATT_HEADS * DA_DV), lambda b, h, i: (b * nq + i, h)),
        out_shape=jax.ShapeDtypeStruct((n, DA_HEADS * DA_DV), BF16),
        scratch_shapes=[pltpu.VMEM((units, nq, ATT_TILE, 2 * DA_DH), BF16),
                        pltpu.VMEM((ATT_HEADS, nq, ATT_VROWS, ATT_TILE), BF16),
                        pltpu.VMEM((units, 1, ATT_TILE), F32),
                        pltpu.VMEM((units, ATT_VROWS, ATT_TILE), F32),
                        pltpu.VMEM((2, units, ATT_TILE, ATT_TILE), BF16),
                        pltpu.VMEM((2, units, 1, ATT_TILE), F32),
                        pltpu.VMEM((DA_DV, ATT_TILE), F32)],
        compiler_params=pltpu.CompilerParams(
            dimension_semantics=("arbitrary", "arbitrary", "arbitrary"),
            vmem_limit_bytes=VMEM_LIMIT),
        name="attn_prompt",
    )(qb, kb, v, *lams, gsub)


PROMPT_ROWS = 256


def kernel(x_prompt, x_sample, cache_k, cache_v, state_hgrn, page_table, ffn1_norm, ffn1_w_gate, ffn1_w_up, ffn1_w_down, mix_norm, w_in, hg_lb_logits, hg_out_norm, da_q_norm, da_k_norm, da_lambda_q1, da_lambda_k1, da_lambda_q2, da_lambda_k2, da_subln, w_out, ffn2_norm, ffn2_w_gate, ffn2_w_up, ffn2_w_down):
    batch, seq, d = x_prompt.shape
    nb = x_sample.shape[0]
    depth = ffn1_norm.shape[0]
    nmap = 2 * DA_HEADS
    yp = x_prompt.reshape(batch * seq, d)
    ys = x_sample.reshape(nb, d)
    kt_cache = jnp.transpose(cache_k, (0, 1, 3, 4, 2))
    v_cache = cache_v.reshape(cache_v.shape[:2] + (-1, DA_DV))
    outs = [[] for _ in range(6)]
    for l in range(depth):
        lam_init = 0.8 - 0.6 * math.exp(-0.3 * l)
        bf = lambda w: w[l].astype(BF16)
        w1 = (bf(ffn1_w_gate), bf(ffn1_w_up), bf(ffn1_w_down))
        w2 = (bf(ffn2_w_gate), bf(ffn2_w_up), bf(ffn2_w_down))
        win, wo = bf(w_in), bf(w_out)
        gq = jnp.tile(da_q_norm[l:l + 1], (1, nmap))
        gk = jnp.tile(da_k_norm[l:l + 1], (1, nmap))
        lams = (da_lambda_q1[l:l + 1], da_lambda_k1[l:l + 1], da_lambda_q2[l:l + 1],
                da_lambda_k2[l:l + 1])
        gout, gsub = hg_out_norm[l:l + 1], da_subln[l:l + 1]

        def inproj(x, tm, rows_per_seq, pages=None):
            return _ffn_inproj(x, ffn1_norm[l:l + 1], *w1, mix_norm[l:l + 1], win, gq, gk, tm,
                               rows_per_seq, pages)

        def outproj(h, ohg, oda, tm, pages=None):
            return _outproj_ffn(h, ohg, oda, wo, ffn2_norm[l:l + 1], *w2, tm, pages)

        def keys(kt):
            return jnp.transpose(kt.reshape(kt.shape[0], nmap, DA_DH, kt.shape[2]), (0, 3, 1, 2))

        hs, zhg_s, qs, kt_s, _, vs = inproj(ys, nb, nb)
        ks = keys(kt_s).reshape(nb, DA_WIDTH)
        ohg_s, s_new = _hgrn_sample(zhg_s, state_hgrn[l], hg_lb_logits, gout, l)
        kt_pages, v_pages = kt_cache[l:l + 1], v_cache[l:l + 1]
        first = page_table.shape[1] // PAGE_SHARE_FIRST
        h, zhg, qb, kt, kb, v, m_s, l_s, acc_s = inproj(
            yp, PROMPT_ROWS, seq, (page_table, qs, ks, vs, kt_pages, v_pages, first))
        ohg, s_fin = _hgrn_prompt(zhg, hg_lb_logits, gout, batch, l)
        oda = _attn_prompt(qb, kb, v, lams, gsub, batch, lam_init)
        yp, oda_s = outproj(h, ohg, oda, PROMPT_ROWS, (page_table, qs, m_s, l_s, acc_s, lams, gsub,
                                                       kt_pages, v_pages, first, lam_init))
        outs[0].append(keys(kt))
        outs[1].append(v.reshape(batch, seq, DA_HEADS, DA_DV))
        outs[2].append(s_fin)
        ys = outproj(hs, ohg_s.reshape(nb, HG_WIDTH), oda_s, nb)
        outs[3].append(ks.reshape(nb, 1, nmap, DA_DH))
        outs[4].append(vs.reshape(nb, 1, DA_HEADS, DA_DV))
        outs[5].append(s_new)
    stacked = [jnp.stack(o, axis=0) for o in outs]
    return (yp.reshape(batch, seq, d), ys.reshape(nb, 1, d), *stacked)
```

```python
import functools
import math

import jax
import jax.numpy as jnp
from jax import lax
from jax.experimental import pallas as pl
from jax.experimental.pallas import tpu as pltpu

F32 = jnp.float32
BF16 = jnp.bfloat16

NORM_EPS = 1e-6
NEG_INF = -1e30
LANES = 128
SUBLANES = 8
HG_HEADS = 4
HG_DK = 128
HG_DV = 128
HG_WIDTH = HG_HEADS * HG_DK
DA_HEADS = 4
DA_DH = 64
DA_DV = 128
DA_WIDTH = 2 * DA_HEADS * DA_DH
DA_SCALE = DA_DH ** -0.5
HG_COLS = 4 * HG_WIDTH
VMEM_LIMIT = 60 * 1024 * 1024

_NT = (((1,), (1,)), ((), ()))
_TN = (((0,), (0,)), ((), ()))


def _rmsnorm(x, g):
    ms = jnp.mean(x * x, axis=-1, keepdims=True)
    return x * lax.rsqrt(ms + NORM_EPS) * g


def _sigmoid(x):
    return 0.5 * jnp.tanh(0.5 * x) + 0.5


def _silu(x):
    return x * _sigmoid(x)


def _split3(x):
    hi = x.astype(BF16)
    r1 = x - hi.astype(F32)
    mid = r1.astype(BF16)
    lo = (r1 - mid.astype(F32)).astype(BF16)
    return hi, mid, lo


def _dot_exact_rhs(sel, x):
    hi, mid, lo = _split3(x)
    return (jnp.dot(sel, hi, preferred_element_type=F32)
            + jnp.dot(sel, mid, preferred_element_type=F32)
            + jnp.dot(sel, lo, preferred_element_type=F32))


def _dot_exact_lhs(x, sel):
    hi, mid, lo = _split3(x)
    return (jnp.dot(hi, sel, preferred_element_type=F32)
            + jnp.dot(mid, sel, preferred_element_type=F32)
            + jnp.dot(lo, sel, preferred_element_type=F32))


def _col_bcast(row, width=LANES):
    n = row.shape[1]
    k = 2 * SUBLANES
    first = lax.broadcasted_iota(jnp.int32, (k, n), 0) == 0
    rows = jnp.where(first, jnp.broadcast_to(row, (k, n)), 0.0)
    ones = jnp.ones((k, width), BF16)
    hi, mid, lo = _split3(rows)
    return (lax.dot_general(hi, ones, _TN, preferred_element_type=F32)
            + lax.dot_general(mid, ones, _TN, preferred_element_type=F32)
            + lax.dot_general(lo, ones, _TN, preferred_element_type=F32))


def _group64_rmsnorm(x, g):
    m, w = x.shape
    r = lax.broadcasted_iota(jnp.int32, (LANES, LANES), 0) // DA_DH
    c = lax.broadcasted_iota(jnp.int32, (LANES, LANES), 1) // DA_DH
    same = jnp.where(r == c, 1.0, 0.0).astype(BF16)
    x2 = x * x
    parts = [_dot_exact_lhs(x2[:, i * LANES:(i + 1) * LANES], same) for i in range(w // LANES)]
    ss = jnp.concatenate(parts, axis=-1)
    return x * lax.rsqrt(ss * (1.0 / DA_DH) + NORM_EPS) * g


def _lower_bound(logits, layer):
    m = jnp.max(logits, axis=0, keepdims=True)
    e = jnp.exp(logits - m)
    return jnp.sum(e[:layer + 1], axis=0, keepdims=True) / jnp.sum(e, axis=0, keepdims=True)


def _lambda(lq1_ref, lk1_ref, lq2_ref, lk2_ref, lam_init):
    a = jnp.sum(lq1_ref[...] * lk1_ref[...], axis=-1, keepdims=True)
    b = jnp.sum(lq2_ref[...] * lk2_ref[...], axis=-1, keepdims=True)
    return jnp.exp(a) - jnp.exp(b) + lam_init


FFN_PART = 1024


def _ffn_parts(dff):
    return [(lo, min(lo + FFN_PART, dff)) for lo in range(0, dff, FFN_PART)]


class _NoSide:
    def pre(self, j):
        pass

    def post(self, j):
        pass


def _swiglu_parts(xn, wg_ref, wu_ref, wd_ref, side):
    y = None
    for j, (lo, hi) in enumerate(_ffn_parts(wg_ref.shape[1])):
        side.pre(j)
        g = jnp.dot(xn, wg_ref[:, lo:hi], preferred_element_type=F32)
        u = jnp.dot(xn, wu_ref[:, lo:hi], preferred_element_type=F32)
        a = (_silu(g) * u).astype(BF16)
        part = jnp.dot(a, wd_ref[lo:hi, :], preferred_element_type=F32)
        y = part if y is None else y + part
        side.post(j)
    return y


def _ffn_inproj_body(x_ref, g1_ref, wg_ref, wu_ref, wd_ref, gm_ref, win_ref, gq_ref, gk_ref,
                     h_ref, zhg_ref, q_ref, kt_ref, kb_ref, v_ref, side):
    tm = x_ref.shape[0]
    last = len(_ffn_parts(wg_ref.shape[1]))
    x = x_ref[...]
    y = _swiglu_parts(_rmsnorm(x, g1_ref[...]).astype(BF16), wg_ref, wu_ref, wd_ref, side)
    side.pre(last)
    h = x + 0.5 * y
    h_ref[...] = h
    z = jnp.dot(_rmsnorm(h, gm_ref[...]).astype(BF16), win_ref[...], preferred_element_type=F32)
    zhg_ref[...] = z[:, :HG_COLS]
    dq = z[:, HG_COLS:HG_COLS + DA_WIDTH]
    dk = z[:, HG_COLS + DA_WIDTH:HG_COLS + 2 * DA_WIDTH]
    dv = z[:, HG_COLS + 2 * DA_WIDTH:]
    q_ref[...] = (_group64_rmsnorm(dq, gq_ref[...]) * DA_SCALE).astype(BF16)
    k = _group64_rmsnorm(dk, gk_ref[...])
    kt_ref[0] = k.T
    kb_ref[...] = k.astype(BF16)
    for hd in range(DA_HEADS):
        v_ref[pl.ds(hd, tm, stride=DA_HEADS), :] = dv[:, hd * DA_DV:(hd + 1) * DA_DV]
    side.post(last)


def _outproj_ffn_body(h_ref, ohg_ref, oda_ref, wo_ref, g2_ref, wg_ref, wu_ref, wd_ref, y_ref, side):
    last = len(_ffn_parts(wg_ref.shape[1]))
    h = (h_ref[...]
         + jnp.dot(ohg_ref[...], wo_ref[:HG_WIDTH, :], preferred_element_type=F32)
         + jnp.dot(oda_ref[...], wo_ref[HG_WIDTH:, :], preferred_element_type=F32))
    y = _swiglu_parts(_rmsnorm(h, g2_ref[...]).astype(BF16), wg_ref, wu_ref, wd_ref, side)
    side.pre(last)
    y_ref[...] = h + 0.5 * y
    side.post(last)


def _ffn_inproj_kernel(*refs):
    _ffn_inproj_body(*refs, _NoSide())


def _outproj_ffn_kernel(*refs):
    _outproj_ffn_body(*refs, _NoSide())


PAGE_SHARE_FIRST = 2


class _PageWalk:
    def __init__(self, s, n_steps, pt_ref, kt_hbm, v_hbm, kbuf, vbuf, sem, first_page, n_pages):
        nb = pt_ref.shape[0]
        assert n_steps % nb == 0
        self.steps_per_seq = n_steps // nb
        self.group = kbuf.shape[1]
        assert n_pages % (self.group * self.steps_per_seq) == 0
        self.groups = n_pages // (self.group * self.steps_per_seq)
        assert self.groups % 2 == 0
        self.s, self.n_steps, self.first_page = s, n_steps, first_page
        self.pt_ref, self.kt_hbm, self.v_hbm = pt_ref, kt_hbm, v_hbm
        self.kbuf, self.vbuf, self.sem = kbuf, vbuf, sem
        self.seq_first = (s % self.steps_per_seq) == 0
        self.seq_last = (s % self.steps_per_seq) == self.steps_per_seq - 1

    def first_page_of(self, step, g):
        return self.first_page + ((step % self.steps_per_seq) * self.groups + g) * self.group

    def _copies(self, step, g):
        seq = step // self.steps_per_seq
        page0 = self.first_page_of(step, g)
        slot = g % 2
        copies = []
        for i in range(self.group):
            pg = self.pt_ref[seq, page0 + i]
            copies.append(pltpu.make_async_copy(self.kt_hbm.at[0, pg], self.kbuf.at[slot, i],
                                                self.sem.at[slot, 0]))
            copies.append(pltpu.make_async_copy(self.v_hbm.at[0, pg], self.vbuf.at[slot, i],
                                                self.sem.at[slot, 1]))
        return copies

    def prime(self):
        for g in range(2):
            for c in self._copies(0, g):
                c.start()

    def wait(self, g):
        for c in self._copies(self.s, g):
            c.wait()

    def start_ahead(self, g):
        step = self.s + (g + 2) // self.groups

        @pl.when(step < self.n_steps)
        def _():
            for c in self._copies(step, (g + 2) % self.groups):
                c.start()


def _fold_pages(kbuf, vbuf, slot, first_pos, past_len, qcol_ref, expand_ref, m_ref, l_ref, acc_ref):
    nmap = 2 * DA_HEADS
    page = kbuf.shape[-1]
    vrows = vbuf.shape[-2]
    mrow = lax.broadcasted_iota(jnp.int32, (nmap, 1), 0)
    slope = jnp.exp2((-8.0 / DA_HEADS) * ((mrow // 2) + 1).astype(F32))
    own = (lax.broadcasted_iota(jnp.int32, (1, vrows), 1) % DA_HEADS) == (mrow // 2)
    lane = lax.broadcasted_iota(jnp.int32, (1, page), 1)
    qc = qcol_ref[...].reshape(nmap, DA_DH, page)
    scores = []
    for i in range(kbuf.shape[1]):
        s = jnp.sum(kbuf[slot, i] * qc, axis=1)
        pos = first_pos + i * page + lane
        scores.append(s - slope * (past_len - pos).astype(F32))
    m_old = m_ref[...]
    m_new = m_old
    for s in scores:
        m_new = jnp.maximum(m_new, jnp.max(s, axis=-1, keepdims=True))
    alpha = jnp.exp(m_old - m_new)
    l_new = alpha * l_ref[...]
    acc = alpha * acc_ref[...]
    probs = []
    for s in scores:
        p = jnp.exp(s - m_new)
        l_new = l_new + jnp.sum(p, axis=-1, keepdims=True)
        probs.append(p.astype(BF16))
    spread = jnp.dot(jnp.concatenate(probs, axis=0), expand_ref[...],
                     preferred_element_type=F32)
    for i in range(kbuf.shape[1]):
        w = jnp.where(own, spread[i * nmap:(i + 1) * nmap], 0.0).astype(BF16)
        acc = acc + jnp.dot(w, vbuf[slot, i].astype(BF16), preferred_element_type=F32)
    m_ref[...] = m_new
    l_ref[...] = l_new
    acc_ref[...] = acc


class _PageSide:
    def __init__(self, walk, n_stages, past_len, qcol_ref, expand_ref, m_ref, l_ref, acc_ref,
                 begin, finish):
        self.walk, self.n_stages, self.past_len = walk, n_stages, past_len
        self.state = (qcol_ref, expand_ref, m_ref, l_ref, acc_ref)
        self.begin, self.finish = begin, finish

    def _groups(self, j):
        return range(j, self.walk.groups, self.n_stages)

    def pre(self, j):
        walk = self.walk
        if j == 0:
            page, vrows = walk.kbuf.shape[-1], walk.vbuf.shape[-2]
            expand_ref = self.state[1]

            @pl.when(walk.s == 0)
            def _():
                walk.prime()
                tok = lax.broadcasted_iota(jnp.int32, (page, vrows), 0)
                vrow = lax.broadcasted_iota(jnp.int32, (page, vrows), 1)
                expand_ref[...] = jnp.where(vrow // DA_HEADS == tok, 1.0, 0.0).astype(BF16)

            pl.when(walk.seq_first)(self.begin)
        for g in self._groups(j):
            walk.wait(g)

    def post(self, j):
        walk = self.walk
        page = walk.kbuf.shape[-1]
        for g in self._groups(j):
            _fold_pages(walk.kbuf, walk.vbuf, g % 2, walk.first_page_of(walk.s, g) * page,
                        self.past_len, *self.state)
            walk.start_ahead(g)
        if j == self.n_stages - 1:
            pl.when(walk.seq_last)(self.finish)


def _ffn_inproj_pages_kernel(pt_ref, x_ref, g1_ref, wg_ref, wu_ref, wd_ref, gm_ref, win_ref, gq_ref,
                             gk_ref, qs_ref, kn_ref, vn_ref, kt_hbm, v_hbm,
                             h_ref, zhg_ref, q_ref, kt_ref, kb_ref, v_ref, mo_ref, lo_ref, ao_ref,
                             kbuf, vbuf, sem, qcol_ref, expand_ref, m_ref, l_ref, acc_ref,
                             *, n_steps, n_pages):
    nmap = 2 * DA_HEADS
    page = kbuf.shape[-1]
    walk = _PageWalk(pl.program_id(0), n_steps, pt_ref, kt_hbm, v_hbm, kbuf, vbuf, sem,
                     0, n_pages)

    def begin():
        qcol = _col_bcast(qs_ref[0].astype(F32), page)
        qcol_ref[...] = qcol
        kcol = _col_bcast(kn_ref[0], page)
        m_ref[...] = jnp.sum((qcol * kcol).reshape(nmap, DA_DH, page), axis=1)
        l_ref[...] = jnp.ones_like(l_ref)
        vn = vn_ref[0]
        acc_ref[...] = jnp.concatenate(
            [vn[:, (r // 2) * DA_DV:(r // 2 + 1) * DA_DV] for r in range(nmap)], axis=0)

    def finish():
        mo_ref[0] = m_ref[...]
        lo_ref[0] = l_ref[...]
        ao_ref[0] = acc_ref[...]

    n_stages = len(_ffn_parts(wg_ref.shape[1])) + 1
    side = _PageSide(walk, n_stages, pt_ref.shape[1] * page, qcol_ref, expand_ref, m_ref, l_ref,
                     acc_ref, begin, finish)
    _ffn_inproj_body(x_ref, g1_ref, wg_ref, wu_ref, wd_ref, gm_ref, win_ref, gq_ref, gk_ref,
                     h_ref, zhg_ref, q_ref, kt_ref, kb_ref, v_ref, side)


def _outproj_ffn_pages_kernel(pt_ref, h_ref, ohg_ref, oda_ref, wo_ref, g2_ref, wg_ref, wu_ref, wd_ref,
                              qs_ref, mi_ref, li_ref, ai_ref, lq1_ref, lk1_ref, lq2_ref, lk2_ref,
                              gsub_ref, kt_hbm, v_hbm, y_ref, os_ref,
                              kbuf, vbuf, sem, qcol_ref, expand_ref, m_ref, l_ref, acc_ref,
                              *, n_steps, first_page, lam_init):
    page = kbuf.shape[-1]
    n_pages = pt_ref.shape[1]
    walk = _PageWalk(pl.program_id(0), n_steps, pt_ref, kt_hbm, v_hbm, kbuf, vbuf, sem,
                     first_page, n_pages - first_page)

    def begin():
        qcol_ref[...] = _col_bcast(qs_ref[0].astype(F32), page)
        m_ref[...] = mi_ref[0]
        l_ref[...] = li_ref[0]
        acc_ref[...] = ai_ref[0]

    def finish():
        lam = _lambda(lq1_ref, lk1_ref, lq2_ref, lk2_ref, lam_init)
        acc, l_fin = acc_ref[...], l_ref[...]
        outs = []
        for hd in range(DA_HEADS):
            r1, r2 = 2 * hd, 2 * hd + 1
            outs.append(_diff_combine(acc[r1:r1 + 1], l_fin[r1:r1 + 1], acc[r2:r2 + 1],
                                      l_fin[r2:r2 + 1], lam, gsub_ref[...], lam_init))
        os_ref[0] = jnp.concatenate(outs, axis=-1).astype(BF16)

    n_stages = len(_ffn_parts(wg_ref.shape[1])) + 1
    side = _PageSide(walk, n_stages, n_pages * page, qcol_ref, expand_ref, m_ref, l_ref, acc_ref,
                     begin, finish)
    _outproj_ffn_body(h_ref, ohg_ref, oda_ref, wo_ref, g2_ref, wg_ref, wu_ref, wd_ref, y_ref, side)


def _resident(shape):
    return pl.BlockSpec(shape, lambda *_: (0,) * len(shape), pipeline_mode=pl.Buffered(1))


def _page_scratch(page, vrows, pages_per_step, groups_per_step):
    nmap = 2 * DA_HEADS
    assert pages_per_step % groups_per_step == 0 and groups_per_step % 2 == 0
    group = pages_per_step // groups_per_step
    return [pltpu.VMEM((2, group, nmap, DA_DH, page), F32),
            pltpu.VMEM((2, group, vrows, DA_DV), F32),
            pltpu.SemaphoreType.DMA((2, 2)),
            pltpu.VMEM((DA_WIDTH, page), F32), pltpu.VMEM((page, vrows), BF16),
            pltpu.VMEM((nmap, page), F32), pltpu.VMEM((nmap, page), F32),
            pltpu.VMEM((nmap, DA_DV), F32)]


def _ffn_inproj(x, g1, wg, wu, wd, gm, win, gq, gk, tm, seq, pages=None):
    n, d = x.shape
    dff = wg.shape[1]
    tiles = seq // tm
    rows = lambda w: pl.BlockSpec((tm, w), lambda i, *_: (i, 0))
    in_specs = [rows(d), _resident((1, d)), _resident((d, dff)), _resident((d, dff)),
                _resident((dff, d)), _resident((1, d)), _resident(win.shape),
                _resident((1, DA_WIDTH)), _resident((1, DA_WIDTH))]
    out_specs = [rows(d), rows(HG_COLS), rows(DA_WIDTH),
                 pl.BlockSpec((1, DA_WIDTH, tm), lambda i, *_: (i // tiles, 0, i % tiles)),
                 rows(DA_WIDTH),
                 pl.BlockSpec((tm * DA_HEADS, DA_DV), lambda i, *_: (i, 0))]
    out_shape = [jax.ShapeDtypeStruct((n, d), F32), jax.ShapeDtypeStruct((n, HG_COLS), F32),
                 jax.ShapeDtypeStruct((n, DA_WIDTH), BF16),
                 jax.ShapeDtypeStruct((n // seq, DA_WIDTH, seq), F32),
                 jax.ShapeDtypeStruct((n, DA_WIDTH), BF16),
                 jax.ShapeDtypeStruct((n * DA_HEADS, DA_DV), F32)]
    params = pltpu.CompilerParams(dimension_semantics=("arbitrary",), vmem_limit_bytes=VMEM_LIMIT)
    args = (x, g1, wg, wu, wd, gm, win, gq, gk)
    if pages is None:
        return pl.pallas_call(_ffn_inproj_kernel, grid=(n // tm,), in_specs=in_specs,
                              out_specs=out_specs, out_shape=out_shape, compiler_params=params,
                              name="ffn_inproj")(*args)
    page_table, qs, kn, vn, kt_pages, v_pages, n_pages = pages
    nb = page_table.shape[0]
    nmap = 2 * DA_HEADS
    page, vrows = kt_pages.shape[-1], v_pages.shape[-2]
    per_seq = (n // tm) // nb
    seq_row = lambda w: pl.BlockSpec((1, 1, w), lambda i, *_: (i // per_seq, 0, 0))
    state = lambda w: pl.BlockSpec((1, nmap, w), lambda i, *_: (i // per_seq, 0, 0))
    grid_spec = pltpu.PrefetchScalarGridSpec(
        num_scalar_prefetch=1, grid=(n // tm,),
        in_specs=in_specs + [seq_row(DA_WIDTH), seq_row(DA_WIDTH), seq_row(DA_HEADS * DA_DV),
                             pl.BlockSpec(memory_space=pl.ANY), pl.BlockSpec(memory_space=pl.ANY)],
        out_specs=out_specs + [state(page), state(page), state(DA_DV)],
        scratch_shapes=_page_scratch(page, vrows, n_pages // per_seq, 2))
    out_shape = out_shape + [jax.ShapeDtypeStruct((nb, nmap, page), F32),
                             jax.ShapeDtypeStruct((nb, nmap, page), F32),
                             jax.ShapeDtypeStruct((nb, nmap, DA_DV), F32)]
    return pl.pallas_call(
        functools.partial(_ffn_inproj_pages_kernel, n_steps=n // tm, n_pages=n_pages),
        grid_spec=grid_spec, out_shape=out_shape, compiler_params=params, name="ffn_inproj_pages",
    )(page_table, *args, qs.reshape(nb, 1, DA_WIDTH), kn.reshape(nb, 1, DA_WIDTH),
      vn.reshape(nb, 1, DA_HEADS * DA_DV), kt_pages, v_pages)


def _outproj_ffn(h, ohg, oda, wo, g2, wg, wu, wd, tm, pages=None):
    n, d = h.shape
    dff = wg.shape[1]
    rows = lambda w: pl.BlockSpec((tm, w), lambda i, *_: (i, 0))
    in_specs = [rows(d), rows(HG_WIDTH), rows(DA_WIDTH), _resident(wo.shape), _resident((1, d)),
                _resident((d, dff)), _resident((d, dff)), _resident((dff, d))]
    params = pltpu.CompilerParams(dimension_semantics=("arbitrary",), vmem_limit_bytes=VMEM_LIMIT)
    args = (h, ohg, oda, wo, g2, wg, wu, wd)
    if pages is None:
        return pl.pallas_call(_outproj_ffn_kernel, grid=(n // tm,), in_specs=in_specs,
                              out_specs=rows(d), out_shape=jax.ShapeDtypeStruct((n, d), F32),
                              compiler_params=params, name="outproj_ffn")(*args)
    page_table, qs, m, l, acc, lams, gsub, kt_pages, v_pages, first_page, lam_init = pages
    nb = page_table.shape[0]
    nmap = 2 * DA_HEADS
    page, vrows = kt_pages.shape[-1], v_pages.shape[-2]
    per_seq = (n // tm) // nb
    seq_row = lambda w: pl.BlockSpec((1, 1, w), lambda i, *_: (i // per_seq, 0, 0))
    state = lambda w: pl.BlockSpec((1, nmap, w), lambda i, *_: (i // per_seq, 0, 0))
    small = lambda a: pl.BlockSpec(a.shape, lambda i, *_: (0, 0))
    grid_spec = pltpu.PrefetchScalarGridSpec(
        num_scalar_prefetch=1, grid=(n // tm,),
        in_specs=in_specs + [seq_row(DA_WIDTH), state(page), state(page), state(DA_DV)]
                 + [small(a) for a in lams] + [small(gsub)]
                 + [pl.BlockSpec(memory_space=pl.ANY), pl.BlockSpec(memory_space=pl.ANY)],
        out_specs=[rows(d), seq_row(DA_HEADS * DA_DV)],
        scratch_shapes=_page_scratch(page, vrows, (page_table.shape[1] - first_page) // per_seq, 2))
    y, o_s = pl.pallas_call(
        functools.partial(_outproj_ffn_pages_kernel, n_steps=n // tm, first_page=first_page,
                          lam_init=lam_init),
        grid_spec=grid_spec,
        out_shape=[jax.ShapeDtypeStruct((n, d), F32),
                   jax.ShapeDtypeStruct((nb, 1, DA_HEADS * DA_DV), BF16)],
        compiler_params=params, name="outproj_ffn_pages",
    )(page_table, *args, qs.reshape(nb, 1, DA_WIDTH), m, l, acc, *lams, gsub, kt_pages, v_pages)
    return y, o_s.reshape(nb, DA_HEADS * DA_DV)


HG_CHUNK = 128
HG_BLOCK = 2
HG_GROUP = 4


def _hgrn_gates(z, lb):
    xq = z[:, :HG_WIDTH]
    xf = z[:, HG_WIDTH:2 * HG_WIDTH]
    xi = z[:, 2 * HG_WIDTH:3 * HG_WIDTH]
    xg = z[:, 3 * HG_WIDTH:]
    q = _silu(xq) * (HG_DK ** -0.5)
    f = lb + (1.0 - lb) * _sigmoid(xf)
    return q, 1.0 - f, f, xi, _silu(xg)


def _hgrn_chunk_kernel(z_ref, lbl_ref, gout_ref, o_ref, sfin_ref, st_ref, *, layer):
    c = pl.program_id(1)
    group, n = z_ref.shape[0], z_ref.shape[1]

    @pl.when(c == 0)
    def _():
        st_ref[...] = jnp.zeros_like(st_ref)

    lb = _lower_bound(lbl_ref[...], layer)
    row = lax.broadcasted_iota(jnp.int32, (n, n), 0)
    col = lax.broadcasted_iota(jnp.int32, (n, n), 1)
    tri = jnp.where(row >= col, 1.0, 0.0).astype(BF16)
    trow = lax.broadcasted_iota(jnp.int32, (n, 1), 0)
    spans = []
    w = HG_BLOCK
    while 2 * w <= n:
        span = 2 * w
        spans.append((w, span, (trow % span) >= w,
                      ((row // span) == (col // span)) & ((row % span) >= w) & ((col % span) < w)))
        w = span
    nears = [((row - col) == d) & ((row % HG_BLOCK) >= d) for d in range(HG_BLOCK)]

    for r in range(group):
        q, k, f, v, gate = _hgrn_gates(z_ref[r], lb)
        b = _dot_exact_rhs(tri, jnp.log2(f))
        a = [jnp.zeros((n, n), F32) for _ in range(HG_HEADS)]
        for w, span, right, lvl in spans:
            ref = jnp.concatenate(
                [jnp.broadcast_to(b[p * span + w - 1:p * span + w, :], (span, HG_WIDTH))
                 for p in range(n // span)], axis=0)
            e = jnp.exp2(-jnp.abs(b - ref))
            qw = jnp.where(right, q * e, 0.0).astype(BF16)
            kw = jnp.where(right, 0.0, k * e).astype(BF16)
            for h in range(HG_HEADS):
                hs = slice(h * HG_DK, (h + 1) * HG_DK)
                p = lax.dot_general(qw[:, hs], kw[:, hs], _NT, preferred_element_type=F32)
                a[h] = jnp.where(lvl, p, a[h])
        for d in range(HG_BLOCK):
            kd = pltpu.roll(k, d, axis=0) if d else k
            bd = pltpu.roll(b, d, axis=0) if d else b
            p = q * kd * jnp.exp2(jnp.minimum(b - bd, 0.0))
            for h in range(HG_HEADS):
                hs = slice(h * HG_DK, (h + 1) * HG_DK)
                a[h] = jnp.where(nears[d], jnp.sum(p[:, hs], axis=-1, keepdims=True), a[h])

        b_last = b[n - 1:n, :]
        q_in = (q * jnp.exp2(b)).astype(BF16)
        k_out = (k * jnp.exp2(b_last - b)).astype(BF16)
        carry = jnp.exp2(b_last)
        vb = v.astype(BF16)
        for h in range(HG_HEADS):
            hs = slice(h * HG_DK, (h + 1) * HG_DK)
            st = st_ref[r, h]
            o = (jnp.dot(a[h].astype(BF16), vb[:, hs], preferred_element_type=F32)
                 + lax.dot_general(q_in[:, hs], st.astype(BF16), _NT, preferred_element_type=F32))
            st_new = st * carry[:, hs] + lax.dot_general(vb[:, hs], k_out[:, hs], _TN,
                                                         preferred_element_type=F32)
            st_ref[r, h] = st_new
            o_ref[r, :, hs] = (_rmsnorm(o, gout_ref[...]) * gate[:, hs]).astype(BF16)

    @pl.when(c == pl.num_programs(1) - 1)
    def _():
        for r in range(group):
            for h in range(HG_HEADS):
                sfin_ref[r, h] = st_ref[r, h].T


def _hgrn_prompt(zhg, lb_logits, gout, batch, layer):
    n = zhg.shape[0]
    seq = n // batch
    nc = seq // HG_CHUNK
    group = HG_GROUP if batch % HG_GROUP == 0 else 1
    o, s_fin = pl.pallas_call(
        functools.partial(_hgrn_chunk_kernel, layer=layer),
        grid=(batch // group, nc),
        in_specs=[pl.BlockSpec((group, HG_CHUNK, HG_COLS), lambda g, c: (g, c, 0)),
                  pl.BlockSpec(lb_logits.shape, lambda g, c: (0, 0)),
                  pl.BlockSpec((1, HG_DV), lambda g, c: (0, 0))],
        out_specs=[pl.BlockSpec((group, HG_CHUNK, HG_WIDTH), lambda g, c: (g, c, 0)),
                   pl.BlockSpec((group, HG_HEADS, HG_DK, HG_DV), lambda g, c: (g, 0, 0, 0))],
        out_shape=[jax.ShapeDtypeStruct((batch, seq, HG_WIDTH), BF16),
                   jax.ShapeDtypeStruct((batch, HG_HEADS, HG_DK, HG_DV), F32)],
        scratch_shapes=[pltpu.VMEM((group, HG_HEADS, HG_DV, HG_DK), F32)],
        compiler_params=pltpu.CompilerParams(dimension_semantics=("arbitrary", "arbitrary")),
        name="hgrn_prompt",
    )(zhg.reshape(batch, seq, HG_COLS), lb_logits, gout)
    return o.reshape(n, HG_WIDTH), s_fin


HG_STEP_GROUP = 8


def _hgrn_step_kernel(z_ref, s_ref, lbl_ref, gout_ref, o_ref, snew_ref, *, layer):
    lb = _lower_bound(lbl_ref[...], layer)
    for r in range(z_ref.shape[0]):
        q, k, f, v, gate = _hgrn_gates(z_ref[r], lb)
        outs = []
        for h in range(HG_HEADS):
            hs = slice(h * HG_DK, (h + 1) * HG_DK)
            s_new = _col_bcast(f[:, hs]) * s_ref[r, h] + _col_bcast(k[:, hs]) * v[:, hs]
            snew_ref[r, h] = s_new
            o = jnp.sum(_col_bcast(q[:, hs]) * s_new, axis=0, keepdims=True)
            outs.append(_rmsnorm(o, gout_ref[...]) * gate[:, hs])
        o_ref[r] = jnp.concatenate(outs, axis=-1).astype(BF16)


def _hgrn_sample(zhg, state, lb_logits, gout, layer):
    nb = zhg.shape[0]
    grp = HG_STEP_GROUP if nb % HG_STEP_GROUP == 0 else 1
    return pl.pallas_call(
        functools.partial(_hgrn_step_kernel, layer=layer),
        grid=(nb // grp,),
        in_specs=[pl.BlockSpec((grp, 1, HG_COLS), lambda b: (b, 0, 0)),
                  pl.BlockSpec((grp, HG_HEADS, HG_DK, HG_DV), lambda b: (b, 0, 0, 0)),
                  pl.BlockSpec(lb_logits.shape, lambda b: (0, 0)),
                  pl.BlockSpec((1, HG_DV), lambda b: (0, 0))],
        out_specs=[pl.BlockSpec((grp, 1, HG_WIDTH), lambda b: (b, 0, 0)),
                   pl.BlockSpec((grp, HG_HEADS, HG_DK, HG_DV), lambda b: (b, 0, 0, 0))],
        out_shape=[jax.ShapeDtypeStruct((nb, 1, HG_WIDTH), BF16),
                   jax.ShapeDtypeStruct(state.shape, F32)],
        compiler_params=pltpu.CompilerParams(dimension_semantics=("arbitrary",)),
        name="hgrn_sample",
    )(zhg.reshape(nb, 1, HG_COLS), state, lb_logits, gout)


ATT_TILE = 512


def _head_slope(h):
    return jnp.exp2(jnp.full((1, 1), -8.0 / DA_HEADS, F32) * (h + 1).astype(F32))


def _diff_combine(acc1, l1, acc2, l2, lam, gsub, lam_init):
    out = acc1 / l1 - lam * (acc2 / l2)
    return _rmsnorm(out, gsub) * (1.0 - lam_init)


ATT_VROWS = DA_DV + 16


def _attn_prompt_kernel(q_ref, k_ref, v_ref, lq1_ref, lk1_ref, lq2_ref, lk2_ref, gsub_ref, o_ref,
                        kaug_ref, vt_ref, m_ref, acc_ref, p_ref, alpha_ref, gcol_ref, *, lam_init):
    qi = pl.program_id(2)
    t = q_ref.shape[0]
    nt = kaug_ref.shape[1]
    heads = vt_ref.shape[0]
    units = [(hh, i) for hh in range(heads) for i in range(2)]
    slopes = [_head_slope(pl.program_id(1) * heads + hh) for hh in range(heads)]
    lane = lax.broadcasted_iota(jnp.int32, (t, 2 * DA_DH), 1)
    loc = lax.broadcasted_iota(jnp.int32, (t, 2 * DA_DH), 0)
    loc_lo = (loc % 256).astype(F32)
    loc_hi = (loc - loc % 256).astype(F32)
    slot = [(1 - i) * DA_DH for i in range(2)]

    def augment(x, i, extras):
        out = jnp.zeros_like(x)
        for n, e in enumerate(extras):
            out = jnp.where(lane == slot[i] + n, e, out)
        return jnp.where((lane // DA_DH) == i, x, out)

    @pl.when(qi == 0)
    def _():
        for j in range(nt):
            for u, (hh, i) in enumerate(units):
                k = k_ref[j * t:(j + 1) * t, hh * 2 * DA_DH:(hh + 1) * 2 * DA_DH].astype(F32)
                kaug_ref[u, j] = augment(k, i, (slopes[hh] * loc_lo, slopes[hh] * loc_hi, 1.0, 1.0)
                                         ).astype(BF16)
            for hh in range(heads):
                head = pl.program_id(1) * heads + hh
                vj = v_ref[pl.ds(j * t * DA_HEADS + head, t, stride=DA_HEADS), :]
                vt_ref[hh, j, :DA_DV, :] = vj.T.astype(BF16)
                vt_ref[hh, j, DA_DV:, :] = jnp.ones((ATT_VROWS - DA_DV, t), BF16)
        gcol_ref[...] = _col_bcast(gsub_ref[...], t)

    qts = []
    for hh, i in units:
        q = q_ref[:, hh * 2 * DA_DH:(hh + 1) * 2 * DA_DH].astype(F32)
        qts.append(augment(q, i, (1.0, 1.0, -slopes[hh] * loc_lo, -slopes[hh] * loc_hi)
                           ).T.astype(BF16))
    m_ref[...] = jnp.full_like(m_ref, NEG_INF)
    acc_ref[...] = jnp.zeros_like(acc_ref)
    causal = (lax.broadcasted_iota(jnp.int32, (t, t), 0)
              <= lax.broadcasted_iota(jnp.int32, (t, t), 1))

    def values(ki, slot):
        for u, (hh, _) in enumerate(units):
            acc_ref[u] = (alpha_ref[slot, u] * acc_ref[u]
                          + jnp.dot(vt_ref[hh, ki], p_ref[slot, u], preferred_element_type=F32))

    def step(ki, slot, keep=None, first=False):
        ss = [jnp.dot(kaug_ref[u, ki], qts[u], preferred_element_type=F32)
              for u in range(len(units))]
        if not first:
            values(ki - 1, 1 - slot)
        for u, (hh, _) in enumerate(units):
            shift = -slopes[hh] * ((qi - ki) * t).astype(F32)
            s = ss[u] if keep is None else jnp.where(keep, ss[u], NEG_INF)
            m_old = m_ref[u]
            m_new = jnp.maximum(m_old, jnp.max(s, axis=0, keepdims=True) + shift)
            alpha_ref[slot, u] = jnp.exp(m_old - m_new)
            p_ref[slot, u] = jnp.exp(s - (m_new - shift)).astype(BF16)
            m_ref[u] = m_new

    pl.when(qi == 0)(lambda: step(0, 0, keep=causal, first=True))
    pl.when(qi > 0)(lambda: step(0, 0, first=True))
    inner = qi - 1

    def pair(j, carry):
        step(1 + 2 * j, 1)
        step(2 + 2 * j, 0)
        return carry

    lax.fori_loop(0, inner // 2, pair, 0)
    for parity in range(2):
        @pl.when(jnp.logical_and(qi > 0, qi % 2 == parity))
        def _():
            if parity == 0:
                step(qi - 1, 1)
            step(qi, parity, keep=causal)

        @pl.when(qi % 2 == parity)
        def _():
            values(qi, parity)

    lam = _lambda(lq1_ref, lk1_ref, lq2_ref, lk2_ref, lam_init)
    for hh in range(heads):
        a1, a2 = acc_ref[2 * hh], acc_ref[2 * hh + 1]
        inv1 = 1.0 / a1[DA_DV:DA_DV + 1]
        inv2 = lam / a2[DA_DV:DA_DV + 1]
        out = a1[:DA_DV] * inv1 - a2[:DA_DV] * inv2
        ms = jnp.mean(out * out, axis=0, keepdims=True)
        out = out * (lax.rsqrt(ms + NORM_EPS) * (1.0 - lam_init)) * gcol_ref[...]
        o_ref[:, hh * DA_DV:(hh + 1) * DA_DV] = out.T.astype(BF16)


ATT_HEADS = 4


def _attn_prompt(qb, kb, v, lams, gsub, batch, lam_init):
    n = qb.shape[0]
    seq = n // batch
    nq = seq // ATT_TILE
    units = 2 * ATT_HEADS
    small = lambda a: pl.BlockSpec(a.shape, lambda b, h, i: (0, 0))
    return pl.pallas_call(
        functools.partial(_attn_prompt_kernel, lam_init=lam_init),
        grid=(batch, DA_HEADS // ATT_HEADS, nq),
        in_specs=[pl.BlockSpec((ATT_TILE, ATT_HEADS * 2 * DA_DH), lambda b, h, i: (b * nq + i, h)),
                  pl.BlockSpec((seq, ATT_HEADS * 2 * DA_DH), lambda b, h, i: (b, h)),
                  pl.BlockSpec((seq * DA_HEADS, DA_DV), lambda b, h, i: (b, 0))]
                 + [small(a) for a in lams] + [small(gsub)],
        out_specs=pl.BlockSpec((ATT_TILE, ATT_HEADS * DA_DV), lambda b, h, i: (b * nq + i, h)),
        out_shape=jax.ShapeDtypeStruct((n, DA_HEADS * DA_DV), BF16),
        scratch_shapes=[pltpu.VMEM((units, nq, ATT_TILE, 2 * DA_DH), BF16),
                        pltpu.VMEM((ATT_HEADS, nq, ATT_VROWS, ATT_TILE), BF16),
                        pltpu.VMEM((units, 1, ATT_TILE), F32),
                        pltpu.VMEM((units, ATT_VROWS, ATT_TILE), F32),
                        pltpu.VMEM((2, units, ATT_TILE, ATT_TILE), BF16),
                        pltpu.VMEM((2, units, 1, ATT_TILE), F32),
                        pltpu.VMEM((DA_DV, ATT_TILE), F32)],
        compiler_params=pltpu.CompilerParams(
            dimension_semantics=("arbitrary", "arbitrary", "arbitrary"),
            vmem_limit_bytes=VMEM_LIMIT),
        name="attn_prompt",
    )(qb, kb, v, *lams, gsub)


PROMPT_ROWS = 256


def kernel(x_prompt, x_sample, cache_k, cache_v, state_hgrn, page_table, ffn1_norm, ffn1_w_gate, ffn1_w_up, ffn1_w_down, mix_norm, w_in, hg_lb_logits, hg_out_norm, da_q_norm, da_k_norm, da_lambda_q1, da_lambda_k1, da_lambda_q2, da_lambda_k2, da_subln, w_out, ffn2_norm, ffn2_w_gate, ffn2_w_up, ffn2_w_down):
    batch, seq, d = x_prompt.shape
    nb = x_sample.shape[0]
    depth = ffn1_norm.shape[0]
    nmap = 2 * DA_HEADS
    yp = x_prompt.reshape(batch * seq, d)
    ys = x_sample.reshape(nb, d)
    kt_cache = jnp.transpose(cache_k, (0, 1, 3, 4, 2))
    v_cache = cache_v.reshape(cache_v.shape[:2] + (-1, DA_DV))
    outs = [[] for _ in range(6)]
    for l in range(depth):
        lam_init = 0.8 - 0.6 * math.exp(-0.3 * l)
        bf = lambda w: w[l].astype(BF16)
        w1 = (bf(ffn1_w_gate), bf(ffn1_w_up), bf(ffn1_w_down))
        w2 = (bf(ffn2_w_gate), bf(ffn2_w_up), bf(ffn2_w_down))
        win, wo = bf(w_in), bf(w_out)
        gq = jnp.tile(da_q_norm[l:l + 1], (1, nmap))
        gk = jnp.tile(da_k_norm[l:l + 1], (1, nmap))
        lams = (da_lambda_q1[l:l + 1], da_lambda_k1[l:l + 1], da_lambda_q2[l:l + 1],
                da_lambda_k2[l:l + 1])
        gout, gsub = hg_out_norm[l:l + 1], da_subln[l:l + 1]

        def inproj(x, tm, rows_per_seq, pages=None):
            return _ffn_inproj(x, ffn1_norm[l:l + 1], *w1, mix_norm[l:l + 1], win, gq, gk, tm,
                               rows_per_seq, pages)

        def outproj(h, ohg, oda, tm, pages=None):
            return _outproj_ffn(h, ohg, oda, wo, ffn2_norm[l:l + 1], *w2, tm, pages)

        def keys(kt):
            return jnp.transpose(kt.reshape(kt.shape[0], nmap, DA_DH, kt.shape[2]), (0, 3, 1, 2))

        hs, zhg_s, qs, kt_s, _, vs = inproj(ys, nb, nb)
        ks = keys(kt_s).reshape(nb, DA_WIDTH)
        ohg_s, s_new = _hgrn_sample(zhg_s, state_hgrn[l], hg_lb_logits, gout, l)
        kt_pages, v_pages = kt_cache[l:l + 1], v_cache[l:l + 1]
        first = page_table.shape[1] // PAGE_SHARE_FIRST
        h, zhg, qb, kt, kb, v, m_s, l_s, acc_s = inproj(
            yp, PROMPT_ROWS, seq, (page_table, qs, ks, vs, kt_pages, v_pages, first))
        ohg, s_fin = _hgrn_prompt(zhg, hg_lb_logits, gout, batch, l)
        oda = _attn_prompt(qb, kb, v, lams, gsub, batch, lam_init)
        yp, oda_s = outproj(h, ohg, oda, PROMPT_ROWS, (page_table, qs, m_s, l_s, acc_s, lams, gsub,
                                                       kt_pages, v_pages, first, lam_init))
        outs[0].append(keys(kt))
        outs[1].append(v.reshape(batch, seq, DA_HEADS, DA_DV))
        outs[2].append(s_fin)
        ys = outproj(hs, ohg_s.reshape(nb, HG_WIDTH), oda_s, nb)
        outs[3].append(ks.reshape(nb, 1, nmap, DA_DH))
        outs[4].append(vs.reshape(nb, 1, DA_HEADS, DA_DV))
        outs[5].append(s_new)
    stacked = [jnp.stack(o, axis=0) for o in outs]
    return (yp.reshape(batch, seq, d), ys.reshape(nb, 1, d), *stacked)
```
